```python
import jax, jax.numpy as jnp
from jax import lax
import numpy as np

D_MODEL = 2048
BATCH = 4
SEQ = 2048
DEPTH = 1
DEC_BATCH = 128
DEC_SEQ = 4
PAST_LEN = 16384
PAGE_SIZE = 128

D_MIX = D_MODEL
CONV_CH = D_MIX // 2
HGRN_WIDTH = D_MIX - CONV_CH
CONV_WIDTH = 31
HGRN_HEAD_K = 128
HGRN_HEAD_V = 128
HGRN_HEADS = HGRN_WIDTH // HGRN_HEAD_K
HGRN_CHUNK = 64
N_GROUPS = 4
EXPERTS_PER_GROUP = 8
N_EXPERTS = N_GROUPS * EXPERTS_PER_GROUP
TOP_K_IN_GROUP = 2
D_EXPERT = D_MODEL // 4
EPS = 1e-6
IN_COLS = 2 * CONV_CH + 2 * HGRN_HEADS * HGRN_HEAD_K + 2 * HGRN_HEADS * HGRN_HEAD_V

kernel_name = "hybrid_conformerconv_hgrn2_hmoe_step"


def _rmsnorm(x, g):
    xf = x.astype(jnp.float32)
    y = xf * lax.rsqrt(jnp.mean(xf * xf, axis=-1, keepdims=True) + EPS)
    return (y * g.astype(jnp.float32)).astype(x.dtype)


def _layernorm(x, g, b):
    xf = x.astype(jnp.float32)
    mu = jnp.mean(xf, axis=-1, keepdims=True)
    var = jnp.mean(jnp.square(xf - mu), axis=-1, keepdims=True)
    y = (xf - mu) * lax.rsqrt(var + EPS)
    return (y * g.astype(jnp.float32) + b.astype(jnp.float32)).astype(x.dtype)


def _hgrn2_recurrence(q, k, v, logf, s0):
    B, H, T, DK = q.shape
    DV = v.shape[-1]
    C = HGRN_CHUNK if T % HGRN_CHUNK == 0 else T
    n = T // C

    def to_chunks(t):
        return jnp.moveaxis(t.reshape(B, H, n, C, t.shape[-1]), 2, 0)

    causal = jnp.tril(jnp.ones((C, C), dtype=bool))[:, :, None]

    def step(S, inp):
        qc, kc, vc, lfc = inp
        bc = jnp.cumsum(lfc, axis=-2)
        rel = bc[..., :, None, :] - bc[..., None, :, :]
        decay = jnp.where(causal, jnp.exp(jnp.where(causal, rel, 0.0)), 0.0)
        att = jnp.einsum('bhtd,bhsd,bhtsd->bhts', qc, kc, decay)
        o = jnp.einsum('bhts,bhsv->bhtv', att, vc) + jnp.einsum('bhtd,bhdv->bhtv', qc * jnp.exp(bc), S)
        blast = bc[..., -1:, :]
        S_new = jnp.exp(blast[..., 0, :])[..., None] * S + jnp.einsum('bhsd,bhsv->bhdv', kc * jnp.exp(blast - bc), vc)
        return S_new, o

    S_T, o = lax.scan(step, s0, (to_chunks(q), to_chunks(k), to_chunks(v), to_chunks(logf)))
    o = jnp.moveaxis(o, 0, 2).reshape(B, H, T, DV)
    return o, S_T


def _mixer(h, conv_buf, s0, w_in, w_dw, b_dw, ln_g, ln_b, lb, g_norm, w_out):
    B, T, _ = h.shape
    z = jnp.einsum('btd,dc->btc', h, w_in)
    c1 = CONV_CH
    c2 = 2 * CONV_CH
    c3 = c2 + HGRN_HEADS * HGRN_HEAD_K
    c4 = c3 + HGRN_HEADS * HGRN_HEAD_K
    c5 = c4 + HGRN_HEADS * HGRN_HEAD_V
    a, ga, q, f, i, g = jnp.split(z, [c1, c2, c3, c4, c5], axis=-1)

    u = a * jax.nn.sigmoid(ga)
    ext = jnp.concatenate([conv_buf.astype(u.dtype), u], axis=1)
    c = lax.conv_general_dilated(ext, w_dw[:, None, :].astype(ext.dtype), (1,), 'VALID',
                                 dimension_numbers=('NWC', 'WIO', 'NWC'),
                                 feature_group_count=CONV_CH) + b_dw.astype(ext.dtype)
    new_buf = ext[:, -(CONV_WIDTH - 1):]
    c = jax.nn.silu(_layernorm(c, ln_g, ln_b))

    def heads(t, d):
        return t.reshape(B, T, HGRN_HEADS, d).transpose(0, 2, 1, 3)
    fg = lb + (1.0 - lb) * jax.nn.sigmoid(f.astype(jnp.float32))
    qh = heads(jax.nn.silu(q.astype(jnp.float32)), HGRN_HEAD_K)
    kh = heads(1.0 - fg, HGRN_HEAD_K)
    lfh = heads(jnp.log(fg), HGRN_HEAD_K)
    vh = heads(i.astype(jnp.float32), HGRN_HEAD_V)
    o, s_new = _hgrn2_recurrence(qh, kh, vh, lfh, s0.astype(jnp.float32))
    o = _rmsnorm(o, g_norm)
    o = o.transpose(0, 2, 1, 3).reshape(B, T, HGRN_HEADS * HGRN_HEAD_V).astype(h.dtype) * jax.nn.silu(g)

    y = jnp.einsum('btc,cd->btd', jnp.concatenate([c, o], axis=-1), w_out)
    return y, new_buf, s_new


def _hier_moe(h, w_rg, b_rg, w_re, b_re, w_gate, w_up, w_down):
    B, T, D = h.shape
    x = h.reshape(B * T, D)
    gprob = jax.nn.softmax((x @ w_rg + b_rg).astype(jnp.float32), axis=-1)
    p_top, g_idx = lax.top_k(gprob, 1)
    elog = (x @ w_re + b_re).astype(jnp.float32).reshape(-1, N_GROUPS, EXPERTS_PER_GROUP)
    elog_sel = jnp.take_along_axis(elog, g_idx[:, :, None], axis=1)[:, 0]
    e_val, e_idx = lax.top_k(elog_sel, TOP_K_IN_GROUP)
    e_w = jax.nn.softmax(e_val, axis=-1) * p_top
    ids = g_idx * EXPERTS_PER_GROUP + e_idx
    combine = jnp.sum(jax.nn.one_hot(ids, N_EXPERTS, dtype=jnp.float32) * e_w[..., None], axis=1)
    hid = jax.nn.silu(jnp.einsum('nd,edf->nef', x, w_gate)) * jnp.einsum('nd,edf->nef', x, w_up)
    hid = hid * combine[..., None].astype(hid.dtype)
    y = jnp.einsum('nef,efd->nd', hid, w_down)
    return y.reshape(B, T, D)


def setup_inputs(seed: int = 0) -> dict:
    key = jax.random.key(seed)
    ks = jax.random.split(key, 24)

    def nrm(k, shape, scale):
        return jax.random.normal(k, shape, jnp.float32) * scale

    return {
        'x_prompt': nrm(ks[0], (BATCH, SEQ, D_MODEL), 1.0),
        'x_sample': nrm(ks[1], (DEC_BATCH, DEC_SEQ, D_MODEL), 1.0),
        'state_conv': nrm(ks[2], (DEPTH, DEC_BATCH, CONV_WIDTH - 1, CONV_CH), 0.5),
        'state_hgrn': nrm(ks[3], (DEPTH, DEC_BATCH, HGRN_HEADS, HGRN_HEAD_K, HGRN_HEAD_V), 0.1),
        'norm_mix': 1.0 + nrm(ks[4], (DEPTH, D_MODEL), 0.02),
        'w_in': nrm(ks[5], (DEPTH, D_MODEL, IN_COLS), D_MODEL ** -0.5),
        'w_dw': nrm(ks[6], (DEPTH, CONV_WIDTH, CONV_CH), CONV_WIDTH ** -0.5),
        'b_dw': nrm(ks[7], (DEPTH, CONV_CH), 0.02),
        'ln_conv_g': 1.0 + nrm(ks[8], (DEPTH, CONV_CH), 0.02),
        'ln_conv_b': nrm(ks[9], (DEPTH, CONV_CH), 0.02),
        'lb_logits': nrm(ks[10], (DEPTH + 1, HGRN_HEADS * HGRN_HEAD_K), 0.5),
        'hgrn_norm_g': 1.0 + nrm(ks[11], (DEPTH, HGRN_HEAD_V), 0.02),
        'w_out': nrm(ks[12], (DEPTH, D_MIX, D_MODEL), D_MIX ** -0.5),
        'norm_ffn': 1.0 + nrm(ks[13], (DEPTH, D_MODEL), 0.02),
        'w_router_group': nrm(ks[14], (DEPTH, D_MODEL, N_GROUPS), D_MODEL ** -0.5),
        'b_router_group': nrm(ks[15], (DEPTH, N_GROUPS), 0.01),
        'w_router_expert': nrm(ks[16], (DEPTH, D_MODEL, N_EXPERTS), D_MODEL ** -0.5),
        'b_router_expert': nrm(ks[17], (DEPTH, N_EXPERTS), 0.01),
        'w_exp_gate': nrm(ks[18], (DEPTH, N_EXPERTS, D_MODEL, D_EXPERT), D_MODEL ** -0.5),
        'w_exp_up': nrm(ks[19], (DEPTH, N_EXPERTS, D_MODEL, D_EXPERT), D_MODEL ** -0.5),
        'w_exp_down': nrm(ks[20], (DEPTH, N_EXPERTS, D_EXPERT, D_MODEL), D_EXPERT ** -0.5),
        'norm_final': 1.0 + nrm(ks[21], (D_MODEL,), 0.02),
    }


def reference(x_prompt, x_sample, state_conv, state_hgrn, norm_mix, w_in, w_dw, b_dw, ln_conv_g, ln_conv_b,
              lb_logits, hgrn_norm_g, w_out, norm_ffn, w_router_group, b_router_group, w_router_expert,
              b_router_expert, w_exp_gate, w_exp_up, w_exp_down, norm_final):
    lb_all = jnp.cumsum(jax.nn.softmax(lb_logits.astype(jnp.float32), axis=0), axis=0)
    xp, xs = x_prompt, x_sample
    conv_p, hgrn_p, conv_s, hgrn_s = [], [], [], []
    for l in range(DEPTH):
        mix_w = (w_in[l], w_dw[l], b_dw[l], ln_conv_g[l], ln_conv_b[l], lb_all[l], hgrn_norm_g[l], w_out[l])
        moe_w = (w_router_group[l], b_router_group[l], w_router_expert[l], b_router_expert[l],
                 w_exp_gate[l], w_exp_up[l], w_exp_down[l])
        buf0 = jnp.zeros((xp.shape[0], CONV_WIDTH - 1, CONV_CH), xp.dtype)
        s00 = jnp.zeros((xp.shape[0], HGRN_HEADS, HGRN_HEAD_K, HGRN_HEAD_V), jnp.float32)
        yp, bp, sp = _mixer(_rmsnorm(xp, norm_mix[l]), buf0, s00, *mix_w)
        xp = xp + yp
        xp = xp + _hier_moe(_rmsnorm(xp, norm_ffn[l]), *moe_w)
        ys, bs, ss = _mixer(_rmsnorm(xs, norm_mix[l]), state_conv[l], state_hgrn[l], *mix_w)
        xs = xs + ys
        xs = xs + _hier_moe(_rmsnorm(xs, norm_ffn[l]), *moe_w)
        conv_p.append(bp)
        hgrn_p.append(sp)
        conv_s.append(bs)
        hgrn_s.append(ss)
    y_prompt = _rmsnorm(xp, norm_final)
    y_sample = _rmsnorm(xs, norm_final)
    new_conv_prompt = jnp.stack(conv_p, axis=0)
    new_hgrn_prompt = jnp.stack(hgrn_p, axis=0)
    new_conv_sample = jnp.stack(conv_s, axis=0)
    new_hgrn_sample = jnp.stack(hgrn_s, axis=0)
    return (y_prompt, y_sample, new_conv_prompt, new_hgrn_prompt, new_conv_sample, new_hgrn_sample)
```

```python
import functools

import numpy as np
import jax
import jax.numpy as jnp
from jax import lax
from jax.experimental import pallas as pl
from jax.experimental.pallas import tpu as pltpu

F32 = jnp.float32
BF16 = jnp.bfloat16
EPS = 1e-6
LANES = 128
SUBLANES = 8
HEAD_DIM = 128
HGRN_CHUNK = 64
N_GROUPS = 4
EXPERTS_PER_GROUP = 8
VMEM_LIMIT = 56 * 1024 * 1024
TOKEN_TILE = 256
INPROJ_TILE = 256
CONV_TILE = 256
CONV_HALO = 32
CONV_ROWS = 64
EXPERT_TILE = 256
SAMPLE_BATCH_BLOCK = 8


def _sigmoid(x):
    return 1.0 / (1.0 + jnp.exp(-x))


def _silu(x):
    return x * _sigmoid(x)


def _rms(x, g):
    return x * lax.rsqrt(jnp.mean(x * x, axis=-1, keepdims=True) + EPS) * g


def _split3(x):
    h1 = x.astype(BF16)
    r1 = x - h1.astype(F32)
    h2 = r1.astype(BF16)
    h3 = (r1 - h2.astype(F32)).astype(BF16)
    return h1, h2, h3


def _params(sem):
    return pltpu.CompilerParams(dimension_semantics=sem, vmem_limit_bytes=VMEM_LIMIT)


def _inproj_kernel(xp_ref, xs_ref, g_ref, w_ref, lbl_ref,
                   u_ref, q_ref, k_ref, lf_ref, v_ref, gs_ref, h_ref, *, n_prompt_tiles, ch, hk):
    i = pl.program_id(0)

    @pl.when(i < n_prompt_tiles)
    def _():
        h_ref[...] = _rms(xp_ref[...], g_ref[...]).astype(BF16)

    @pl.when(i >= n_prompt_tiles)
    def _():
        h_ref[...] = _rms(xs_ref[...], g_ref[...]).astype(BF16)

    h = h_ref[...]

    def proj(c0, width):
        return jnp.dot(h, w_ref[:, c0:c0 + width], preferred_element_type=F32)

    a = proj(0, ch)
    ga = proj(ch, ch)
    u_ref[...] = a * _sigmoid(ga)
    q = proj(2 * ch, hk)
    q_ref[...] = _silu(q).astype(BF16)
    f = proj(2 * ch + hk, hk)
    lbl = lbl_ref[...]
    e = jnp.exp(lbl - jnp.max(lbl, axis=0, keepdims=True))
    lb = e[0:1, :] / jnp.sum(e, axis=0, keepdims=True)
    fg = lb + (1.0 - lb) * _sigmoid(f)
    k_ref[...] = (1.0 - fg).astype(BF16)
    lf_ref[...] = jnp.log(fg)
    v_ref[...] = proj(2 * ch + 2 * hk, hk).astype(BF16)
    g = proj(2 * ch + 3 * hk, hk)
    gs_ref[...] = _silu(g).astype(BF16)


def _inproj(xp, xs, norm_g, w_in_bf16, lb_logits, ch, hk):
    n_p, d = xp.shape
    n_s = xs.shape[0]
    tm = min(INPROJ_TILE, n_s)
    assert n_p % tm == 0 and n_s % tm == 0
    npt, nst = n_p // tm, n_s // tm
    n = n_p + n_s
    cols = w_in_bf16.shape[1]
    row = lambda i: (i, 0)
    const = lambda i: (0, 0)
    outs = [jax.ShapeDtypeStruct((n, ch), F32)] + [
        jax.ShapeDtypeStruct((n, hk), dt) for dt in (BF16, BF16, F32, BF16, BF16)]
    return pl.pallas_call(
        functools.partial(_inproj_kernel, n_prompt_tiles=npt, ch=ch, hk=hk),
        grid=(npt + nst,),
        in_specs=[
            pl.BlockSpec((tm, d), lambda i: (jnp.minimum(i, npt - 1), 0)),
            pl.BlockSpec((tm, d), lambda i: (jnp.maximum(i - npt, 0), 0)),
            pl.BlockSpec((1, d), const),
            pl.BlockSpec((d, cols), const, pipeline_mode=pl.Buffered(1)),
            pl.BlockSpec(lb_logits.shape, const),
        ],
        out_specs=[pl.BlockSpec((tm, ch), row)] + [pl.BlockSpec((tm, hk), row)] * 5,
        out_shape=outs,
        scratch_shapes=[pltpu.VMEM((tm, d), BF16)],
        compiler_params=_params(("arbitrary",)),
        name="inproj",
    )(xp, xs, norm_g, w_in_bf16, lb_logits)


def _ln_silu(c, g, b):
    mu = jnp.mean(c, axis=-1, keepdims=True)
    d = c - mu
    var = jnp.mean(d * d, axis=-1, keepdims=True)
    return _silu(d * lax.rsqrt(var + EPS) * g + b)


def _conv_prompt_kernel(halo_ref, cur_ref, w_ref, b_ref, lg_ref, lb_ref, c_ref, ext_ref, acc_ref, *, width):
    t = pl.program_id(1)
    tt, ch = cur_ref.shape
    halo = halo_ref[...]
    ext_ref[0:CONV_HALO, :] = jnp.where(t == 0, jnp.zeros_like(halo), halo)
    ext_ref[CONV_HALO:, :] = cur_ref[...]
    off = CONV_HALO - (width - 1)
    rows = min(CONV_ROWS, tt)
    for l in range(ch // LANES):
        ls = slice(l * LANES, (l + 1) * LANES)
        wl = w_ref[:, ls]
        bl = b_ref[:, ls]
        for r0 in range(0, tt, rows):
            acc = jnp.broadcast_to(bl, (rows, LANES))
            for j in range(width):
                acc = acc + ext_ref[r0 + off + j:r0 + off + j + rows, ls] * wl[j:j + 1, :]
            acc_ref[r0:r0 + rows, ls] = acc
    c_ref[...] = _ln_silu(acc_ref[...], lg_ref[...], lb_ref[...]).astype(BF16)


def _conv_prompt(u, n_batch, seq, w_dw, b_dw, ln_g, ln_b):
    ch = u.shape[1]
    width = w_dw.shape[0]
    tt = min(CONV_TILE, seq)
    assert seq % tt == 0 and tt % CONV_HALO == 0 and width - 1 <= CONV_HALO
    nt = seq // tt
    hb = tt // CONV_HALO
    const = lambda b, t: (0, 0)
    return pl.pallas_call(
        functools.partial(_conv_prompt_kernel, width=width),
        grid=(n_batch, nt),
        in_specs=[
            pl.BlockSpec((CONV_HALO, ch), lambda b, t: (jnp.maximum((b * nt + t) * hb - 1, 0), 0)),
            pl.BlockSpec((tt, ch), lambda b, t: (b * nt + t, 0)),
            pl.BlockSpec((width, ch), const),
            pl.BlockSpec((1, ch), const),
            pl.BlockSpec((1, ch), const),
            pl.BlockSpec((1, ch), const),
        ],
        out_specs=pl.BlockSpec((tt, ch), lambda b, t: (b * nt + t, 0)),
        out_shape=jax.ShapeDtypeStruct((n_batch * seq, ch), BF16),
        scratch_shapes=[pltpu.VMEM((CONV_HALO + tt, ch), F32), pltpu.VMEM((tt, ch), F32)],
        compiler_params=_params(("arbitrary", "arbitrary")),
        name="conv_prompt",
    )(u, u, w_dw, b_dw, ln_g, ln_b)


def _conv_sample_kernel(state_ref, u_ref, ws_ref, wu_ref, b_ref, lg_ref, lb_ref, c_ref, acc_ref, *, steps):
    bb = state_ref.shape[0]
    for b in range(bb):
        st = state_ref[b]
        ub = u_ref[b * steps:(b + 1) * steps, :]
        for t in range(steps):
            row = (jnp.sum(st * ws_ref[t], axis=0, keepdims=True)
                   + jnp.sum(ub * wu_ref[t], axis=0, keepdims=True))
            acc_ref[b * steps + t:b * steps + t + 1, :] = row + b_ref[...]
    c_ref[...] = _ln_silu(acc_ref[...], lg_ref[...], lb_ref[...]).astype(BF16)


def _conv_sample(u_s, state, w_dw, b_dw, ln_g, ln_b):
    n_batch, hist, ch = state.shape
    width = w_dw.shape[0]
    steps = u_s.shape[0] // n_batch
    assert hist == width - 1 and steps <= hist
    ws = jnp.stack([jnp.concatenate([jnp.zeros((t, ch), F32), w_dw[:hist - t]], axis=0) for t in range(steps)])
    wu = jnp.stack([jnp.concatenate([w_dw[hist - t:], jnp.zeros((steps - 1 - t, ch), F32)], axis=0)
                    for t in range(steps)])
    bb = min(SAMPLE_BATCH_BLOCK, n_batch)
    assert n_batch % bb == 0
    c3 = lambda i: (0, 0, 0)
    c2 = lambda i: (0, 0)
    return pl.pallas_call(
        functools.partial(_conv_sample_kernel, steps=steps),
        grid=(n_batch // bb,),
        in_specs=[
            pl.BlockSpec((bb, hist, ch), lambda i: (i, 0, 0)),
            pl.BlockSpec((bb * steps, ch), lambda i: (i, 0)),
            pl.BlockSpec((steps, hist, ch), c3),
            pl.BlockSpec((steps, steps, ch), c3),
            pl.BlockSpec((1, ch), c2),
            pl.BlockSpec((1, ch), c2),
            pl.BlockSpec((1, ch), c2),
        ],
        out_specs=pl.BlockSpec((bb * steps, ch), lambda i: (i, 0)),
        out_shape=jax.ShapeDtypeStruct((n_batch * steps, ch), BF16),
        scratch_shapes=[pltpu.VMEM((bb * steps, ch), F32)],
        compiler_params=_params(("arbitrary",)),
        name="conv_sample",
    )(state, u_s, ws, wu, b_dw, ln_g, ln_b)


def _chunk_constants(c):
    levels = int(np.log2(c))
    assert 2 ** levels == c
    t = np.arange(c)[:, None]
    j = np.arange(c)[None, :]
    mats = [(j <= t)]
    right = np.zeros((levels, c, LANES), np.float32)
    left = np.zeros((levels, c, LANES), np.float32)
    same = np.zeros((levels, c, c), np.float32)
    for lv in range(levels):
        half = c >> (lv + 1)
        split = (t // (2 * half)) * (2 * half) + half
        is_right = t >= split
        mats.append(np.where(is_right, (j >= split) & (j <= t), (j > t) & (j < split)))
        right[lv] = is_right
        left[lv] = ~is_right
        same[lv] = (t // (2 * half)) == (j // (2 * half))
    mats.append(j > t)
    dm = np.concatenate(mats, axis=0).astype(np.float32)
    return (jnp.asarray(dm, BF16), jnp.asarray(right), jnp.asarray(left), jnp.asarray(same))


def _hgrn_prompt_kernel(q_ref, k_ref, lf_ref, v_ref, gs_ref, gn_ref, dm_ref, rm_ref, lm_ref, bm_ref,
                        o_ref, s_ref, *, chunk):
    seq = q_ref.shape[0]
    levels = rm_ref.shape[0]
    nt = (((1,), (1,)), ((), ()))
    tn = (((0,), (0,)), ((), ()))
    eye = (lax.broadcasted_iota(jnp.int32, (chunk, chunk), 0)
           == lax.broadcasted_iota(jnp.int32, (chunk, chunk), 1)).astype(F32)

    def body(c, st):
        rows = pl.ds(pl.multiple_of(c * chunk, chunk), chunk)
        q = q_ref[rows, :].astype(F32)
        k = k_ref[rows, :].astype(F32)
        v = v_ref[rows, :]
        l1, l2, l3 = _split3(lf_ref[rows, :])
        dm = dm_ref[...]
        ex = jnp.exp(jnp.dot(dm, l1, preferred_element_type=F32)
                     + jnp.dot(dm, l2, preferred_element_type=F32)
                     + jnp.dot(dm, l3, preferred_element_type=F32))
        e_cum = ex[0:chunk]
        att = eye * jnp.sum(q * k, axis=-1, keepdims=True)
        for lv in range(levels):
            e = ex[(lv + 1) * chunk:(lv + 2) * chunk]
            ql = (q * e * rm_ref[lv]).astype(BF16)
            kl = (k * e * lm_ref[lv]).astype(BF16)
            att = att + bm_ref[lv] * lax.dot_general(ql, kl, nt, preferred_element_type=F32)
        o = jnp.dot(att.astype(BF16), v, preferred_element_type=F32)
        o = o + lax.dot_general((q * e_cum).astype(BF16), st.astype(BF16), nt, preferred_element_type=F32)
        k_tail = (k * ex[(levels + 1) * chunk:(levels + 2) * chunk]).astype(BF16)
        st = st * e_cum[chunk - 1:chunk, :] + lax.dot_general(v, k_tail, tn, preferred_element_type=F32)
        o = _rms(o, gn_ref[...]) * gs_ref[rows, :].astype(F32)
        o_ref[rows, :] = o.astype(BF16)
        return st

    st = lax.fori_loop(0, seq // chunk, body, jnp.zeros((HEAD_DIM, HEAD_DIM), F32))
    s_ref[0, 0] = st.T


def _hgrn_prompt(q, k, lf, v, gs, g_norm, n_batch, seq, heads):
    chunk = HGRN_CHUNK if seq % HGRN_CHUNK == 0 else seq
    dm, rm, lm, bm = _chunk_constants(chunk)
    tok = pl.BlockSpec((seq, HEAD_DIM), lambda b, h: (b, h))
    c2 = lambda b, h: (0, 0)
    c3 = lambda b, h: (0, 0, 0)
    return pl.pallas_call(
        functools.partial(_hgrn_prompt_kernel, chunk=chunk),
        grid=(n_batch, heads),
        in_specs=[tok, tok, tok, tok, tok,
                  pl.BlockSpec((1, HEAD_DIM), c2),
                  pl.BlockSpec(dm.shape, c2),
                  pl.BlockSpec(rm.shape, c3), pl.BlockSpec(lm.shape, c3), pl.BlockSpec(bm.shape, c3)],
        out_specs=[tok, pl.BlockSpec((1, 1, HEAD_DIM, HEAD_DIM), lambda b, h: (b, h, 0, 0))],
        out_shape=[jax.ShapeDtypeStruct((n_batch * seq, heads * HEAD_DIM), BF16),
                   jax.ShapeDtypeStruct((n_batch, heads, HEAD_DIM, HEAD_DIM), F32)],
        compiler_params=_params(("arbitrary", "arbitrary")),
        name="hgrn_prompt",
    )(q, k, lf, v, gs, g_norm, dm, rm, lm, bm)


def _hgrn_sample_kernel(q_ref, k_ref, lf_ref, v_ref, gs_ref, gn_ref, s0_ref, o_ref, s_ref, inter_ref, *, steps):
    rows, width = q_ref.shape
    heads = width // HEAD_DIM
    q = q_ref[...].astype(F32)
    k = k_ref[...].astype(F32)
    v = v_ref[...].astype(F32)
    lf = lf_ref[...]
    step = lax.broadcasted_iota(jnp.int32, (rows, 1), 0) & (steps - 1)

    def back(x, d):
        return pltpu.roll(x, d, 0)

    cum = lf
    for d in range(1, steps):
        cum = cum + jnp.where(step >= d, back(lf, d), 0.0)
    tail = jnp.zeros_like(lf)
    for d in range(1, steps):
        tail = tail + jnp.where(step + d < steps, pltpu.roll(lf, rows - d, 0), 0.0)

    def head_sum(x):
        return [jnp.sum(x[:, h * HEAD_DIM:(h + 1) * HEAD_DIM], axis=-1, keepdims=True) for h in range(heads)]

    def head_scale(cols, x):
        return jnp.concatenate([cols[h] * x[:, h * HEAD_DIM:(h + 1) * HEAD_DIM] for h in range(heads)], axis=1)

    intra = head_scale(head_sum(q * k), v)
    for d in range(1, steps):
        ok = step >= d
        rel = jnp.where(ok, cum - back(cum, d), 0.0)
        w = jnp.where(ok, q * back(k, d) * jnp.exp(rel), 0.0)
        intra = intra + head_scale(head_sum(w), back(v, d))

    qe = (q * jnp.exp(cum)).astype(BF16)
    kd = k * jnp.exp(tail)
    total = jnp.exp(cum)
    per = SUBLANES // steps
    grp = lax.broadcasted_iota(jnp.int32, (SUBLANES, 1), 0)
    tn = (((0,), (0,)), ((), ()))
    for b in range(rows // steps):
        r8 = (b // per) * SUBLANES
        lo = (b % per) * steps
        mine = (grp >= lo) & (grp < lo + steps)
        spare = (lo + steps) % SUBLANES
        for h in range(heads):
            hs = slice(h * HEAD_DIM, (h + 1) * HEAD_DIM)
            s0 = s0_ref[b, h]
            res = jnp.dot(qe[r8:r8 + SUBLANES, hs], s0.astype(BF16), preferred_element_type=F32)
            inter_ref[b * steps:(b + 1) * steps, hs] = res[lo:lo + steps]
            d1, d2, d3 = _split3(total[r8 + lo + steps - 1:r8 + lo + steps, hs])
            dec = jnp.where(grp == spare, d1.astype(F32),
                            jnp.where(grp == spare + 1, d2.astype(F32),
                                      jnp.where(grp == spare + 2, d3.astype(F32), 0.0)))
            lhs = jnp.where(mine, kd[r8:r8 + SUBLANES, hs], dec).astype(BF16)
            vb = jnp.where(mine, v[r8:r8 + SUBLANES, hs], 0.0)
            ones = jnp.where(mine, 0.0, 1.0) * jnp.ones((SUBLANES, HEAD_DIM), F32)
            rhs = jnp.concatenate([vb, ones], axis=1).astype(BF16)
            upd = lax.dot_general(lhs, rhs, tn, preferred_element_type=F32)
            s_ref[b, h] = upd[:, HEAD_DIM:] * s0 + upd[:, :HEAD_DIM]
    o = intra + inter_ref[...]
    gn = gn_ref[...]
    o = jnp.concatenate([_rms(o[:, h * HEAD_DIM:(h + 1) * HEAD_DIM], gn) for h in range(heads)], axis=1)
    o_ref[...] = (o * gs_ref[...].astype(F32)).astype(BF16)


def _hgrn_sample(q, k, lf, v, gs, g_norm, s0, row0, steps):
    n_batch, heads = s0.shape[:2]
    width = heads * HEAD_DIM
    assert steps & (steps - 1) == 0 and SUBLANES - steps >= 3
    bb = min(SAMPLE_BATCH_BLOCK, n_batch)
    rows = bb * steps
    assert n_batch % bb == 0 and rows % SUBLANES == 0 and row0 % rows == 0
    blk0 = row0 // rows
    tok = pl.BlockSpec((rows, width), lambda i: (blk0 + i, 0))
    st = pl.BlockSpec((bb, heads, HEAD_DIM, HEAD_DIM), lambda i: (i, 0, 0, 0))
    return pl.pallas_call(
        functools.partial(_hgrn_sample_kernel, steps=steps),
        grid=(n_batch // bb,),
        in_specs=[tok, tok, tok, tok, tok, pl.BlockSpec((1, HEAD_DIM), lambda i: (0, 0)), st],
        out_specs=[pl.BlockSpec((rows, width), lambda i: (i, 0)), st],
        out_shape=[jax.ShapeDtypeStruct((n_batch * steps, width), BF16),
                   jax.ShapeDtypeStruct(s0.shape, F32)],
        scratch_shapes=[pltpu.VMEM((rows, width), F32)],
        compiler_params=_params(("arbitrary",)),
        name="hgrn_sample",
    )(q, k, lf, v, gs, g_norm, s0)


def _outproj_kernel(cp_ref, cs_ref, op_ref, os_ref, xp_ref, xs_ref, wc_ref, wo_ref, g_ref, wr_ref, br_ref, tri_ref,
                    x1_ref, h2_ref, ri_ref, rw_ref, cnt_ref, run_ref, *, n_prompt_tiles, n_experts):
    i = pl.program_id(0)
    is_p = i < n_prompt_tiles

    @pl.when(i == 0)
    def _():
        run_ref[...] = jnp.zeros_like(run_ref)

    def mix(c_ref, o_ref, x_ref):
        y = jnp.dot(c_ref[...], wc_ref[...], preferred_element_type=F32)
        y = y + jnp.dot(o_ref[...], wo_ref[...], preferred_element_type=F32)
        x1_ref[...] = x_ref[...] + y

    pl.when(is_p)(lambda: mix(cp_ref, op_ref, xp_ref))
    pl.when(jnp.logical_not(is_p))(lambda: mix(cs_ref, os_ref, xs_ref))

    h2 = _rms(x1_ref[...], g_ref[...])
    h2_ref[...] = h2
    a1, a2, _ = _split3(h2)
    logits = (jnp.dot(a1, wr_ref[0], preferred_element_type=F32)
              + jnp.dot(a1, wr_ref[1], preferred_element_type=F32)
              + jnp.dot(a2, wr_ref[0], preferred_element_type=F32)) + br_ref[...]
    tm = logits.shape[0]
    lane = lax.broadcasted_iota(jnp.int32, (tm, LANES), 1)
    lane_f = lane.astype(F32)
    neg = jnp.float32(-jnp.inf)

    def top(x):
        m = jnp.max(x, axis=-1, keepdims=True)
        idx = jnp.min(jnp.where(x == m, lane_f, float(LANES)), axis=-1, keepdims=True)
        return m, idx.astype(jnp.int32)

    is_group = (lane >= n_experts) & (lane < n_experts + N_GROUPS)
    gl = jnp.where(is_group, logits, neg)
    gmax, gidx = top(gl)
    p_top = 1.0 / jnp.sum(jnp.exp(gl - gmax), axis=-1, keepdims=True)
    g_lo = (gidx - n_experts) * EXPERTS_PER_GROUP
    el = jnp.where((lane >= g_lo) & (lane < g_lo + EXPERTS_PER_GROUP), logits, neg)
    v1, e1 = top(el)
    v2, e2 = top(jnp.where(lane == e1, neg, el))
    t = jnp.exp(v2 - v1)
    w1 = p_top / (1.0 + t)
    w2 = p_top * t / (1.0 + t)
    hot = ((lane == e1) | (lane == e2)).astype(F32)
    before = run_ref[...] + jnp.dot(tri_ref[...], hot.astype(BF16), preferred_element_type=F32)
    r1 = jnp.sum(jnp.where(lane == e1, before, 0.0), axis=-1, keepdims=True).astype(jnp.int32)
    r2 = jnp.sum(jnp.where(lane == e2, before, 0.0), axis=-1, keepdims=True).astype(jnp.int32)
    run_ref[...] = run_ref[...] + jnp.sum(hot, axis=0, keepdims=True)
    cnt_ref[...] = run_ref[...]
    ri_ref[...] = jnp.where(lane == 0, e1, jnp.where(lane == 1, e2, jnp.where(lane == 2, r1, jnp.where(lane == 3, r2, 0))))
    rw_ref[...] = jnp.where(lane == 0, w1, jnp.where(lane == 1, w2, 0.0))


def _outproj(c_p, c_s, o_p, o_s, xp, xs, w_out_bf16, norm_g, w_router3, b_router, n_experts):
    n_p, d = xp.shape
    n_s = xs.shape[0]
    tm = min(TOKEN_TILE, n_s)
    npt, nst = n_p // tm, n_s // tm
    n = n_p + n_s
    ch = c_p.shape[1]
    hv = o_p.shape[1]
    wc, wo = w_out_bf16[:ch], w_out_bf16[ch:]
    tri = jnp.asarray(np.tril(np.ones((tm, tm), np.float32), -1), BF16)
    pidx = lambda i: (jnp.minimum(i, npt - 1), 0)
    sidx = lambda i: (jnp.maximum(i - npt, 0), 0)
    row = lambda i: (i, 0)
    c2 = lambda i: (0, 0)
    return pl.pallas_call(
        functools.partial(_outproj_kernel, n_prompt_tiles=npt, n_experts=n_experts),
        grid=(npt + nst,),
        in_specs=[
            pl.BlockSpec((tm, ch), pidx), pl.BlockSpec((tm, ch), sidx),
            pl.BlockSpec((tm, hv), pidx), pl.BlockSpec((tm, hv), sidx),
            pl.BlockSpec((tm, d), pidx), pl.BlockSpec((tm, d), sidx),
            pl.BlockSpec((ch, d), c2), pl.BlockSpec((hv, d), c2),
            pl.BlockSpec((1, d), c2),
            pl.BlockSpec((2, d, LANES), lambda i: (0, 0, 0)),
            pl.BlockSpec((1, LANES), c2),
            pl.BlockSpec((tm, tm), c2),
        ],
        out_specs=[pl.BlockSpec((tm, d), row), pl.BlockSpec((tm, d), row),
                   pl.BlockSpec((tm, LANES), row), pl.BlockSpec((tm, LANES), row),
                   pl.BlockSpec((1, LANES), c2)],
        out_shape=[jax.ShapeDtypeStruct((n, d), F32), jax.ShapeDtypeStruct((n, d), F32),
                   jax.ShapeDtypeStruct((n, LANES), jnp.int32), jax.ShapeDtypeStruct((n, LANES), F32),
                   jax.ShapeDtypeStruct((1, LANES), F32)],
        scratch_shapes=[pltpu.VMEM((1, LANES), F32)],
        compiler_params=_params(("arbitrary",)),
        name="outproj",
    )(c_p, c_s, o_p, o_s, xp, xs, wc, wo, norm_g, w_router3, b_router, tri)


def _dispatch_kernel(dest_ref, h2_ref, zero_ref, xs_ref, sem, *, n_pairs):
    del zero_ref

    def copy(p):
        return pltpu.make_async_copy(h2_ref.at[pl.ds(p // 2, 1)], xs_ref.at[pl.ds(dest_ref[p], 1)], sem)

    def start(p, carry):
        copy(p).start()
        return carry

    def wait(p, carry):
        copy(p).wait()
        return carry

    lax.fori_loop(0, n_pairs, start, 0)
    lax.fori_loop(0, n_pairs, wait, 0)


def _dispatch(dest_flat, h2, n_rows):
    n, d = h2.shape
    zeros = jnp.zeros((n_rows, d), h2.dtype)
    return pl.pallas_call(
        functools.partial(_dispatch_kernel, n_pairs=2 * n),
        grid_spec=pltpu.PrefetchScalarGridSpec(
            num_scalar_prefetch=1, grid=(1,),
            in_specs=[pl.BlockSpec(memory_space=pl.ANY), pl.BlockSpec(memory_space=pl.ANY)],
            out_specs=pl.BlockSpec(memory_space=pl.ANY),
            scratch_shapes=[pltpu.SemaphoreType.DMA(())],
        ),
        out_shape=jax.ShapeDtypeStruct((n_rows, d), h2.dtype),
        input_output_aliases={2: 0},
        compiler_params=_params(("arbitrary",)),
        name="dispatch",
    )(dest_flat, h2, zeros)


def _experts_kernel(te_ref, tf_ref, nu_ref, x_ref, wg_ref, wu_ref, wd_ref, y_ref, wgb_ref, wub_ref, wdb_ref):
    i = pl.program_id(0)

    @pl.when(tf_ref[i] == 1)
    def _():
        wgb_ref[...] = wg_ref[...].astype(BF16)
        wub_ref[...] = wu_ref[...].astype(BF16)
        wdb_ref[...] = wd_ref[...].astype(BF16)

    @pl.when(i < nu_ref[0])
    def _():
        x = x_ref[...].astype(BF16)
        hg = jnp.dot(x, wgb_ref[...], preferred_element_type=F32)
        hu = jnp.dot(x, wub_ref[...], preferred_element_type=F32)
        hid = (_silu(hg) * hu).astype(BF16)
        y_ref[...] = jnp.dot(hid, wdb_ref[...], preferred_element_type=F32)


def _experts(tile_expert, tile_first, n_used, xs, w_gate, w_up, w_down):
    n_rows, d = xs.shape
    n_exp, _, f = w_gate.shape
    n_tiles = n_rows // EXPERT_TILE
    tile = lambda i, te, tf, nu: (jnp.minimum(i, nu[0] - 1), 0)
    wsel = lambda i, te, tf, nu: (te[i], 0, 0)
    return pl.pallas_call(
        _experts_kernel,
        grid_spec=pltpu.PrefetchScalarGridSpec(
            num_scalar_prefetch=3, grid=(n_tiles,),
            in_specs=[pl.BlockSpec((EXPERT_TILE, d), tile),
                      pl.BlockSpec((None, d, f), wsel), pl.BlockSpec((None, d, f), wsel),
                      pl.BlockSpec((None, f, d), wsel)],
            out_specs=pl.BlockSpec((EXPERT_TILE, d), tile),
            scratch_shapes=[pltpu.VMEM((d, f), BF16), pltpu.VMEM((d, f), BF16), pltpu.VMEM((f, d), BF16)],
        ),
        out_shape=jax.ShapeDtypeStruct((n_rows, d), F32),
        compiler_params=_params(("arbitrary",)),
        name="experts",
    )(tile_expert, tile_first, n_used, xs, w_gate, w_up, w_down)


def _combine_kernel(dest_ref, ys_ref, x1_ref, rw_ref, g_ref, yp_ref, ysm_ref, buf_ref, sem, *, n_prompt_tiles):
    i = pl.program_id(0)
    tm = x1_ref.shape[0]

    def copy(p):
        r = p // 2
        slot = p % 2
        return pltpu.make_async_copy(ys_ref.at[pl.ds(dest_ref[i * (2 * tm) + p], 1)],
                                     buf_ref.at[slot, pl.ds(r, 1)], sem)

    def start(p, carry):
        copy(p).start()
        return carry

    def wait(p, carry):
        copy(p).wait()
        return carry

    lax.fori_loop(0, 2 * tm, start, 0)
    lax.fori_loop(0, 2 * tm, wait, 0)
    rw = rw_ref[...]
    x2 = x1_ref[...] + rw[:, 0:1] * buf_ref[0] + rw[:, 1:2] * buf_ref[1]
    y = _rms(x2, g_ref[...])

    @pl.when(i < n_prompt_tiles)
    def _():
        yp_ref[...] = y

    @pl.when(i >= n_prompt_tiles)
    def _():
        ysm_ref[...] = y


def _combine(dest_flat, ys, x1, rw, norm_g, n_p, n_s):
    n, d = x1.shape
    tm = min(CONV_TILE, n_s)
    npt, nst = n_p // tm, n_s // tm
    row = lambda i, dest: (i, 0)
    return pl.pallas_call(
        functools.partial(_combine_kernel, n_prompt_tiles=npt),
        grid_spec=pltpu.PrefetchScalarGridSpec(
            num_scalar_prefetch=1, grid=(npt + nst,),
            in_specs=[pl.BlockSpec(memory_space=pl.ANY),
                      pl.BlockSpec((tm, d), row), pl.BlockSpec((tm, LANES), row),
                      pl.BlockSpec((1, d), lambda i, dest: (0, 0))],
            out_specs=[pl.BlockSpec((tm, d), lambda i, dest: (jnp.minimum(i, npt - 1), 0)),
                       pl.BlockSpec((tm, d), lambda i, dest: (jnp.maximum(i - npt, 0), 0))],
            scratch_shapes=[pltpu.VMEM((2, tm, d), F32), pltpu.SemaphoreType.DMA(())],
        ),
        out_shape=[jax.ShapeDtypeStruct((n_p, d), F32), jax.ShapeDtypeStruct((n_s, d), F32)],
        compiler_params=_params(("arbitrary",)),
        name="combine",
    )(dest_flat, ys, x1, rw, norm_g)


def kernel(x_prompt, x_sample, state_conv, state_hgrn, norm_mix, w_in, w_dw, b_dw, ln_conv_g, ln_conv_b, lb_logits, hgrn_norm_g, w_out, norm_ffn, w_router_group, b_router_group, w_router_expert, b_router_expert, w_exp_gate, w_exp_up, w_exp_down, norm_final):
    assert w_in.shape[0] == 1, "single-layer trunk"
    n_batch, seq, d = x_prompt.shape
    s_batch, steps, _ = x_sample.shape
    ch = w_dw.shape[-1]
    hk = lb_logits.shape[-1]
    heads = hk // HEAD_DIM
    n_experts = w_exp_gate.shape[1]
    assert n_experts == N_GROUPS * EXPERTS_PER_GROUP and n_experts + N_GROUPS <= LANES
    n_p, n_s = n_batch * seq, s_batch * steps
    n = n_p + n_s
    xp = x_prompt.reshape(n_p, d)
    xs = x_sample.reshape(n_s, d)

    u, q, k, lf, v, gs = _inproj(xp, xs, norm_mix, w_in[0].astype(BF16), lb_logits, ch, hk)

    c_p = _conv_prompt(u, n_batch, seq, w_dw[0], b_dw, ln_conv_g, ln_conv_b)
    u_s = u[n_p:]
    c_s = _conv_sample(u_s, state_conv[0], w_dw[0], b_dw, ln_conv_g, ln_conv_b)
    hist = state_conv.shape[2]
    new_conv_prompt = u[:n_p].reshape(n_batch, seq, ch)[:, seq - hist:][None]
    new_conv_sample = jnp.concatenate([state_conv[0][:, steps:], u_s.reshape(s_batch, steps, ch)], axis=1)[None]

    o_p, hgrn_p = _hgrn_prompt(q, k, lf, v, gs, hgrn_norm_g, n_batch, seq, heads)
    o_s, hgrn_s = _hgrn_sample(q, k, lf, v, gs, hgrn_norm_g, state_hgrn[0], n_p, steps)

    w_r = jnp.concatenate([w_router_expert[0], w_router_group[0]], axis=1)
    w_r = jnp.pad(w_r, ((0, 0), (0, LANES - w_r.shape[1])))
    r1 = w_r.astype(BF16)
    r2 = (w_r - r1.astype(F32)).astype(BF16)
    b_r = jnp.pad(jnp.concatenate([b_router_expert[0], b_router_group[0]]), (0, LANES - n_experts - N_GROUPS))[None]
    x1, h2, ri, rw, counts = _outproj(c_p, c_s, o_p, o_s, xp, xs, w_out[0].astype(BF16), norm_ffn,
                                      jnp.stack([r1, r2]), b_r, n_experts)

    cnt = counts[0, :n_experts].astype(jnp.int32)
    tiles_per = (cnt + EXPERT_TILE - 1) // EXPERT_TILE
    tile_end = jnp.cumsum(tiles_per)
    row_start = (tile_end - tiles_per) * EXPERT_TILE
    n_tiles = (2 * n) // EXPERT_TILE + n_experts
    dest = row_start[ri[:, 0:2]] + ri[:, 2:4]
    dest_flat = dest.reshape(-1)
    n_used = tile_end[-1:]
    tid = jnp.minimum(jnp.arange(n_tiles, dtype=jnp.int32), n_used - 1)
    tile_expert = jnp.searchsorted(tile_end, tid, side="right").astype(jnp.int32)
    prev = jnp.concatenate([jnp.full((1,), -1, jnp.int32), tile_expert[:-1]])
    tile_first = ((tile_expert != prev) & (jnp.arange(n_tiles) < n_used)).astype(jnp.int32)

    xs_sorted = _dispatch(dest_flat, h2, n_tiles * EXPERT_TILE)
    ys_sorted = _experts(tile_expert, tile_first, n_used.astype(jnp.int32), xs_sorted,
                         w_exp_gate[0], w_exp_up[0], w_exp_down[0])
    y_p, y_s = _combine(dest_flat, ys_sorted, x1, rw, norm_final[None], n_p, n_s)

    return (y_p.reshape(n_batch, seq, d), y_s.reshape(s_batch, steps, d),
            new_conv_prompt, hgrn_p[None], new_conv_sample, hgrn_s[None])
```

```python
import functools

import numpy as np
import jax
import jax.numpy as jnp
from jax import lax
from jax.experimental import pallas as pl
from jax.experimental.pallas import tpu as pltpu

F32 = jnp.float32
BF16 = jnp.bfloat16
EPS = 1e-6
LANES = 128
SUBLANES = 8
HEAD_DIM = 128
HGRN_CHUNK = 64
HGRN_HEADS_PER_STEP = 4
FAST_DECAY_LIMIT = -60.0
N_GROUPS = 4
EXPERTS_PER_GROUP = 8
VMEM_LIMIT = 56 * 1024 * 1024
TOKEN_TILE = 256
INPROJ_TILE = 256
CONV_TILE = 256
CONV_HALO = 32
CONV_ROWS = 64
EXPERT_TILE = 256
SAMPLE_BATCH_BLOCK = 8
GATHER_UNROLL = 8


def _sigmoid(x):
    return 1.0 / (1.0 + jnp.exp(-x))


def _silu(x):
    return x * _sigmoid(x)


def _rms(x, g):
    return x * lax.rsqrt(jnp.mean(x * x, axis=-1, keepdims=True) + EPS) * g


def _split3(x):
    h1 = x.astype(BF16)
    r1 = x - h1.astype(F32)
    h2 = r1.astype(BF16)
    h3 = (r1 - h2.astype(F32)).astype(BF16)
    return h1, h2, h3


def _params(sem):
    return pltpu.CompilerParams(dimension_semantics=sem, vmem_limit_bytes=VMEM_LIMIT)


def _inproj_kernel(xp_ref, xs_ref, g_ref, w_ref, lbl_ref,
                   u_ref, q_ref, k_ref, lf_ref, v_ref, gs_ref, h_ref, *, n_prompt_tiles, ch, hk):
    i = pl.program_id(0)

    @pl.when(i < n_prompt_tiles)
    def _():
        h_ref[...] = _rms(xp_ref[...], g_ref[...]).astype(BF16)

    @pl.when(i >= n_prompt_tiles)
    def _():
        h_ref[...] = _rms(xs_ref[...], g_ref[...]).astype(BF16)

    h = h_ref[...]

    def proj(c0, width):
        return jnp.dot(h, w_ref[:, c0:c0 + width], preferred_element_type=F32)

    a = proj(0, ch)
    ga = proj(ch, ch)
    u_ref[...] = a * _sigmoid(ga)
    q = proj(2 * ch, hk)
    q_ref[...] = _silu(q).astype(BF16)
    f = proj(2 * ch + hk, hk)
    lbl = lbl_ref[...]
    e = jnp.exp(lbl - jnp.max(lbl, axis=0, keepdims=True))
    lb = e[0:1, :] / jnp.sum(e, axis=0, keepdims=True)
    fg = lb + (1.0 - lb) * _sigmoid(f)
    k_ref[...] = (1.0 - fg).astype(BF16)
    lf_ref[...] = jnp.log(fg)
    v_ref[...] = proj(2 * ch + 2 * hk, hk).astype(BF16)
    g = proj(2 * ch + 3 * hk, hk)
    gs_ref[...] = _silu(g).astype(BF16)


def _inproj(xp, xs, norm_g, w_in_bf16, lb_logits, ch, hk):
    n_p, d = xp.shape
    n_s = xs.shape[0]
    tm = min(INPROJ_TILE, n_s)
    assert n_p % tm == 0 and n_s % tm == 0
    npt, nst = n_p // tm, n_s // tm
    n = n_p + n_s
    cols = w_in_bf16.shape[1]
    row = lambda i: (i, 0)
    const = lambda i: (0, 0)
    outs = [jax.ShapeDtypeStruct((n, ch), F32)] + [
        jax.ShapeDtypeStruct((n, hk), dt) for dt in (BF16, BF16, F32, BF16, BF16)]
    return pl.pallas_call(
        functools.partial(_inproj_kernel, n_prompt_tiles=npt, ch=ch, hk=hk),
        grid=(npt + nst,),
        in_specs=[
            pl.BlockSpec((tm, d), lambda i: (jnp.minimum(i, npt - 1), 0)),
            pl.BlockSpec((tm, d), lambda i: (jnp.maximum(i - npt, 0), 0)),
            pl.BlockSpec((1, d), const),
            pl.BlockSpec((d, cols), const, pipeline_mode=pl.Buffered(1)),
            pl.BlockSpec(lb_logits.shape, const),
        ],
        out_specs=[pl.BlockSpec((tm, ch), row)] + [pl.BlockSpec((tm, hk), row)] * 5,
        out_shape=outs,
        scratch_shapes=[pltpu.VMEM((tm, d), BF16)],
        compiler_params=_params(("arbitrary",)),
        name="inproj",
    )(xp, xs, norm_g, w_in_bf16, lb_logits)


def _ln_silu(c, g, b):
    mu = jnp.mean(c, axis=-1, keepdims=True)
    d = c - mu
    var = jnp.mean(d * d, axis=-1, keepdims=True)
    return _silu(d * lax.rsqrt(var + EPS) * g + b)


def _conv_prompt_kernel(halo_ref, cur_ref, w_ref, b_ref, lg_ref, lb_ref, c_ref, ext_ref, acc_ref, *, width):
    t = pl.program_id(1)
    tt, ch = cur_ref.shape
    halo = halo_ref[...]
    ext_ref[0:CONV_HALO, :] = jnp.where(t == 0, jnp.zeros_like(halo), halo)
    ext_ref[CONV_HALO:, :] = cur_ref[...]
    off = CONV_HALO - (width - 1)
    rows = min(CONV_ROWS, tt)
    for l in range(ch // LANES):
        ls = slice(l * LANES, (l + 1) * LANES)
        wl = w_ref[:, ls]
        bl = b_ref[:, ls]
        for r0 in range(0, tt, rows):
            acc = jnp.broadcast_to(bl, (rows, LANES))
            for j in range(width):
                acc = acc + ext_ref[r0 + off + j:r0 + off + j + rows, ls] * wl[j:j + 1, :]
            acc_ref[r0:r0 + rows, ls] = acc
    c_ref[...] = _ln_silu(acc_ref[...], lg_ref[...], lb_ref[...]).astype(BF16)


def _conv_prompt(u, n_batch, seq, w_dw, b_dw, ln_g, ln_b):
    ch = u.shape[1]
    width = w_dw.shape[0]
    tt = min(CONV_TILE, seq)
    assert seq % tt == 0 and tt % CONV_HALO == 0 and width - 1 <= CONV_HALO
    nt = seq // tt
    hb = tt // CONV_HALO
    const = lambda b, t: (0, 0)
    return pl.pallas_call(
        functools.partial(_conv_prompt_kernel, width=width),
        grid=(n_batch, nt),
        in_specs=[
            pl.BlockSpec((CONV_HALO, ch), lambda b, t: (jnp.maximum((b * nt + t) * hb - 1, 0), 0)),
            pl.BlockSpec((tt, ch), lambda b, t: (b * nt + t, 0)),
            pl.BlockSpec((width, ch), const),
            pl.BlockSpec((1, ch), const),
            pl.BlockSpec((1, ch), const),
            pl.BlockSpec((1, ch), const),
        ],
        out_specs=pl.BlockSpec((tt, ch), lambda b, t: (b * nt + t, 0)),
        out_shape=jax.ShapeDtypeStruct((n_batch * seq, ch), BF16),
        scratch_shapes=[pltpu.VMEM((CONV_HALO + tt, ch), F32), pltpu.VMEM((tt, ch), F32)],
        compiler_params=_params(("arbitrary", "arbitrary")),
        name="conv_prompt",
    )(u, u, w_dw, b_dw, ln_g, ln_b)


def _conv_sample_kernel(state_ref, u_ref, ws_ref, wu_ref, b_ref, lg_ref, lb_ref, c_ref, acc_ref, *, steps):
    bb = state_ref.shape[0]
    for b in range(bb):
        st = state_ref[b]
        ub = u_ref[b * steps:(b + 1) * steps, :]
        for t in range(steps):
            row = (jnp.sum(st * ws_ref[t], axis=0, keepdims=True)
                   + jnp.sum(ub * wu_ref[t], axis=0, keepdims=True))
            acc_ref[b * steps + t:b * steps + t + 1, :] = row + b_ref[...]
    c_ref[...] = _ln_silu(acc_ref[...], lg_ref[...], lb_ref[...]).astype(BF16)


def _conv_sample(u_s, state, w_dw, b_dw, ln_g, ln_b):
    n_batch, hist, ch = state.shape
    width = w_dw.shape[0]
    steps = u_s.shape[0] // n_batch
    assert hist == width - 1 and steps <= hist
    ws = jnp.stack([jnp.concatenate([jnp.zeros((t, ch), F32), w_dw[:hist - t]], axis=0) for t in range(steps)])
    wu = jnp.stack([jnp.concatenate([w_dw[hist - t:], jnp.zeros((steps - 1 - t, ch), F32)], axis=0)
                    for t in range(steps)])
    bb = min(SAMPLE_BATCH_BLOCK, n_batch)
    assert n_batch % bb == 0
    c3 = lambda i: (0, 0, 0)
    c2 = lambda i: (0, 0)
    return pl.pallas_call(
        functools.partial(_conv_sample_kernel, steps=steps),
        grid=(n_batch // bb,),
        in_specs=[
            pl.BlockSpec((bb, hist, ch), lambda i: (i, 0, 0)),
            pl.BlockSpec((bb * steps, ch), lambda i: (i, 0)),
            pl.BlockSpec((steps, hist, ch), c3),
            pl.BlockSpec((steps, steps, ch), c3),
            pl.BlockSpec((1, ch), c2),
            pl.BlockSpec((1, ch), c2),
            pl.BlockSpec((1, ch), c2),
        ],
        out_specs=pl.BlockSpec((bb * steps, ch), lambda i: (i, 0)),
        out_shape=jax.ShapeDtypeStruct((n_batch * steps, ch), BF16),
        scratch_shapes=[pltpu.VMEM((bb * steps, ch), F32)],
        compiler_params=_params(("arbitrary",)),
        name="conv_sample",
    )(state, u_s, ws, wu, b_dw, ln_g, ln_b)


def _chunk_constants(c):
    levels = int(np.log2(c))
    assert 2 ** levels == c
    t = np.arange(c)[:, None]
    j = np.arange(c)[None, :]
    mats = [(j <= t)]
    right = np.zeros((levels, c, LANES), np.float32)
    left = np.zeros((levels, c, LANES), np.float32)
    same = np.zeros((levels, c, c), np.float32)
    for lv in range(levels):
        half = c >> (lv + 1)
        split = (t // (2 * half)) * (2 * half) + half
        is_right = t >= split
        mats.append(np.where(is_right, (j >= split) & (j <= t), (j > t) & (j < split)))
        right[lv] = is_right
        left[lv] = ~is_right
        same[lv] = (t // (2 * half)) == (j // (2 * half))
    mats.append(j > t)
    dm = np.concatenate(mats, axis=0).astype(np.float32)
    return (jnp.asarray(dm, BF16), jnp.asarray(right), jnp.asarray(left), jnp.asarray(same))


def _hgrn_prompt_kernel(q_ref, k_ref, lf_ref, v_ref, gs_ref, gn_ref, dm_ref, rm_ref, lm_ref, bm_ref,
                        o_ref, s_ref, *, chunk):
    seq, width = q_ref.shape
    hp = width // HEAD_DIM
    levels = rm_ref.shape[0]
    n_chunks = seq // chunk
    nt = (((1,), (1,)), ((), ()))
    tn = (((0,), (0,)), ((), ()))
    row_i = lax.broadcasted_iota(jnp.int32, (chunk, chunk), 0)
    col_i = lax.broadcasted_iota(jnp.int32, (chunk, chunk), 1)

    def chunk_step(c, st, hs, robust):
        rows = pl.ds(pl.multiple_of(c * chunk, chunk), chunk)
        q = q_ref[rows, hs].astype(F32)
        k = k_ref[rows, hs].astype(F32)
        v = v_ref[rows, hs]
        l1, l2, l3 = _split3(lf_ref[rows, hs])

        def decay_sums(dm):
            return (jnp.dot(dm, l1, preferred_element_type=F32) + jnp.dot(dm, l2, preferred_element_type=F32)
                    + jnp.dot(dm, l3, preferred_element_type=F32))

        if robust:
            ex = jnp.exp(decay_sums(dm_ref[...]))
            e_cum = ex[0:chunk]
            e_tail = ex[(levels + 1) * chunk:(levels + 2) * chunk]
            att = jnp.where(row_i == col_i, jnp.sum(q * k, axis=-1, keepdims=True), 0.0)
            for lv in range(levels):
                e = ex[(lv + 1) * chunk:(lv + 2) * chunk]
                ql = (q * e * rm_ref[lv]).astype(BF16)
                kl = (k * e * lm_ref[lv]).astype(BF16)
                att = att + bm_ref[lv] * lax.dot_general(ql, kl, nt, preferred_element_type=F32)
            qe = (q * e_cum).astype(BF16)
        else:
            cum = decay_sums(dm_ref[0:chunk, :])
            e_cum = jnp.exp(cum)
            e_tail = jnp.exp(decay_sums(dm_ref[(levels + 1) * chunk:(levels + 2) * chunk, :]))
            qe = (q * e_cum).astype(BF16)
            kn = (k * jnp.exp(-cum)).astype(BF16)
            att = jnp.where(row_i >= col_i, lax.dot_general(qe, kn, nt, preferred_element_type=F32), 0.0)
        o = jnp.dot(att.astype(BF16), v, preferred_element_type=F32)
        o = o + lax.dot_general(qe, st.astype(BF16), nt, preferred_element_type=F32)
        st = st * e_cum[chunk - 1:chunk, :] + lax.dot_general(v, (k * e_tail).astype(BF16), tn,
                                                               preferred_element_type=F32)
        o = _rms(o, gn_ref[...]) * gs_ref[rows, hs].astype(F32)
        o_ref[rows, hs] = o.astype(BF16)
        return st

    def run(robust):
        def body(c, sts):
            return tuple(chunk_step(c, sts[h], slice(h * HEAD_DIM, (h + 1) * HEAD_DIM), robust) for h in range(hp))
        sts = lax.fori_loop(0, n_chunks, body, tuple(jnp.zeros((HEAD_DIM, HEAD_DIM), F32) for _ in range(hp)))
        for h in range(hp):
            s_ref[0, h] = sts[h].T

    lf_all = lf_ref[...].reshape(n_chunks, chunk, width)
    slowest = jnp.min(jnp.sum(lf_all, axis=1))
    fast = slowest >= FAST_DECAY_LIMIT
    pl.when(fast)(lambda: run(False))
    pl.when(jnp.logical_not(fast))(lambda: run(True))


def _hgrn_prompt(q, k, lf, v, gs, g_norm, n_batch, seq, heads):
    chunk = HGRN_CHUNK if seq % HGRN_CHUNK == 0 else seq
    dm, rm, lm, bm = _chunk_constants(chunk)
    hp = min(HGRN_HEADS_PER_STEP, heads)
    assert heads % hp == 0
    tok = pl.BlockSpec((seq, hp * HEAD_DIM), lambda b, h: (b, h))
    c2 = lambda b, h: (0, 0)
    c3 = lambda b, h: (0, 0, 0)
    return pl.pallas_call(
        functools.partial(_hgrn_prompt_kernel, chunk=chunk),
        grid=(n_batch, heads // hp),
        in_specs=[tok, tok, tok, tok, tok,
                  pl.BlockSpec((1, HEAD_DIM), c2),
                  pl.BlockSpec(dm.shape, c2),
                  pl.BlockSpec(rm.shape, c3), pl.BlockSpec(lm.shape, c3), pl.BlockSpec(bm.shape, c3)],
        out_specs=[tok, pl.BlockSpec((1, hp, HEAD_DIM, HEAD_DIM), lambda b, h: (b, h, 0, 0))],
        out_shape=[jax.ShapeDtypeStruct((n_batch * seq, heads * HEAD_DIM), BF16),
                   jax.ShapeDtypeStruct((n_batch, heads, HEAD_DIM, HEAD_DIM), F32)],
        compiler_params=_params(("arbitrary", "arbitrary")),
        name="hgrn_prompt",
    )(q, k, lf, v, gs, g_norm, dm, rm, lm, bm)


def _hgrn_sample_kernel(q_ref, k_ref, lf_ref, v_ref, gs_ref, gn_ref, s0_ref, o_ref, s_ref, inter_ref, *, steps):
    rows, width = q_ref.shape
    heads = width // HEAD_DIM
    q = q_ref[...].astype(F32)
    k = k_ref[...].astype(F32)
    v = v_ref[...].astype(F32)
    lf = lf_ref[...]
    step = lax.broadcasted_iota(jnp.int32, (rows, 1), 0) & (steps - 1)

    def back(x, d):
        return pltpu.roll(x, d, 0)

    cum = lf
    for d in range(1, steps):
        cum = cum + jnp.where(step >= d, back(lf, d), 0.0)
    tail = jnp.zeros_like(lf)
    for d in range(1, steps):
        tail = tail + jnp.where(step + d < steps, pltpu.roll(lf, rows - d, 0), 0.0)

    def head_sum(x):
        return [jnp.sum(x[:, h * HEAD_DIM:(h + 1) * HEAD_DIM], axis=-1, keepdims=True) for h in range(heads)]

    def head_scale(cols, x):
        return jnp.concatenate([cols[h] * x[:, h * HEAD_DIM:(h + 1) * HEAD_DIM] for h in range(heads)], axis=1)

    intra = head_scale(head_sum(q * k), v)
    for d in range(1, steps):
        ok = step >= d
        rel = jnp.where(ok, cum - back(cum, d), 0.0)
        w = jnp.where(ok, q * back(k, d) * jnp.exp(rel), 0.0)
        intra = intra + head_scale(head_sum(w), back(v, d))

    qe = (q * jnp.exp(cum)).astype(BF16)
    kd = k * jnp.exp(tail)
    total = jnp.exp(cum)
    per = SUBLANES // steps
    grp = lax.broadcasted_iota(jnp.int32, (SUBLANES, 1), 0)
    tn = (((0,), (0,)), ((), ()))
    for b in range(rows // steps):
        r8 = (b // per) * SUBLANES
        lo = (b % per) * steps
        mine = (grp >= lo) & (grp < lo + steps)
        spare = (lo + steps) % SUBLANES
        for h in range(heads):
            hs = slice(h * HEAD_DIM, (h + 1) * HEAD_DIM)
            s0 = s0_ref[b, h]
            res = jnp.dot(qe[r8:r8 + SUBLANES, hs], s0.astype(BF16), preferred_element_type=F32)
            inter_ref[b * steps:(b + 1) * steps, hs] = res[lo:lo + steps]
            d1, d2, d3 = _split3(total[r8 + lo + steps - 1:r8 + lo + steps, hs])
            dec = jnp.where(grp == spare, d1.astype(F32),
                            jnp.where(grp == spare + 1, d2.astype(F32),
                                      jnp.where(grp == spare + 2, d3.astype(F32), 0.0)))
            lhs = jnp.where(mine, kd[r8:r8 + SUBLANES, hs], dec).astype(BF16)
            vb = jnp.where(mine, v[r8:r8 + SUBLANES, hs], 0.0)
            ones = jnp.where(mine, 0.0, 1.0) * jnp.ones((SUBLANES, HEAD_DIM), F32)
            rhs = jnp.concatenate([vb, ones], axis=1).astype(BF16)
            upd = lax.dot_general(lhs, rhs, tn, preferred_element_type=F32)
            s_ref[b, h] = upd[:, HEAD_DIM:] * s0 + upd[:, :HEAD_DIM]
    o = intra + inter_ref[...]
    gn = gn_ref[...]
    o = jnp.concatenate([_rms(o[:, h * HEAD_DIM:(h + 1) * HEAD_DIM], gn) for h in range(heads)], axis=1)
    o_ref[...] = (o * gs_ref[...].astype(F32)).astype(BF16)


def _hgrn_sample(q, k, lf, v, gs, g_norm, s0, row0, steps):
    n_batch, heads = s0.shape[:2]
    width = heads * HEAD_DIM
    assert steps & (steps - 1) == 0 and SUBLANES - steps >= 3
    bb = min(SAMPLE_BATCH_BLOCK, n_batch)
    rows = bb * steps
    assert n_batch % bb == 0 and rows % SUBLANES == 0 and row0 % rows == 0
    blk0 = row0 // rows
    tok = pl.BlockSpec((rows, width), lambda i: (blk0 + i, 0))
    st = pl.BlockSpec((bb, heads, HEAD_DIM, HEAD_DIM), lambda i: (i, 0, 0, 0))
    return pl.pallas_call(
        functools.partial(_hgrn_sample_kernel, steps=steps),
        grid=(n_batch // bb,),
        in_specs=[tok, tok, tok, tok, tok, pl.BlockSpec((1, HEAD_DIM), lambda i: (0, 0)), st],
        out_specs=[pl.BlockSpec((rows, width), lambda i: (i, 0)), st],
        out_shape=[jax.ShapeDtypeStruct((n_batch * steps, width), BF16),
                   jax.ShapeDtypeStruct(s0.shape, F32)],
        scratch_shapes=[pltpu.VMEM((rows, width), F32)],
        compiler_params=_params(("arbitrary",)),
        name="hgrn_sample",
    )(q, k, lf, v, gs, g_norm, s0)


def _outproj_kernel(cp_ref, cs_ref, op_ref, os_ref, xp_ref, xs_ref, wc_ref, wo_ref, g_ref, wr_ref, br_ref, tri_ref,
                    x1_ref, h2_ref, ri_ref, rw_ref, cnt_ref, run_ref, *, n_prompt_tiles, n_experts):
    i = pl.program_id(0)
    is_p = i < n_prompt_tiles

    @pl.when(i == 0)
    def _():
        run_ref[...] = jnp.zeros_like(run_ref)

    def mix(c_ref, o_ref, x_ref):
        y = jnp.dot(c_ref[...], wc_ref[...], preferred_element_type=F32)
        y = y + jnp.dot(o_ref[...], wo_ref[...], preferred_element_type=F32)
        x1_ref[...] = x_ref[...] + y

    pl.when(is_p)(lambda: mix(cp_ref, op_ref, xp_ref))
    pl.when(jnp.logical_not(is_p))(lambda: mix(cs_ref, os_ref, xs_ref))

    h2 = _rms(x1_ref[...], g_ref[...])
    h2_ref[...] = h2
    a1, a2, _ = _split3(h2)
    logits = (jnp.dot(a1, wr_ref[0], preferred_element_type=F32)
              + jnp.dot(a1, wr_ref[1], preferred_element_type=F32)
              + jnp.dot(a2, wr_ref[0], preferred_element_type=F32)) + br_ref[...]
    tm = logits.shape[0]
    lane = lax.broadcasted_iota(jnp.int32, (tm, LANES), 1)
    lane_f = lane.astype(F32)
    neg = jnp.float32(-jnp.inf)

    def top(x):
        m = jnp.max(x, axis=-1, keepdims=True)
        idx = jnp.min(jnp.where(x == m, lane_f, float(LANES)), axis=-1, keepdims=True)
        return m, idx.astype(jnp.int32)

    is_group = (lane >= n_experts) & (lane < n_experts + N_GROUPS)
    gl = jnp.where(is_group, logits, neg)
    gmax, gidx = top(gl)
    p_top = 1.0 / jnp.sum(jnp.exp(gl - gmax), axis=-1, keepdims=True)
    g_lo = (gidx - n_experts) * EXPERTS_PER_GROUP
    el = jnp.where((lane >= g_lo) & (lane < g_lo + EXPERTS_PER_GROUP), logits, neg)
    v1, e1 = top(el)
    v2, e2 = top(jnp.where(lane == e1, neg, el))
    t = jnp.exp(v2 - v1)
    w1 = p_top / (1.0 + t)
    w2 = p_top * t / (1.0 + t)
    hot = ((lane == e1) | (lane == e2)).astype(F32)
    before = run_ref[...] + jnp.dot(tri_ref[...], hot.astype(BF16), preferred_element_type=F32)
    r1 = jnp.sum(jnp.where(lane == e1, before, 0.0), axis=-1, keepdims=True).astype(jnp.int32)
    r2 = jnp.sum(jnp.where(lane == e2, before, 0.0), axis=-1, keepdims=True).astype(jnp.int32)
    run_ref[...] = run_ref[...] + jnp.sum(hot, axis=0, keepdims=True)
    cnt_ref[...] = run_ref[...]
    ri_ref[...] = jnp.where(lane == 0, e1, jnp.where(lane == 1, e2, jnp.where(lane == 2, r1, jnp.where(lane == 3, r2, 0))))
    rw_ref[...] = jnp.where(lane == 0, w1, jnp.where(lane == 1, w2, 0.0))


def _outproj(c_p, c_s, o_p, o_s, xp, xs, w_out_bf16, norm_g, w_router3, b_router, n_experts):
    n_p, d = xp.shape
    n_s = xs.shape[0]
    tm = min(TOKEN_TILE, n_s)
    npt, nst = n_p // tm, n_s // tm
    n = n_p + n_s
    ch = c_p.shape[1]
    hv = o_p.shape[1]
    wc, wo = w_out_bf16[:ch], w_out_bf16[ch:]
    tri = jnp.asarray(np.tril(np.ones((tm, tm), np.float32), -1), BF16)
    pidx = lambda i: (jnp.minimum(i, npt - 1), 0)
    sidx = lambda i: (jnp.maximum(i - npt, 0), 0)
    row = lambda i: (i, 0)
    c2 = lambda i: (0, 0)
    return pl.pallas_call(
        functools.partial(_outproj_kernel, n_prompt_tiles=npt, n_experts=n_experts),
        grid=(npt + nst,),
        in_specs=[
            pl.BlockSpec((tm, ch), pidx), pl.BlockSpec((tm, ch), sidx),
            pl.BlockSpec((tm, hv), pidx), pl.BlockSpec((tm, hv), sidx),
            pl.BlockSpec((tm, d), pidx), pl.BlockSpec((tm, d), sidx),
            pl.BlockSpec((ch, d), c2), pl.BlockSpec((hv, d), c2),
            pl.BlockSpec((1, d), c2),
            pl.BlockSpec((2, d, LANES), lambda i: (0, 0, 0)),
            pl.BlockSpec((1, LANES), c2),
            pl.BlockSpec((tm, tm), c2),
        ],
        out_specs=[pl.BlockSpec((tm, d), row), pl.BlockSpec((tm, d), row),
                   pl.BlockSpec((tm, LANES), row), pl.BlockSpec((tm, LANES), row),
                   pl.BlockSpec((1, LANES), c2)],
        out_shape=[jax.ShapeDtypeStruct((n, d), F32), jax.ShapeDtypeStruct((n, d), F32),
                   jax.ShapeDtypeStruct((n, LANES), jnp.int32), jax.ShapeDtypeStruct((n, LANES), F32),
                   jax.ShapeDtypeStruct((1, LANES), F32)],
        scratch_shapes=[pltpu.VMEM((1, LANES), F32)],
        compiler_params=_params(("arbitrary",)),
        name="outproj",
    )(c_p, c_s, o_p, o_s, xp, xs, wc, wo, norm_g, w_router3, b_router, tri)


def _experts_kernel(src_ref, te_ref, tf_ref, nu_ref, h2_ref, wg_ref, wu_ref, wd_ref, y_ref,
                    x_ref, wgb_ref, wub_ref, wdb_ref, sem):
    i = pl.program_id(0)
    n_used = nu_ref[0]
    rows = x_ref.shape[1]

    def row_copy(tile, slot, r):
        return pltpu.make_async_copy(h2_ref.at[pl.ds(src_ref[tile * rows + r], 1)],
                                     x_ref.at[slot, pl.ds(r, 1)], sem.at[slot])

    def gather(tile, slot):
        def start(r, carry):
            row_copy(tile, slot, r).start()
            return carry
        lax.fori_loop(0, rows, start, 0, unroll=GATHER_UNROLL)

    @pl.when(i == 0)
    def _():
        gather(0, 0)

    @pl.when(i + 1 < n_used)
    def _():
        gather(i + 1, (i + 1) % 2)

    @pl.when(tf_ref[i] == 1)
    def _():
        wgb_ref[...] = wg_ref[...].astype(BF16)
        wub_ref[...] = wu_ref[...].astype(BF16)
        wdb_ref[...] = wd_ref[...].astype(BF16)

    @pl.when(i < n_used)
    def _():
        slot = i % 2

        def wait(r, carry):
            row_copy(i, slot, r).wait()
            return carry
        lax.fori_loop(0, rows, wait, 0, unroll=GATHER_UNROLL)
        x = x_ref[slot].astype(BF16)
        hg = jnp.dot(x, wgb_ref[...], preferred_element_type=F32)
        hu = jnp.dot(x, wub_ref[...], preferred_element_type=F32)
        hid = (_silu(hg) * hu).astype(BF16)
        y_ref[...] = jnp.dot(hid, wdb_ref[...], preferred_element_type=F32)

    @pl.when(i >= n_used)
    def _():
        y_ref[...] = jnp.zeros_like(y_ref)


def _experts(src_token, tile_expert, tile_first, n_used, h2, w_gate, w_up, w_down):
    n_rows = src_token.shape[0]
    d = h2.shape[1]
    f = w_gate.shape[2]
    n_tiles = n_rows // EXPERT_TILE
    tile = lambda i, src, te, tf, nu: (i, 0)
    wsel = lambda i, src, te, tf, nu: (te[i], 0, 0)
    return pl.pallas_call(
        _experts_kernel,
        grid_spec=pltpu.PrefetchScalarGridSpec(
            num_scalar_prefetch=4, grid=(n_tiles,),
            in_specs=[pl.BlockSpec(memory_space=pl.ANY),
                      pl.BlockSpec((None, d, f), wsel), pl.BlockSpec((None, d, f), wsel),
                      pl.BlockSpec((None, f, d), wsel)],
            out_specs=pl.BlockSpec((EXPERT_TILE, d), tile),
            scratch_shapes=[pltpu.VMEM((2, EXPERT_TILE, d), F32),
                            pltpu.VMEM((d, f), BF16), pltpu.VMEM((d, f), BF16), pltpu.VMEM((f, d), BF16),
                            pltpu.SemaphoreType.DMA((2,))],
        ),
        out_shape=jax.ShapeDtypeStruct((n_rows, d), F32),
        compiler_params=_params(("arbitrary",)),
        name="experts",
    )(src_token, tile_expert, tile_first, n_used, h2, w_gate, w_up, w_down)


def _combine_kernel(dest_ref, ys_ref, x1_ref, rw_ref, g_ref, yp_ref, ysm_ref, buf_ref, sem, *, n_prompt_tiles):
    i = pl.program_id(0)
    tm = x1_ref.shape[0]

    base = i * (2 * tm)

    def copy(r, slot):
        return pltpu.make_async_copy(ys_ref.at[pl.ds(dest_ref[base + 2 * r + slot], 1)],
                                     buf_ref.at[slot, pl.ds(r, 1)], sem)

    def start(r, carry):
        copy(r, 0).start()
        copy(r, 1).start()
        return carry

    def wait(r, carry):
        copy(r, 0).wait()
        copy(r, 1).wait()
        return carry

    lax.fori_loop(0, tm, start, 0, unroll=GATHER_UNROLL)
    lax.fori_loop(0, tm, wait, 0, unroll=GATHER_UNROLL)
    rw = rw_ref[...]
    x2 = x1_ref[...] + rw[:, 0:1] * buf_ref[0] + rw[:, 1:2] * buf_ref[1]
    y = _rms(x2, g_ref[...])

    @pl.when(i < n_prompt_tiles)
    def _():
        yp_ref[...] = y

    @pl.when(i >= n_prompt_tiles)
    def _():
        ysm_ref[...] = y


def _combine(dest_flat, ys, x1, rw, norm_g, n_p, n_s):
    n, d = x1.shape
    tm = min(CONV_TILE, n_s)
    npt, nst = n_p // tm, n_s // tm
    row = lambda i, dest: (i, 0)
    return pl.pallas_call(
        functools.partial(_combine_kernel, n_prompt_tiles=npt),
        grid_spec=pltpu.PrefetchScalarGridSpec(
            num_scalar_prefetch=1, grid=(npt + nst,),
            in_specs=[pl.BlockSpec(memory_space=pl.ANY),
                      pl.BlockSpec((tm, d), row), pl.BlockSpec((tm, LANES), row),
                      pl.BlockSpec((1, d), lambda i, dest: (0, 0))],
            out_specs=[pl.BlockSpec((tm, d), lambda i, dest: (jnp.minimum(i, npt - 1), 0)),
                       pl.BlockSpec((tm, d), lambda i, dest: (jnp.maximum(i - npt, 0), 0))],
            scratch_shapes=[pltpu.VMEM((2, tm, d), F32), pltpu.SemaphoreType.DMA(())],
        ),
        out_shape=[jax.ShapeDtypeStruct((n_p, d), F32), jax.ShapeDtypeStruct((n_s, d), F32)],
        compiler_params=_params(("arbitrary",)),
        name="combine",
    )(dest_flat, ys, x1, rw, norm_g)


def kernel(x_prompt, x_sample, state_conv, state_hgrn, norm_mix, w_in, w_dw, b_dw, ln_conv_g, ln_conv_b, lb_logits, hgrn_norm_g, w_out, norm_ffn, w_router_group, b_router_group, w_router_expert, b_router_expert, w_exp_gate, w_exp_up, w_exp_down, norm_final):
    assert w_in.shape[0] == 1, "single-layer trunk"
    n_batch, seq, d = x_prompt.shape
    s_batch, steps, _ = x_sample.shape
    ch = w_dw.shape[-1]
    hk = lb_logits.shape[-1]
    heads = hk // HEAD_DIM
    n_experts = w_exp_gate.shape[1]
    assert n_experts == N_GROUPS * EXPERTS_PER_GROUP and n_experts + N_GROUPS <= LANES
    n_p, n_s = n_batch * seq, s_batch * steps
    n = n_p + n_s
    xp = x_prompt.reshape(n_p, d)
    xs = x_sample.reshape(n_s, d)

    u, q, k, lf, v, gs = _inproj(xp, xs, norm_mix, w_in[0].astype(BF16), lb_logits, ch, hk)

    c_p = _conv_prompt(u, n_batch, seq, w_dw[0], b_dw, ln_conv_g, ln_conv_b)
    u_s = u[n_p:]
    c_s = _conv_sample(u_s, state_conv[0], w_dw[0], b_dw, ln_conv_g, ln_conv_b)
    hist = state_conv.shape[2]
    new_conv_prompt = u[:n_p].reshape(n_batch, seq, ch)[:, seq - hist:][None]
    new_conv_sample = jnp.concatenate([state_conv[0][:, steps:], u_s.reshape(s_batch, steps, ch)], axis=1)[None]

    o_p, hgrn_p = _hgrn_prompt(q, k, lf, v, gs, hgrn_norm_g, n_batch, seq, heads)
    o_s, hgrn_s = _hgrn_sample(q, k, lf, v, gs, hgrn_norm_g, state_hgrn[0], n_p, steps)

    w_r = jnp.concatenate([w_router_expert[0], w_router_group[0]], axis=1)
    w_r = jnp.pad(w_r, ((0, 0), (0, LANES - w_r.shape[1])))
    r1 = w_r.astype(BF16)
    r2 = (w_r - r1.astype(F32)).astype(BF16)
    b_r = jnp.pad(jnp.concatenate([b_router_expert[0], b_router_group[0]]), (0, LANES - n_experts - N_GROUPS))[None]
    x1, h2, ri, rw, counts = _outproj(c_p, c_s, o_p, o_s, xp, xs, w_out[0].astype(BF16), norm_ffn,
                                      jnp.stack([r1, r2]), b_r, n_experts)

    cnt = counts[0, :n_experts].astype(jnp.int32)
    tiles_per = (cnt + EXPERT_TILE - 1) // EXPERT_TILE
    tile_end = jnp.cumsum(tiles_per)
    row_start = (tile_end - tiles_per) * EXPERT_TILE
    n_tiles = (2 * n) // EXPERT_TILE + n_experts
    dest = row_start[ri[:, 0:2]] + ri[:, 2:4]
    dest_flat = dest.reshape(-1)
    n_used = tile_end[-1:]
    tid = jnp.minimum(jnp.arange(n_tiles, dtype=jnp.int32), n_used - 1)
    tile_expert = jnp.sum((tile_end[None, :] <= tid[:, None]).astype(jnp.int32), axis=1)
    prev = jnp.concatenate([jnp.full((1,), -1, jnp.int32), tile_expert[:-1]])
    tile_first = ((tile_expert != prev) & (jnp.arange(n_tiles) < n_used)).astype(jnp.int32)
    token = jnp.repeat(jnp.arange(n, dtype=jnp.int32), 2)
    src_token = jnp.zeros((n_tiles * EXPERT_TILE,), jnp.int32).at[dest_flat].set(token, unique_indices=True)

    ys_sorted = _experts(src_token, tile_expert, tile_first, n_used.astype(jnp.int32), h2,
                         w_exp_gate[0], w_exp_up[0], w_exp_down[0])
    y_p, y_s = _combine(dest_flat, ys_sorted, x1, rw, norm_final[None], n_p, n_s)

    return (y_p.reshape(n_batch, seq, d), y_s.reshape(s_batch, steps, d),
            new_conv_prompt, hgrn_p[None], new_conv_sample, hgrn_s[None])
```

```python
import functools

import numpy as np
import jax
import jax.numpy as jnp
from jax import lax
from jax.experimental import pallas as pl
from jax.experimental.pallas import tpu as pltpu

F32 = jnp.float32
BF16 = jnp.bfloat16
EPS = 1e-6
LANES = 128
SUBLANES = 8
HEAD_DIM = 128
HGRN_CHUNK = 64
HGRN_HEADS_PER_STEP = 2
HGRN_CHUNK_GROUP = 4
FAST_DECAY_LIMIT = -60.0
N_GROUPS = 4
EXPERTS_PER_GROUP = 8
VMEM_LIMIT = 56 * 1024 * 1024
TOKEN_TILE = 256
INPROJ_TILE = 256
CONV_TILE = 256
CONV_HALO = 32
CONV_ROWS = 64
EXPERT_TILE = 256
SAMPLE_BATCH_BLOCK = 8
GATHER_UNROLL = 8


def _sigmoid(x):
    return 1.0 / (1.0 + jnp.exp(-x))


def _silu(x):
    return x * _sigmoid(x)


def _rms(x, g):
    return x * lax.rsqrt(jnp.mean(x * x, axis=-1, keepdims=True) + EPS) * g


def _split3(x):
    h1 = x.astype(BF16)
    r1 = x - h1.astype(F32)
    h2 = r1.astype(BF16)
    h3 = (r1 - h2.astype(F32)).astype(BF16)
    return h1, h2, h3


def _params(sem, flags=None):
    return pltpu.CompilerParams(dimension_semantics=sem, vmem_limit_bytes=VMEM_LIMIT, flags=flags)


def _inproj_kernel(xp_ref, xs_ref, g_ref, w_ref, lbl_ref,
                   u_ref, q_ref, k_ref, lf_ref, v_ref, gs_ref, h_ref, *, n_prompt_tiles, ch, hk):
    i = pl.program_id(0)

    @pl.when(i < n_prompt_tiles)
    def _():
        h_ref[...] = _rms(xp_ref[...], g_ref[...]).astype(BF16)

    @pl.when(i >= n_prompt_tiles)
    def _():
        h_ref[...] = _rms(xs_ref[...], g_ref[...]).astype(BF16)

    h = h_ref[...]

    def proj(c0, width):
        return jnp.dot(h, w_ref[:, c0:c0 + width], preferred_element_type=F32)

    a = proj(0, ch)
    ga = proj(ch, ch)
    u_ref[...] = a * _sigmoid(ga)
    q = proj(2 * ch, hk)
    q_ref[...] = _silu(q).astype(BF16)
    f = proj(2 * ch + hk, hk)
    lbl = lbl_ref[...]
    e = jnp.exp(lbl - jnp.max(lbl, axis=0, keepdims=True))
    lb = e[0:1, :] / jnp.sum(e, axis=0, keepdims=True)
    fg = lb + (1.0 - lb) * _sigmoid(f)
    k_ref[...] = (1.0 - fg).astype(BF16)
    lf_ref[...] = jnp.log(fg)
    v_ref[...] = proj(2 * ch + 2 * hk, hk).astype(BF16)
    g = proj(2 * ch + 3 * hk, hk)
    gs_ref[...] = _silu(g).astype(BF16)


def _inproj(xp, xs, norm_g, w_in_bf16, lb_logits, ch, hk):
    n_p, d = xp.shape
    n_s = xs.shape[0]
    tm = min(INPROJ_TILE, n_s)
    assert n_p % tm == 0 and n_s % tm == 0
    npt, nst = n_p // tm, n_s // tm
    n = n_p + n_s
    cols = w_in_bf16.shape[1]
    row = lambda i: (i, 0)
    const = lambda i: (0, 0)
    outs = [jax.ShapeDtypeStruct((n, ch), F32)] + [
        jax.ShapeDtypeStruct((n, hk), dt) for dt in (BF16, BF16, F32, BF16, BF16)]
    return pl.pallas_call(
        functools.partial(_inproj_kernel, n_prompt_tiles=npt, ch=ch, hk=hk),
        grid=(npt + nst,),
        in_specs=[
            pl.BlockSpec((tm, d), lambda i: (jnp.minimum(i, npt - 1), 0)),
            pl.BlockSpec((tm, d), lambda i: (jnp.maximum(i - npt, 0), 0)),
            pl.BlockSpec((1, d), const),
            pl.BlockSpec((d, cols), const, pipeline_mode=pl.Buffered(1)),
            pl.BlockSpec(lb_logits.shape, const),
        ],
        out_specs=[pl.BlockSpec((tm, ch), row)] + [pl.BlockSpec((tm, hk), row)] * 5,
        out_shape=outs,
        scratch_shapes=[pltpu.VMEM((tm, d), BF16)],
        compiler_params=_params(("arbitrary",)),
        name="inproj",
    )(xp, xs, norm_g, w_in_bf16, lb_logits)


def _ln_silu(c, g, b):
    mu = jnp.mean(c, axis=-1, keepdims=True)
    d = c - mu
    var = jnp.mean(d * d, axis=-1, keepdims=True)
    return _silu(d * lax.rsqrt(var + EPS) * g + b)


def _conv_prompt_kernel(halo_ref, cur_ref, w_ref, b_ref, lg_ref, lb_ref, c_ref, ext_ref, acc_ref, *, width):
    t = pl.program_id(1)
    tt, ch = cur_ref.shape
    halo = halo_ref[...]
    ext_ref[0:CONV_HALO, :] = jnp.where(t == 0, jnp.zeros_like(halo), halo)
    ext_ref[CONV_HALO:, :] = cur_ref[...]
    off = CONV_HALO - (width - 1)
    rows = min(CONV_ROWS, tt)
    for l in range(ch // LANES):
        ls = slice(l * LANES, (l + 1) * LANES)
        wl = w_ref[:, ls]
        bl = b_ref[:, ls]
        for r0 in range(0, tt, rows):
            acc = jnp.broadcast_to(bl, (rows, LANES))
            for j in range(width):
                acc = acc + ext_ref[r0 + off + j:r0 + off + j + rows, ls] * wl[j:j + 1, :]
            acc_ref[r0:r0 + rows, ls] = acc
    c_ref[...] = _ln_silu(acc_ref[...], lg_ref[...], lb_ref[...]).astype(BF16)


def _conv_prompt(u, n_batch, seq, w_dw, b_dw, ln_g, ln_b):
    ch = u.shape[1]
    width = w_dw.shape[0]
    tt = min(CONV_TILE, seq)
    assert seq % tt == 0 and tt % CONV_HALO == 0 and width - 1 <= CONV_HALO
    nt = seq // tt
    hb = tt // CONV_HALO
    const = lambda b, t: (0, 0)
    return pl.pallas_call(
        functools.partial(_conv_prompt_kernel, width=width),
        grid=(n_batch, nt),
        in_specs=[
            pl.BlockSpec((CONV_HALO, ch), lambda b, t: (jnp.maximum((b * nt + t) * hb - 1, 0), 0)),
            pl.BlockSpec((tt, ch), lambda b, t: (b * nt + t, 0)),
            pl.BlockSpec((width, ch), const),
            pl.BlockSpec((1, ch), const),
            pl.BlockSpec((1, ch), const),
            pl.BlockSpec((1, ch), const),
        ],
        out_specs=pl.BlockSpec((tt, ch), lambda b, t: (b * nt + t, 0)),
        out_shape=jax.ShapeDtypeStruct((n_batch * seq, ch), BF16),
        scratch_shapes=[pltpu.VMEM((CONV_HALO + tt, ch), F32), pltpu.VMEM((tt, ch), F32)],
        compiler_params=_params(("arbitrary", "arbitrary")),
        name="conv_prompt",
    )(u, u, w_dw, b_dw, ln_g, ln_b)


def _conv_sample_kernel(state_ref, u_ref, ws_ref, wu_ref, b_ref, lg_ref, lb_ref, c_ref, new_ref, acc_ref, *, steps):
    bb, hist, _ = state_ref.shape
    for b in range(bb):
        st = state_ref[b]
        ub = u_ref[b * steps:(b + 1) * steps, :]
        new_ref[b, 0:hist - steps, :] = st[steps:, :]
        new_ref[b, hist - steps:hist, :] = ub
        for t in range(steps):
            row = (jnp.sum(st * ws_ref[t], axis=0, keepdims=True)
                   + jnp.sum(ub * wu_ref[t], axis=0, keepdims=True))
            acc_ref[b * steps + t:b * steps + t + 1, :] = row + b_ref[...]
    c_ref[...] = _ln_silu(acc_ref[...], lg_ref[...], lb_ref[...]).astype(BF16)


def _conv_sample(u_s, state, w_dw, b_dw, ln_g, ln_b):
    n_batch, hist, ch = state.shape
    width = w_dw.shape[0]
    steps = u_s.shape[0] // n_batch
    assert hist == width - 1 and steps <= hist
    ws = jnp.stack([jnp.concatenate([jnp.zeros((t, ch), F32), w_dw[:hist - t]], axis=0) for t in range(steps)])
    wu = jnp.stack([jnp.concatenate([w_dw[hist - t:], jnp.zeros((steps - 1 - t, ch), F32)], axis=0)
                    for t in range(steps)])
    bb = min(SAMPLE_BATCH_BLOCK, n_batch)
    assert n_batch % bb == 0
    c3 = lambda i: (0, 0, 0)
    c2 = lambda i: (0, 0)
    return pl.pallas_call(
        functools.partial(_conv_sample_kernel, steps=steps),
        grid=(n_batch // bb,),
        in_specs=[
            pl.BlockSpec((bb, hist, ch), lambda i: (i, 0, 0)),
            pl.BlockSpec((bb * steps, ch), lambda i: (i, 0)),
            pl.BlockSpec((steps, hist, ch), c3),
            pl.BlockSpec((steps, steps, ch), c3),
            pl.BlockSpec((1, ch), c2),
            pl.BlockSpec((1, ch), c2),
            pl.BlockSpec((1, ch), c2),
        ],
        out_specs=[pl.BlockSpec((bb * steps, ch), lambda i: (i, 0)),
                   pl.BlockSpec((bb, hist, ch), lambda i: (i, 0, 0))],
        out_shape=[jax.ShapeDtypeStruct((n_batch * steps, ch), BF16),
                   jax.ShapeDtypeStruct((n_batch, hist, ch), F32)],
        scratch_shapes=[pltpu.VMEM((bb * steps, ch), F32)],
        compiler_params=_params(("arbitrary",)),
        name="conv_sample",
    )(state, u_s, ws, wu, b_dw, ln_g, ln_b)


def _chunk_constants(c):
    levels = int(np.log2(c))
    assert 2 ** levels == c
    t = np.arange(c)[:, None]
    j = np.arange(c)[None, :]
    mats = [(j <= t)]
    right = np.zeros((levels, c, LANES), np.float32)
    left = np.zeros((levels, c, LANES), np.float32)
    same = np.zeros((levels, c, c), np.float32)
    for lv in range(levels):
        half = c >> (lv + 1)
        split = (t // (2 * half)) * (2 * half) + half
        is_right = t >= split
        mats.append(np.where(is_right, (j >= split) & (j <= t), (j > t) & (j < split)))
        right[lv] = is_right
        left[lv] = ~is_right
        same[lv] = (t // (2 * half)) == (j // (2 * half))
    mats.append(j > t)
    dm = np.concatenate(mats, axis=0).astype(np.float32)
    return (jnp.asarray(dm, BF16), jnp.asarray(right), jnp.asarray(left), jnp.asarray(same))


def _hgrn_prompt_kernel(q_ref, k_ref, lf_ref, v_ref, gs_ref, gn_ref, dm_ref, rm_ref, lm_ref, bm_ref,
                        o_ref, s_ref, oi_ref, qe_ref, dec_ref, kv_ref, st_ref, *, chunk):
    seq, width = q_ref.shape
    hp = width // HEAD_DIM
    levels = rm_ref.shape[0]
    n_chunks = seq // chunk
    nt = (((1,), (1,)), ((), ()))
    tn = (((0,), (0,)), ((), ()))
    row_i = lax.broadcasted_iota(jnp.int32, (chunk, chunk), 0)
    col_i = lax.broadcasted_iota(jnp.int32, (chunk, chunk), 1)

    def intra(chains, robust):
        n = len(chains)
        hs = [slice(h * HEAD_DIM, (h + 1) * HEAD_DIM) for _, h in chains]
        rows = [pl.ds(pl.multiple_of(c * chunk, chunk), chunk) for c, _ in chains]
        q = [q_ref[rows[i], hs[i]].astype(F32) for i in range(n)]
        k = [k_ref[rows[i], hs[i]].astype(F32) for i in range(n)]
        v = [v_ref[rows[i], hs[i]] for i in range(n)]
        lsp = [_split3(lf_ref[rows[i], hs[i]]) for i in range(n)]

        def decay_sums(i, dm):
            return (jnp.dot(dm, lsp[i][0], preferred_element_type=F32)
                    + jnp.dot(dm, lsp[i][1], preferred_element_type=F32)
                    + jnp.dot(dm, lsp[i][2], preferred_element_type=F32))

        if robust:
            ex = [jnp.exp(decay_sums(i, dm_ref[...])) for i in range(n)]
            e_cum = [e[0:chunk] for e in ex]
            e_tail = [e[(levels + 1) * chunk:(levels + 2) * chunk] for e in ex]
            att = [jnp.where(row_i == col_i, jnp.sum(q[i] * k[i], axis=-1, keepdims=True), 0.0) for i in range(n)]
            for lv in range(levels):
                for i in range(n):
                    e = ex[i][(lv + 1) * chunk:(lv + 2) * chunk]
                    ql = (q[i] * e * rm_ref[lv]).astype(BF16)
                    kl = (k[i] * e * lm_ref[lv]).astype(BF16)
                    att[i] = att[i] + bm_ref[lv] * lax.dot_general(ql, kl, nt, preferred_element_type=F32)
            qe = [(q[i] * e_cum[i]).astype(BF16) for i in range(n)]
        else:
            sums = [decay_sums(i, dm_ref[0:chunk, :]) for i in range(n)]
            tails = [decay_sums(i, dm_ref[(levels + 1) * chunk:(levels + 2) * chunk, :]) for i in range(n)]
            e_cum = [jnp.exp(s) for s in sums]
            e_tail = [jnp.exp(t) for t in tails]
            qe = [(q[i] * e_cum[i]).astype(BF16) for i in range(n)]
            kn = [(k[i] * jnp.exp(-sums[i])).astype(BF16) for i in range(n)]
            att = [lax.dot_general(qe[i], kn[i], nt, preferred_element_type=F32) for i in range(n)]
            att = [jnp.where(row_i >= col_i, a, 0.0) for a in att]
        kv = [lax.dot_general(v[i], (k[i] * e_tail[i]).astype(BF16), tn, preferred_element_type=F32) for i in range(n)]
        oi = [jnp.dot(att[i].astype(BF16), v[i], preferred_element_type=F32) for i in range(n)]
        for i, (c, h) in enumerate(chains):
            kv_ref[c, h] = kv[i]
            oi_ref[rows[i], hs[i]] = oi[i]
            qe_ref[rows[i], hs[i]] = qe[i]
            dec_ref[c, :, hs[i]] = e_cum[i][chunk - 1:chunk, :]

    def finish(chains):
        n = len(chains)
        hs = [slice(h * HEAD_DIM, (h + 1) * HEAD_DIM) for _, h in chains]
        rows = [pl.ds(pl.multiple_of(c * chunk, chunk), chunk) for c, _ in chains]
        inter = [lax.dot_general(qe_ref[rows[i], hs[i]], st_ref[c, h], nt, preferred_element_type=F32)
                 for i, (c, h) in enumerate(chains)]
        o = [oi_ref[rows[i], hs[i]] + inter[i] for i in range(n)]
        o = [_rms(o[i], gn_ref[...]) * gs_ref[rows[i], hs[i]].astype(F32) for i in range(n)]
        for i in range(n):
            o_ref[rows[i], hs[i]] = o[i].astype(BF16)

    def run(robust):
        group = min(HGRN_CHUNK_GROUP, n_chunks)

        def intra_body(g, carry):
            intra([(g * group + u, h) for u in range(group) for h in range(hp)], robust)
            return carry
        lax.fori_loop(0, n_chunks // group, intra_body, 0)

        def scan_body(c, sts):
            new = []
            for h in range(hp):
                st_ref[c, h] = sts[h].astype(BF16)
                new.append(sts[h] * dec_ref[c, :, h * HEAD_DIM:(h + 1) * HEAD_DIM] + kv_ref[c, h])
            return tuple(new)
        sts = lax.fori_loop(0, n_chunks, scan_body, tuple(jnp.zeros((HEAD_DIM, HEAD_DIM), F32) for _ in range(hp)))
        for h in range(hp):
            s_ref[0, h] = sts[h].T

        def finish_body(g, carry):
            finish([(g * group + u, h) for u in range(group) for h in range(hp)])
            return carry
        lax.fori_loop(0, n_chunks // group, finish_body, 0)

    lf_all = lf_ref[...].reshape(n_chunks, chunk, width)
    slowest = jnp.min(jnp.sum(lf_all, axis=1))
    fast = slowest >= FAST_DECAY_LIMIT
    pl.when(fast)(lambda: run(False))
    pl.when(jnp.logical_not(fast))(lambda: run(True))


def _hgrn_prompt(q, k, lf, v, gs, g_norm, n_batch, seq, heads):
    chunk = HGRN_CHUNK if seq % HGRN_CHUNK == 0 else seq
    dm, rm, lm, bm = _chunk_constants(chunk)
    hp = min(HGRN_HEADS_PER_STEP, heads)
    assert heads % hp == 0
    n_chunks = seq // chunk
    tok = pl.BlockSpec((seq, hp * HEAD_DIM), lambda b, h: (b, h))
    c2 = lambda b, h: (0, 0)
    c3 = lambda b, h: (0, 0, 0)
    return pl.pallas_call(
        functools.partial(_hgrn_prompt_kernel, chunk=chunk),
        grid=(n_batch, heads // hp),
        in_specs=[tok, tok, tok, tok, tok,
                  pl.BlockSpec((1, HEAD_DIM), c2),
                  pl.BlockSpec(dm.shape, c2),
                  pl.BlockSpec(rm.shape, c3), pl.BlockSpec(lm.shape, c3), pl.BlockSpec(bm.shape, c3)],
        out_specs=[tok, pl.BlockSpec((1, hp, HEAD_DIM, HEAD_DIM), lambda b, h: (b, h, 0, 0))],
        out_shape=[jax.ShapeDtypeStruct((n_batch * seq, heads * HEAD_DIM), BF16),
                   jax.ShapeDtypeStruct((n_batch, heads, HEAD_DIM, HEAD_DIM), F32)],
        scratch_shapes=[pltpu.VMEM((seq, hp * HEAD_DIM), F32),
                        pltpu.VMEM((seq, hp * HEAD_DIM), BF16),
                        pltpu.VMEM((n_chunks, 1, hp * HEAD_DIM), F32),
                        pltpu.VMEM((n_chunks, hp, HEAD_DIM, HEAD_DIM), F32),
                        pltpu.VMEM((n_chunks, hp, HEAD_DIM, HEAD_DIM), BF16)],
        compiler_params=_params(("arbitrary", "arbitrary")),
        name="hgrn_prompt",
    )(q, k, lf, v, gs, g_norm, dm, rm, lm, bm)


def _hgrn_sample_kernel(q_ref, k_ref, lf_ref, v_ref, gs_ref, gn_ref, s0_ref, o_ref, s_ref, inter_ref, *, steps):
    rows, width = q_ref.shape
    heads = width // HEAD_DIM
    q = q_ref[...].astype(F32)
    k = k_ref[...].astype(F32)
    v = v_ref[...].astype(F32)
    lf = lf_ref[...]
    step = lax.broadcasted_iota(jnp.int32, (rows, 1), 0) & (steps - 1)

    def back(x, d):
        return pltpu.roll(x, d, 0)

    cum = lf
    for d in range(1, steps):
        cum = cum + jnp.where(step >= d, back(lf, d), 0.0)
    tail = jnp.zeros_like(lf)
    for d in range(1, steps):
        tail = tail + jnp.where(step + d < steps, pltpu.roll(lf, rows - d, 0), 0.0)

    def head_sum(x):
        return [jnp.sum(x[:, h * HEAD_DIM:(h + 1) * HEAD_DIM], axis=-1, keepdims=True) for h in range(heads)]

    def head_scale(cols, x):
        return jnp.concatenate([cols[h] * x[:, h * HEAD_DIM:(h + 1) * HEAD_DIM] for h in range(heads)], axis=1)

    intra = head_scale(head_sum(q * k), v)
    for d in range(1, steps):
        ok = step >= d
        rel = jnp.where(ok, cum - back(cum, d), 0.0)
        w = jnp.where(ok, q * back(k, d) * jnp.exp(rel), 0.0)
        intra = intra + head_scale(head_sum(w), back(v, d))

    qe = (q * jnp.exp(cum)).astype(BF16)
    kd = k * jnp.exp(tail)
    total = jnp.exp(cum)
    per = SUBLANES // steps
    grp = lax.broadcasted_iota(jnp.int32, (SUBLANES, 1), 0)
    tn = (((0,), (0,)), ((), ()))
    for b in range(rows // steps):
        r8 = (b // per) * SUBLANES
        lo = (b % per) * steps
        mine = (grp >= lo) & (grp < lo + steps)
        spare = (lo + steps) % SUBLANES
        for h in range(heads):
            hs = slice(h * HEAD_DIM, (h + 1) * HEAD_DIM)
            s0 = s0_ref[b, h]
            res = jnp.dot(qe[r8:r8 + SUBLANES, hs], s0.astype(BF16), preferred_element_type=F32)
            inter_ref[b * steps:(b + 1) * steps, hs] = res[lo:lo + steps]
            d1, d2, d3 = _split3(total[r8 + lo + steps - 1:r8 + lo + steps, hs])
            dec = jnp.where(grp == spare, d1.astype(F32),
                            jnp.where(grp == spare + 1, d2.astype(F32),
                                      jnp.where(grp == spare + 2, d3.astype(F32), 0.0)))
            lhs = jnp.where(mine, kd[r8:r8 + SUBLANES, hs], dec).astype(BF16)
            vb = jnp.where(mine, v[r8:r8 + SUBLANES, hs], 0.0)
            ones = jnp.where(mine, 0.0, 1.0) * jnp.ones((SUBLANES, HEAD_DIM), F32)
            rhs = jnp.concatenate([vb, ones], axis=1).astype(BF16)
            upd = lax.dot_general(lhs, rhs, tn, preferred_element_type=F32)
            s_ref[b, h] = upd[:, HEAD_DIM:] * s0 + upd[:, :HEAD_DIM]
    o = intra + inter_ref[...]
    gn = gn_ref[...]
    o = jnp.concatenate([_rms(o[:, h * HEAD_DIM:(h + 1) * HEAD_DIM], gn) for h in range(heads)], axis=1)
    o_ref[...] = (o * gs_ref[...].astype(F32)).astype(BF16)


def _hgrn_sample(q, k, lf, v, gs, g_norm, s0, row0, steps):
    n_batch, heads = s0.shape[:2]
    width = heads * HEAD_DIM
    assert steps & (steps - 1) == 0 and SUBLANES - steps >= 3
    bb = min(SAMPLE_BATCH_BLOCK, n_batch)
    rows = bb * steps
    assert n_batch % bb == 0 and rows % SUBLANES == 0 and row0 % rows == 0
    blk0 = row0 // rows
    tok = pl.BlockSpec((rows, width), lambda i: (blk0 + i, 0))
    st = pl.BlockSpec((bb, heads, HEAD_DIM, HEAD_DIM), lambda i: (i, 0, 0, 0))
    return pl.pallas_call(
        functools.partial(_hgrn_sample_kernel, steps=steps),
        grid=(n_batch // bb,),
        in_specs=[tok, tok, tok, tok, tok, pl.BlockSpec((1, HEAD_DIM), lambda i: (0, 0)), st],
        out_specs=[pl.BlockSpec((rows, width), lambda i: (i, 0)), st],
        out_shape=[jax.ShapeDtypeStruct((n_batch * steps, width), BF16),
                   jax.ShapeDtypeStruct(s0.shape, F32)],
        scratch_shapes=[pltpu.VMEM((rows, width), F32)],
        compiler_params=_params(("arbitrary",)),
        name="hgrn_sample",
    )(q, k, lf, v, gs, g_norm, s0)


def _outproj_kernel(cp_ref, cs_ref, op_ref, os_ref, xp_ref, xs_ref, wc_ref, wo_ref, g_ref, wr_ref, br_ref, tri_ref,
                    x1_ref, h2_ref, ri_ref, rw_ref, cnt_ref, run_ref, *, n_prompt_tiles, n_experts):
    i = pl.program_id(0)
    is_p = i < n_prompt_tiles

    @pl.when(i == 0)
    def _():
        run_ref[...] = jnp.zeros_like(run_ref)

    def mix(c_ref, o_ref, x_ref):
        y = jnp.dot(c_ref[...], wc_ref[...], preferred_element_type=F32)
        y = y + jnp.dot(o_ref[...], wo_ref[...], preferred_element_type=F32)
        x1_ref[...] = x_ref[...] + y

    pl.when(is_p)(lambda: mix(cp_ref, op_ref, xp_ref))
    pl.when(jnp.logical_not(is_p))(lambda: mix(cs_ref, os_ref, xs_ref))

    h2 = _rms(x1_ref[...], g_ref[...])
    h2_ref[...] = h2
    a1, a2, _ = _split3(h2)
    logits = (jnp.dot(a1, wr_ref[0], preferred_element_type=F32)
              + jnp.dot(a1, wr_ref[1], preferred_element_type=F32)
              + jnp.dot(a2, wr_ref[0], preferred_element_type=F32)) + br_ref[...]
    tm = logits.shape[0]
    lane = lax.broadcasted_iota(jnp.int32, (tm, LANES), 1)
    lane_f = lane.astype(F32)
    neg = jnp.float32(-jnp.inf)

    def top(x):
        m = jnp.max(x, axis=-1, keepdims=True)
        idx = jnp.min(jnp.where(x == m, lane_f, float(LANES)), axis=-1, keepdims=True)
        return m, idx.astype(jnp.int32)

    is_group = (lane >= n_experts) & (lane < n_experts + N_GROUPS)
    gl = jnp.where(is_group, logits, neg)
    gmax, gidx = top(gl)
    p_top = 1.0 / jnp.sum(jnp.exp(gl - gmax), axis=-1, keepdims=True)
    g_lo = (gidx - n_experts) * EXPERTS_PER_GROUP
    el = jnp.where((lane >= g_lo) & (lane < g_lo + EXPERTS_PER_GROUP), logits, neg)
    v1, e1 = top(el)
    v2, e2 = top(jnp.where(lane == e1, neg, el))
    t = jnp.exp(v2 - v1)
    w1 = p_top / (1.0 + t)
    w2 = p_top * t / (1.0 + t)
    hot = ((lane == e1) | (lane == e2)).astype(F32)
    before = run_ref[...] + jnp.dot(tri_ref[...], hot.astype(BF16), preferred_element_type=F32)
    r1 = jnp.sum(jnp.where(lane == e1, before, 0.0), axis=-1, keepdims=True)
    r2 = jnp.sum(jnp.where(lane == e2, before, 0.0), axis=-1, keepdims=True)
    run_ref[...] = run_ref[...] + jnp.sum(hot, axis=0, keepdims=True)
    cnt_ref[...] = run_ref[...]
    info = jnp.where(lane == 0, e1.astype(F32), jnp.where(lane == 1, e2.astype(F32),
                     jnp.where(lane == 2, r1, jnp.where(lane == 3, r2, 0.0))))
    ri_ref[...] = info.T[0:SUBLANES, :].astype(jnp.int32)
    rw_ref[...] = jnp.where(lane == 0, w1, jnp.where(lane == 1, w2, 0.0))


def _outproj(c_p, c_s, o_p, o_s, xp, xs, w_out_bf16, norm_g, w_router3, b_router, n_experts):
    n_p, d = xp.shape
    n_s = xs.shape[0]
    tm = min(TOKEN_TILE, n_s)
    npt, nst = n_p // tm, n_s // tm
    n = n_p + n_s
    ch = c_p.shape[1]
    hv = o_p.shape[1]
    wc, wo = w_out_bf16[:ch], w_out_bf16[ch:]
    tri = jnp.asarray(np.tril(np.ones((tm, tm), np.float32), -1), BF16)
    pidx = lambda i: (jnp.minimum(i, npt - 1), 0)
    sidx = lambda i: (jnp.maximum(i - npt, 0), 0)
    row = lambda i: (i, 0)
    c2 = lambda i: (0, 0)
    return pl.pallas_call(
        functools.partial(_outproj_kernel, n_prompt_tiles=npt, n_experts=n_experts),
        grid=(npt + nst,),
        in_specs=[
            pl.BlockSpec((tm, ch), pidx), pl.BlockSpec((tm, ch), sidx),
            pl.BlockSpec((tm, hv), pidx), pl.BlockSpec((tm, hv), sidx),
            pl.BlockSpec((tm, d), pidx), pl.BlockSpec((tm, d), sidx),
            pl.BlockSpec((ch, d), c2), pl.BlockSpec((hv, d), c2),
            pl.BlockSpec((1, d), c2),
            pl.BlockSpec((2, d, LANES), lambda i: (0, 0, 0)),
            pl.BlockSpec((1, LANES), c2),
            pl.BlockSpec((tm, tm), c2),
        ],
        out_specs=[pl.BlockSpec((tm, d), row), pl.BlockSpec((tm, d), row),
                   pl.BlockSpec((SUBLANES, tm), lambda i: (0, i)), pl.BlockSpec((tm, LANES), row),
                   pl.BlockSpec((1, LANES), c2)],
        out_shape=[jax.ShapeDtypeStruct((n, d), F32), jax.ShapeDtypeStruct((n, d), F32),
                   jax.ShapeDtypeStruct((SUBLANES, n), jnp.int32), jax.ShapeDtypeStruct((n, LANES), F32),
                   jax.ShapeDtypeStruct((1, LANES), F32)],
        scratch_shapes=[pltpu.VMEM((1, LANES), F32)],
        compiler_params=_params(("arbitrary",)),
        name="outproj",
    )(c_p, c_s, o_p, o_s, xp, xs, wc, wo, norm_g, w_router3, b_router, tri)


def _experts_kernel(src_ref, te_ref, tf_ref, nu_ref, h2_ref, wg_ref, wu_ref, wd_ref, y_ref,
                    x_ref, wgb_ref, wub_ref, wdb_ref, sem):
    i = pl.program_id(0)
    n_used = nu_ref[0]
    rows = x_ref.shape[1]

    def row_copy(tile, slot, r):
        return pltpu.make_async_copy(h2_ref.at[pl.ds(src_ref[tile * rows + r], 1)],
                                     x_ref.at[slot, pl.ds(r, 1)], sem.at[slot])

    def gather(tile, slot):
        def start(r, carry):
            row_copy(tile, slot, r).start()
            return carry
        lax.fori_loop(0, rows, start, 0, unroll=GATHER_UNROLL)

    @pl.when(i == 0)
    def _():
        gather(0, 0)

    @pl.when(i + 1 < n_used)
    def _():
        gather(i + 1, (i + 1) % 2)

    @pl.when(tf_ref[i] == 1)
    def _():
        wgb_ref[...] = wg_ref[...].astype(BF16)
        wub_ref[...] = wu_ref[...].astype(BF16)
        wdb_ref[...] = wd_ref[...].astype(BF16)

    @pl.when(i < n_used)
    def _():
        slot = i % 2

        def wait(r, carry):
            row_copy(i, slot, r).wait()
            return carry
        lax.fori_loop(0, rows, wait, 0, unroll=GATHER_UNROLL)
        x = x_ref[slot].astype(BF16)
        hg = jnp.dot(x, wgb_ref[...], preferred_element_type=F32)
        hu = jnp.dot(x, wub_ref[...], preferred_element_type=F32)
        hid = (_silu(hg) * hu).astype(BF16)
        y_ref[...] = jnp.dot(hid, wdb_ref[...], preferred_element_type=F32)

    @pl.when(i >= n_used)
    def _():
        y_ref[...] = jnp.zeros_like(y_ref)


def _experts(src_token, tile_expert, tile_first, n_used, h2, w_gate, w_up, w_down):
    n_rows = src_token.shape[0]
    d = h2.shape[1]
    f = w_gate.shape[2]
    n_tiles = n_rows // EXPERT_TILE
    tile = lambda i, src, te, tf, nu: (i, 0)
    wsel = lambda i, src, te, tf, nu: (te[i], 0, 0)
    return pl.pallas_call(
        _experts_kernel,
        grid_spec=pltpu.PrefetchScalarGridSpec(
            num_scalar_prefetch=4, grid=(n_tiles,),
            in_specs=[pl.BlockSpec(memory_space=pl.ANY),
                      pl.BlockSpec((None, d, f), wsel), pl.BlockSpec((None, d, f), wsel),
                      pl.BlockSpec((None, f, d), wsel)],
            out_specs=pl.BlockSpec((EXPERT_TILE, d), tile),
            scratch_shapes=[pltpu.VMEM((2, EXPERT_TILE, d), F32),
                            pltpu.VMEM((d, f), BF16), pltpu.VMEM((d, f), BF16), pltpu.VMEM((f, d), BF16),
                            pltpu.SemaphoreType.DMA((2,))],
        ),
        out_shape=jax.ShapeDtypeStruct((n_rows, d), F32),
        compiler_params=_params(("arbitrary",)),
        name="experts",
    )(src_token, tile_expert, tile_first, n_used, h2, w_gate, w_up, w_down)


def _combine_kernel(dest_ref, ys_ref, x1_ref, rw_ref, g_ref, yp_ref, ysm_ref, buf_ref, sem, *, n_prompt_tiles):
    i = pl.program_id(0)
    tm = x1_ref.shape[0]

    n_tokens = dest_ref.shape[0] // 2
    base = i * tm

    def copy(r, slot):
        return pltpu.make_async_copy(ys_ref.at[pl.ds(dest_ref[slot * n_tokens + base + r], 1)],
                                     buf_ref.at[slot, pl.ds(r, 1)], sem)

    def start(r, carry):
        copy(r, 0).start()
        copy(r, 1).start()
        return carry

    def wait(r, carry):
        copy(r, 0).wait()
        copy(r, 1).wait()
        return carry

    lax.fori_loop(0, tm, start, 0, unroll=GATHER_UNROLL)
    lax.fori_loop(0, tm, wait, 0, unroll=GATHER_UNROLL)
    rw = rw_ref[...]
    x2 = x1_ref[...] + rw[:, 0:1] * buf_ref[0] + rw[:, 1:2] * buf_ref[1]
    y = _rms(x2, g_ref[...])

    @pl.when(i < n_prompt_tiles)
    def _():
        yp_ref[...] = y

    @pl.when(i >= n_prompt_tiles)
    def _():
        ysm_ref[...] = y


def _combine(dest_flat, ys, x1, rw, norm_g, n_p, n_s):
    n, d = x1.shape
    tm = min(CONV_TILE, n_s)
    npt, nst = n_p // tm, n_s // tm
    row = lambda i, dest: (i, 0)
    return pl.pallas_call(
        functools.partial(_combine_kernel, n_prompt_tiles=npt),
        grid_spec=pltpu.PrefetchScalarGridSpec(
            num_scalar_prefetch=1, grid=(npt + nst,),
            in_specs=[pl.BlockSpec(memory_space=pl.ANY),
                      pl.BlockSpec((tm, d), row), pl.BlockSpec((tm, LANES), row),
                      pl.BlockSpec((1, d), lambda i, dest: (0, 0))],
            out_specs=[pl.BlockSpec((tm, d), lambda i, dest: (jnp.minimum(i, npt - 1), 0)),
                       pl.BlockSpec((tm, d), lambda i, dest: (jnp.maximum(i - npt, 0), 0))],
            scratch_shapes=[pltpu.VMEM((2, tm, d), F32), pltpu.SemaphoreType.DMA(())],
        ),
        out_shape=[jax.ShapeDtypeStruct((n_p, d), F32), jax.ShapeDtypeStruct((n_s, d), F32)],
        compiler_params=_params(("arbitrary",)),
        name="combine",
    )(dest_flat, ys, x1, rw, norm_g)


def kernel(x_prompt, x_sample, state_conv, state_hgrn, norm_mix, w_in, w_dw, b_dw, ln_conv_g, ln_conv_b, lb_logits, hgrn_norm_g, w_out, norm_ffn, w_router_group, b_router_group, w_router_expert, b_router_expert, w_exp_gate, w_exp_up, w_exp_down, norm_final):
    assert w_in.shape[0] == 1, "single-layer trunk"
    n_batch, seq, d = x_prompt.shape
    s_batch, steps, _ = x_sample.shape
    ch = w_dw.shape[-1]
    hk = lb_logits.shape[-1]
    heads = hk // HEAD_DIM
    n_experts = w_exp_gate.shape[1]
    assert n_experts == N_GROUPS * EXPERTS_PER_GROUP and n_experts + N_GROUPS <= LANES
    n_p, n_s = n_batch * seq, s_batch * steps
    n = n_p + n_s
    xp = x_prompt.reshape(n_p, d)
    xs = x_sample.reshape(n_s, d)

    u, q, k, lf, v, gs = _inproj(xp, xs, norm_mix, w_in[0].astype(BF16), lb_logits, ch, hk)

    c_p = _conv_prompt(u, n_batch, seq, w_dw[0], b_dw, ln_conv_g, ln_conv_b)
    u_s = u[n_p:]
    c_s, new_conv_sample = _conv_sample(u_s, state_conv[0], w_dw[0], b_dw, ln_conv_g, ln_conv_b)
    hist = state_conv.shape[2]
    new_conv_prompt = jnp.stack([u[(b + 1) * seq - hist:(b + 1) * seq] for b in range(n_batch)])

    o_p, hgrn_p = _hgrn_prompt(q, k, lf, v, gs, hgrn_norm_g, n_batch, seq, heads)
    o_s, hgrn_s = _hgrn_sample(q, k, lf, v, gs, hgrn_norm_g, state_hgrn[0], n_p, steps)

    w_r = jnp.concatenate([w_router_expert[0], w_router_group[0]], axis=1)
    w_r = jnp.pad(w_r, ((0, 0), (0, LANES - w_r.shape[1])))
    r1 = w_r.astype(BF16)
    r2 = (w_r - r1.astype(F32)).astype(BF16)
    b_r = jnp.pad(jnp.concatenate([b_router_expert[0], b_router_group[0]]), (0, LANES - n_experts - N_GROUPS))[None]
    x1, h2, ri, rw, counts = _outproj(c_p, c_s, o_p, o_s, xp, xs, w_out[0].astype(BF16), norm_ffn,
                                      jnp.stack([r1, r2]), b_r, n_experts)

    cnt = counts[0, :n_experts].astype(jnp.int32)
    tiles_per = (cnt + EXPERT_TILE - 1) // EXPERT_TILE
    tile_end = jnp.cumsum(tiles_per)
    row_start = (tile_end - tiles_per) * EXPERT_TILE
    n_tiles = (2 * n) // EXPERT_TILE + n_experts
    dest = row_start[ri[0:2]] + ri[2:4]
    dest_flat = dest.reshape(-1)
    n_used = tile_end[-1:]
    tid = jnp.minimum(jnp.arange(n_tiles, dtype=jnp.int32), n_used - 1)
    tile_expert = jnp.sum((tile_end[None, :] <= tid[:, None]).astype(jnp.int32), axis=1)
    prev = jnp.concatenate([jnp.full((1,), -1, jnp.int32), tile_expert[:-1]])
    tile_first = ((tile_expert != prev) & (jnp.arange(n_tiles) < n_used)).astype(jnp.int32)
    token = jnp.tile(jnp.arange(n, dtype=jnp.int32), 2)
    src_token =jnp.zeros((n_tiles * EXPERT_TILE,), jnp.int32).at[dest_flat].set(token, unique_indices=True)

    ys_sorted = _experts(src_token, tile_expert, tile_first, n_used.astype(jnp.int32), h2,
                         w_exp_gate[0], w_exp_up[0], w_exp_down[0])
    y_p, y_s = _combine(dest_flat, ys_sorted, x1, rw, norm_final[None], n_p, n_s)

    return (y_p.reshape(n_batch, seq, d), y_s.reshape(s_batch, steps, d),
            new_conv_prompt[None], hgrn_p[None], new_conv_sample[None], hgrn_s[None])
```

```python
import functools

import numpy as np
import jax
import jax.numpy as jnp
from jax import lax
from jax.experimental import pallas as pl
from jax.experimental.pallas import tpu as pltpu

F32 = jnp.float32
BF16 = jnp.bfloat16
EPS = 1e-6
LANES = 128
SUBLANES = 8
HEAD_DIM = 128
HGRN_CHUNK = 64
HGRN_HEADS_PER_STEP = 2
HGRN_CHUNK_GROUP = 8
FAST_DECAY_LIMIT = -60.0
N_GROUPS = 4
EXPERTS_PER_GROUP = 8
VMEM_LIMIT = 56 * 1024 * 1024
TOKEN_TILE = 256
INPROJ_TILE = 256
CONV_TILE = 256
CONV_HALO = 32
CONV_ROWS = 64
EXPERT_TILE = 256
SAMPLE_BATCH_BLOCK = 8
SAMPLE_CONV_BATCH_BLOCK = 16
GATHER_UNROLL = 8
MXU_COLS = 256


def _sigmoid(x):
    return 1.0 / (1.0 + jnp.exp(-x))


def _silu(x):
    return x * _sigmoid(x)


def _rms(x, g):
    return x * lax.rsqrt(jnp.mean(x * x, axis=-1, keepdims=True) + EPS) * g


def _split3(x):
    h1 = x.astype(BF16)
    r1 = x - h1.astype(F32)
    h2 = r1.astype(BF16)
    h3 = (r1 - h2.astype(F32)).astype(BF16)
    return h1, h2, h3


def _params(sem, flags=None):
    return pltpu.CompilerParams(dimension_semantics=sem, vmem_limit_bytes=VMEM_LIMIT, flags=flags)


def _inproj_kernel(xp_ref, xs_ref, g_ref, w_ref, lbl_ref,
                   u_ref, q_ref, k_ref, lf_ref, v_ref, gs_ref, h_ref, *, n_prompt_tiles, ch, hk):
    i = pl.program_id(0)

    @pl.when(i < n_prompt_tiles)
    def _():
        h_ref[...] = _rms(xp_ref[...], g_ref[...]).astype(BF16)

    @pl.when(i >= n_prompt_tiles)
    def _():
        h_ref[...] = _rms(xs_ref[...], g_ref[...]).astype(BF16)

    h = h_ref[...]

    def proj(c0, width):
        return jnp.dot(h, w_ref[:, c0:c0 + width], preferred_element_type=F32)

    a = proj(0, ch)
    ga = proj(ch, ch)
    u_ref[...] = a * _sigmoid(ga)
    q = proj(2 * ch, hk)
    q_ref[...] = _silu(q).astype(BF16)
    f = proj(2 * ch + hk, hk)
    lbl = lbl_ref[...]
    e = jnp.exp(lbl - jnp.max(lbl, axis=0, keepdims=True))
    lb = e[0:1, :] / jnp.sum(e, axis=0, keepdims=True)
    fg = lb + (1.0 - lb) * _sigmoid(f)
    k_ref[...] = (1.0 - fg).astype(BF16)
    lf_ref[...] = jnp.log(fg)
    g = proj(2 * ch + 3 * hk, hk)
    gs_ref[...] = _silu(g).astype(BF16)
    v_ref[...] = proj(2 * ch + 2 * hk, hk).astype(BF16)


def _inproj(xp, xs, norm_g, w_in_bf16, lb_logits, ch, hk):
    n_p, d = xp.shape
    n_s = xs.shape[0]
    tm = min(INPROJ_TILE, n_s)
    assert n_p % tm == 0 and n_s % tm == 0
    npt, nst = n_p // tm, n_s // tm
    n = n_p + n_s
    cols = w_in_bf16.shape[1]
    row = lambda i: (i, 0)
    const = lambda i: (0, 0)
    outs = [jax.ShapeDtypeStruct((n, ch), F32)] + [
        jax.ShapeDtypeStruct((n, hk), dt) for dt in (BF16, BF16, F32, BF16, BF16)]
    return pl.pallas_call(
        functools.partial(_inproj_kernel, n_prompt_tiles=npt, ch=ch, hk=hk),
        grid=(npt + nst,),
        in_specs=[
            pl.BlockSpec((tm, d), lambda i: (jnp.minimum(i, npt - 1), 0)),
            pl.BlockSpec((tm, d), lambda i: (jnp.maximum(i - npt, 0), 0)),
            pl.BlockSpec((1, d), const),
            pl.BlockSpec((d, cols), const, pipeline_mode=pl.Buffered(1)),
            pl.BlockSpec(lb_logits.shape, const),
        ],
        out_specs=[pl.BlockSpec((tm, ch), row)] + [pl.BlockSpec((tm, hk), row)] * 5,
        out_shape=outs,
        scratch_shapes=[pltpu.VMEM((tm, d), BF16)],
        compiler_params=_params(("arbitrary",)),
        name="inproj",
    )(xp, xs, norm_g, w_in_bf16, lb_logits)


def _ln_silu(c, g, b):
    mu = jnp.mean(c, axis=-1, keepdims=True)
    d = c - mu
    var = jnp.mean(d * d, axis=-1, keepdims=True)
    return _silu(d * lax.rsqrt(var + EPS) * g + b)


def _conv_prompt_kernel(halo_ref, cur_ref, w_ref, b_ref, lg_ref, lb_ref, c_ref, ext_ref, acc_ref, *, width):
    t = pl.program_id(1)
    tt, ch = cur_ref.shape
    halo = halo_ref[...]
    ext_ref[0:CONV_HALO, :] = jnp.where(t == 0, jnp.zeros_like(halo), halo)
    ext_ref[CONV_HALO:, :] = cur_ref[...]
    off = CONV_HALO - (width - 1)
    rows = min(CONV_ROWS, tt)
    for l in range(ch // LANES):
        ls = slice(l * LANES, (l + 1) * LANES)
        wl = w_ref[:, ls]
        bl = b_ref[:, ls]
        for r0 in range(0, tt, rows):
            acc = jnp.broadcast_to(bl, (rows, LANES))
            for res in range(SUBLANES):
                extra = SUBLANES if res else 0
                part = None
                for a in range((off + width - 1) // SUBLANES + 1):
                    j = SUBLANES * a + res - off
                    if 0 <= j < width:
                        lo = r0 + SUBLANES * a
                        term = ext_ref[lo:lo + rows + extra, ls] * wl[j:j + 1, :]
                        part = term if part is None else part + term
                acc = acc + part[res:res + rows]
            acc_ref[r0:r0 + rows, ls] = acc
    c_ref[...] = _ln_silu(acc_ref[...], lg_ref[...], lb_ref[...]).astype(BF16)


def _conv_prompt(u, n_batch, seq, w_dw, b_dw, ln_g, ln_b):
    ch = u.shape[1]
    width = w_dw.shape[0]
    tt = min(CONV_TILE, seq)
    assert seq % tt == 0 and tt % CONV_HALO == 0 and width - 1 <= CONV_HALO
    nt = seq // tt
    hb = tt // CONV_HALO
    const = lambda b, t: (0, 0)
    return pl.pallas_call(
        functools.partial(_conv_prompt_kernel, width=width),
        grid=(n_batch, nt),
        in_specs=[
            pl.BlockSpec((CONV_HALO, ch), lambda b, t: (jnp.maximum((b * nt + t) * hb - 1, 0), 0)),
            pl.BlockSpec((tt, ch), lambda b, t: (b * nt + t, 0)),
            pl.BlockSpec((width, ch), const),
            pl.BlockSpec((1, ch), const),
            pl.BlockSpec((1, ch), const),
            pl.BlockSpec((1, ch), const),
        ],
        out_specs=pl.BlockSpec((tt, ch), lambda b, t: (b * nt + t, 0)),
        out_shape=jax.ShapeDtypeStruct((n_batch * seq, ch), BF16),
        scratch_shapes=[pltpu.VMEM((CONV_HALO + tt, ch), F32), pltpu.VMEM((tt, ch), F32)],
        compiler_params=_params(("arbitrary", "arbitrary")),
        name="conv_prompt",
    )(u, u, w_dw, b_dw, ln_g, ln_b)


def _conv_sample_kernel(state_ref, u_ref, w_ref, b_ref, lg_ref, lb_ref, c_ref, new_ref):
    hist = state_ref.shape[0]
    steps = u_ref.shape[0]
    for r in range(hist - steps):
        new_ref[r] = state_ref[r + steps]
    for s in range(steps):
        new_ref[hist - steps + s] = u_ref[s]
    for t in range(steps):
        acc = jnp.broadcast_to(b_ref[...], u_ref.shape[1:])
        for r in range(t, hist):
            acc = acc + state_ref[r] * w_ref[r - t:r - t + 1, :]
        for s in range(t + 1):
            acc = acc + u_ref[s] * w_ref[hist - t + s:hist - t + s + 1, :]
        c_ref[t] = _ln_silu(acc, lg_ref[...], lb_ref[...]).astype(BF16)


def _conv_sample(u_t, state_t, w_dw, b_dw, ln_g, ln_b):
    hist, n_batch, ch = state_t.shape
    width = w_dw.shape[0]
    steps = u_t.shape[0]
    assert hist == width - 1 and steps <= hist
    bb = min(SAMPLE_CONV_BATCH_BLOCK, n_batch)
    assert n_batch % bb == 0
    c2 = lambda i: (0, 0)
    blk = lambda i: (0, i, 0)
    return pl.pallas_call(
        _conv_sample_kernel,
        grid=(n_batch // bb,),
        in_specs=[
            pl.BlockSpec((hist, bb, ch), blk),
            pl.BlockSpec((steps, bb, ch), blk),
            pl.BlockSpec((width, ch), c2),
            pl.BlockSpec((1, ch), c2),
            pl.BlockSpec((1, ch), c2),
            pl.BlockSpec((1, ch), c2),
        ],
        out_specs=[pl.BlockSpec((steps, bb, ch), blk), pl.BlockSpec((hist, bb, ch), blk)],
        out_shape=[jax.ShapeDtypeStruct((steps, n_batch, ch), BF16),
                   jax.ShapeDtypeStruct((hist, n_batch, ch), F32)],
        compiler_params=_params(("arbitrary",)),
        name="conv_sample",
    )(state_t, u_t, w_dw, b_dw, ln_g, ln_b)


def _chunk_constants(c):
    levels = int(np.log2(c))
    assert 2 ** levels == c
    t = np.arange(c)[:, None]
    j = np.arange(c)[None, :]
    mats = [(j <= t)]
    right = np.zeros((levels, c, LANES), np.float32)
    left = np.zeros((levels, c, LANES), np.float32)
    same = np.zeros((levels, c, c), np.float32)
    for lv in range(levels):
        half = c >> (lv + 1)
        split = (t // (2 * half)) * (2 * half) + half
        is_right = t >= split
        mats.append(np.where(is_right, (j >= split) & (j <= t), (j > t) & (j < split)))
        right[lv] = is_right
        left[lv] = ~is_right
        same[lv] = (t // (2 * half)) == (j // (2 * half))
    mats.append(j > t)
    dm = np.concatenate(mats, axis=0).astype(np.float32)
    dm = np.concatenate([dm, dm, dm], axis=1)
    return (jnp.asarray(dm, BF16), jnp.asarray(right), jnp.asarray(left), jnp.asarray(same))


def _hgrn_prompt_kernel(q_ref, k_ref, lf_ref, v_ref, gs_ref, gn_ref, dm_ref, rm_ref, lm_ref, bm_ref,
                        o_ref, s_ref, oi_ref, qe_ref, dec_ref, kv_ref, st_ref, *, chunk):
    seq, width = q_ref.shape
    hp = width // HEAD_DIM
    levels = rm_ref.shape[0]
    n_chunks = seq // chunk
    nt = (((1,), (1,)), ((), ()))
    tn = (((0,), (0,)), ((), ()))
    row_i = lax.broadcasted_iota(jnp.int32, (chunk, chunk), 0)
    col_i = lax.broadcasted_iota(jnp.int32, (chunk, chunk), 1)

    def intra(chains, robust):
        n = len(chains)
        hs = [slice(h * HEAD_DIM, (h + 1) * HEAD_DIM) for _, h in chains]
        rows = [pl.ds(pl.multiple_of(c * chunk, chunk), chunk) for c, _ in chains]
        q = [q_ref[rows[i], hs[i]].astype(F32) for i in range(n)]
        k = [k_ref[rows[i], hs[i]].astype(F32) for i in range(n)]
        v = [v_ref[rows[i], hs[i]] for i in range(n)]
        lsp = [jnp.concatenate(_split3(lf_ref[rows[i], hs[i]]), axis=0) for i in range(n)]

        def decay_sums(i, dm):
            return jnp.dot(dm, lsp[i], preferred_element_type=F32)

        if robust:
            ex = [jnp.exp(decay_sums(i, dm_ref[...])) for i in range(n)]
            e_cum = [e[0:chunk] for e in ex]
            e_tail = [e[(levels + 1) * chunk:(levels + 2) * chunk] for e in ex]
            att = [jnp.where(row_i == col_i, jnp.sum(q[i] * k[i], axis=-1, keepdims=True), 0.0) for i in range(n)]
            for lv in range(levels):
                for i in range(n):
                    e = ex[i][(lv + 1) * chunk:(lv + 2) * chunk]
                    ql = (q[i] * e * rm_ref[lv]).astype(BF16)
                    kl = (k[i] * e * lm_ref[lv]).astype(BF16)
                    att[i] = att[i] + bm_ref[lv] * lax.dot_general(ql, kl, nt, preferred_element_type=F32)
            qe = [(q[i] * e_cum[i]).astype(BF16) for i in range(n)]
        else:
            sums = [decay_sums(i, dm_ref[0:chunk, :]) for i in range(n)]
            tails = [s[chunk - 1:chunk, :] - s for s in sums]
            e_cum = [jnp.exp(s) for s in sums]
            e_tail = [jnp.exp(t) for t in tails]
            qe = [(q[i] * e_cum[i]).astype(BF16) for i in range(n)]
            kn = [(k[i] * jnp.exp(-sums[i])).astype(BF16) for i in range(n)]
            att = [lax.dot_general(qe[i], kn[i], nt, preferred_element_type=F32) for i in range(n)]
            att = [jnp.where(row_i >= col_i, a, 0.0) for a in att]
        kv = [lax.dot_general(v[i], (k[i] * e_tail[i]).astype(BF16), tn, preferred_element_type=F32) for i in range(n)]
        oi = [jnp.dot(att[i].astype(BF16), v[i], preferred_element_type=F32) for i in range(n)]
        for i, (c, h) in enumerate(chains):
            kv_ref[c, h] = kv[i]
            oi_ref[rows[i], hs[i]] = oi[i]
            qe_ref[rows[i], hs[i]] = qe[i]
            dec_ref[c, :, hs[i]] = e_cum[i][chunk - 1:chunk, :]

    def finish(chains):
        n = len(chains)
        hs = [slice(h * HEAD_DIM, (h + 1) * HEAD_DIM) for _, h in chains]
        rows = [pl.ds(pl.multiple_of(c * chunk, chunk), chunk) for c, _ in chains]
        inter = [lax.dot_general(qe_ref[rows[i], hs[i]], st_ref[c, h], nt, preferred_element_type=F32)
                 for i, (c, h) in enumerate(chains)]
        o = [oi_ref[rows[i], hs[i]] + inter[i] for i in range(n)]
        o = [_rms(o[i], gn_ref[...]) * gs_ref[rows[i], hs[i]].astype(F32) for i in range(n)]
        for i in range(n):
            o_ref[rows[i], hs[i]] = o[i].astype(BF16)

    def run(robust):
        group = min(HGRN_CHUNK_GROUP, n_chunks)

        def intra_body(g, carry):
            intra([(g * group + u, h) for u in range(group) for h in range(hp)], robust)
            return carry
        lax.fori_loop(0, n_chunks // group, intra_body, 0)

        def scan_body(c, sts):
            new = []
            for h in range(hp):
                st_ref[c, h] = sts[h].astype(BF16)
                new.append(sts[h] * dec_ref[c, :, h * HEAD_DIM:(h + 1) * HEAD_DIM] + kv_ref[c, h])
            return tuple(new)
        sts = lax.fori_loop(0, n_chunks, scan_body, tuple(jnp.zeros((HEAD_DIM, HEAD_DIM), F32) for _ in range(hp)))
        for h in range(hp):
            s_ref[0, h] = sts[h].T

        def finish_body(g, carry):
            finish([(g * group + u, h) for u in range(group) for h in range(hp)])
            return carry
        lax.fori_loop(0, n_chunks // group, finish_body, 0)

    lf_all = lf_ref[...].reshape(n_chunks, chunk, width)
    slowest = jnp.min(jnp.sum(lf_all, axis=1))
    fast = slowest >= FAST_DECAY_LIMIT
    pl.when(fast)(lambda: run(False))
    pl.when(jnp.logical_not(fast))(lambda: run(True))


def _hgrn_prompt(q, k, lf, v, gs, g_norm, n_batch, seq, heads):
    chunk = HGRN_CHUNK if seq % HGRN_CHUNK == 0 else seq
    dm, rm, lm, bm = _chunk_constants(chunk)
    hp = min(HGRN_HEADS_PER_STEP, heads)
    assert heads % hp == 0
    n_chunks = seq // chunk
    tok = pl.BlockSpec((seq, hp * HEAD_DIM), lambda b, h: (b, h))
    c2 = lambda b, h: (0, 0)
    c3 = lambda b, h: (0, 0, 0)
    return pl.pallas_call(
        functools.partial(_hgrn_prompt_kernel, chunk=chunk),
        grid=(n_batch, heads // hp),
        in_specs=[tok, tok, tok, tok, tok,
                  pl.BlockSpec((1, HEAD_DIM), c2),
                  pl.BlockSpec(dm.shape, c2),
                  pl.BlockSpec(rm.shape, c3), pl.BlockSpec(lm.shape, c3), pl.BlockSpec(bm.shape, c3)],
        out_specs=[tok, pl.BlockSpec((1, hp, HEAD_DIM, HEAD_DIM), lambda b, h: (b, h, 0, 0))],
        out_shape=[jax.ShapeDtypeStruct((n_batch * seq, heads * HEAD_DIM), BF16),
                   jax.ShapeDtypeStruct((n_batch, heads, HEAD_DIM, HEAD_DIM), F32)],
        scratch_shapes=[pltpu.VMEM((seq, hp * HEAD_DIM), F32),
                        pltpu.VMEM((seq, hp * HEAD_DIM), BF16),
                        pltpu.VMEM((n_chunks, 1, hp * HEAD_DIM), F32),
                        pltpu.VMEM((n_chunks, hp, HEAD_DIM, HEAD_DIM), F32),
                        pltpu.VMEM((n_chunks, hp, HEAD_DIM, HEAD_DIM), BF16)],
        compiler_params=_params(("arbitrary", "arbitrary")),
        name="hgrn_prompt",
    )(q, k, lf, v, gs, g_norm, dm, rm, lm, bm)


def _hgrn_sample_kernel(q_ref, k_ref, lf_ref, v_ref, gs_ref, gn_ref, s0_ref, o_ref, s_ref, inter_ref, *, steps):
    rows, width = q_ref.shape
    heads = width // HEAD_DIM
    q = q_ref[...].astype(F32)
    k = k_ref[...].astype(F32)
    v = v_ref[...].astype(F32)
    lf = lf_ref[...]
    step = lax.broadcasted_iota(jnp.int32, (rows, 1), 0) & (steps - 1)

    def back(x, d):
        return pltpu.roll(x, d, 0)

    cum = lf
    for d in range(1, steps):
        cum = cum + jnp.where(step >= d, back(lf, d), 0.0)
    tail = jnp.zeros_like(lf)
    for d in range(1, steps):
        tail = tail + jnp.where(step + d < steps, pltpu.roll(lf, rows - d, 0), 0.0)

    def head_sum(x):
        return [jnp.sum(x[:, h * HEAD_DIM:(h + 1) * HEAD_DIM], axis=-1, keepdims=True) for h in range(heads)]

    def head_scale(cols, x):
        return jnp.concatenate([cols[h] * x[:, h * HEAD_DIM:(h + 1) * HEAD_DIM] for h in range(heads)], axis=1)

    intra = head_scale(head_sum(q * k), v)
    for d in range(1, steps):
        ok = step >= d
        rel = jnp.where(ok, cum - back(cum, d), 0.0)
        w = jnp.where(ok, q * back(k, d) * jnp.exp(rel), 0.0)
        intra = intra + head_scale(head_sum(w), back(v, d))

    qe = (q * jnp.exp(cum)).astype(BF16)
    kd = k * jnp.exp(tail)
    total = jnp.exp(cum)
    per = SUBLANES // steps
    grp = lax.broadcasted_iota(jnp.int32, (SUBLANES, 1), 0)
    tn = (((0,), (0,)), ((), ()))
    for b in range(rows // steps):
        r8 = (b // per) * SUBLANES
        lo = (b % per) * steps
        mine = (grp >= lo) & (grp < lo + steps)
        spare = (lo + steps) % SUBLANES
        for h in range(heads):
            hs = slice(h * HEAD_DIM, (h + 1) * HEAD_DIM)
            s0 = s0_ref[b, h]
            res = jnp.dot(qe[r8:r8 + SUBLANES, hs], s0.astype(BF16), preferred_element_type=F32)
            inter_ref[b * steps:(b + 1) * steps, hs] = res[lo:lo + steps]
            d1, d2, d3 = _split3(total[r8 + lo + steps - 1:r8 + lo + steps, hs])
            dec = jnp.where(grp == spare, d1.astype(F32),
                            jnp.where(grp == spare + 1, d2.astype(F32),
                                      jnp.where(grp == spare + 2, d3.astype(F32), 0.0)))
            lhs = jnp.where(mine, kd[r8:r8 + SUBLANES, hs], dec).astype(BF16)
            vb = jnp.where(mine, v[r8:r8 + SUBLANES, hs], 0.0)
            ones = jnp.where(mine, 0.0, 1.0) * jnp.ones((SUBLANES, HEAD_DIM), F32)
            rhs = jnp.concatenate([vb, ones], axis=1).astype(BF16)
            upd = lax.dot_general(lhs, rhs, tn, preferred_element_type=F32)
            s_ref[b, h] = upd[:, HEAD_DIM:] * s0 + upd[:, :HEAD_DIM]
    o = intra + inter_ref[...]
    gn = gn_ref[...]
    o = jnp.concatenate([_rms(o[:, h * HEAD_DIM:(h + 1) * HEAD_DIM], gn) for h in range(heads)], axis=1)
    o_ref[...] = (o * gs_ref[...].astype(F32)).astype(BF16)


def _hgrn_sample(q, k, lf, v, gs, g_norm, s0, row0, steps):
    n_batch, heads = s0.shape[:2]
    width = heads * HEAD_DIM
    assert steps & (steps - 1) == 0 and SUBLANES - steps >= 3
    bb = min(SAMPLE_BATCH_BLOCK, n_batch)
    rows = bb * steps
    assert n_batch % bb == 0 and rows % SUBLANES == 0 and row0 % rows == 0
    blk0 = row0 // rows
    tok = pl.BlockSpec((rows, width), lambda i: (blk0 + i, 0))
    st = pl.BlockSpec((bb, heads, HEAD_DIM, HEAD_DIM), lambda i: (i, 0, 0, 0))
    return pl.pallas_call(
        functools.partial(_hgrn_sample_kernel, steps=steps),
        grid=(n_batch // bb,),
        in_specs=[tok, tok, tok, tok, tok, pl.BlockSpec((1, HEAD_DIM), lambda i: (0, 0)), st],
        out_specs=[pl.BlockSpec((rows, width), lambda i: (i, 0)), st],
        out_shape=[jax.ShapeDtypeStruct((n_batch * steps, width), BF16),
                   jax.ShapeDtypeStruct(s0.shape, F32)],
        scratch_shapes=[pltpu.VMEM((rows, width), F32)],
        compiler_params=_params(("arbitrary",)),
        name="hgrn_sample",
    )(q, k, lf, v, gs, g_norm, s0)


def _outproj_kernel(cp_ref, cs_ref, op_ref, os_ref, xp_ref, xs_ref, wc_ref, wo_ref, g_ref, wr_ref, br_ref, tri_ref,
                    x1_ref, h2_ref, ri_ref, rw_ref, cnt_ref, run_ref, *, n_prompt_tiles, n_experts):
    i = pl.program_id(0)
    is_p = i < n_prompt_tiles

    @pl.when(i == 0)
    def _():
        run_ref[...] = jnp.zeros_like(run_ref)

    def mix(c_ref, o_ref, x_ref):
        y = jnp.dot(c_ref[...], wc_ref[...], preferred_element_type=F32)
        y = y + jnp.dot(o_ref[...], wo_ref[...], preferred_element_type=F32)
        x1_ref[...] = x_ref[...] + y

    pl.when(is_p)(lambda: mix(cp_ref, op_ref, xp_ref))
    pl.when(jnp.logical_not(is_p))(lambda: mix(cs_ref, os_ref, xs_ref))

    h2 = _rms(x1_ref[...], g_ref[...])
    h2_ref[...] = h2
    a1, a2, _ = _split3(h2)
    p1 = jnp.dot(a1, wr_ref[...], preferred_element_type=F32)
    p2 = jnp.dot(a2, wr_ref[...], preferred_element_type=F32)
    logits = p1 + pltpu.roll(p1, LANES // 2, 1) + p2 + br_ref[...]
    tm = logits.shape[0]
    lane = lax.broadcasted_iota(jnp.int32, (tm, LANES), 1)
    lane_f = lane.astype(F32)
    neg = jnp.float32(-jnp.inf)

    def top(x):
        m = jnp.max(x, axis=-1, keepdims=True)
        idx = jnp.min(jnp.where(x == m, lane_f, float(LANES)), axis=-1, keepdims=True)
        return m, idx.astype(jnp.int32)

    is_group = (lane >= n_experts) & (lane < n_experts + N_GROUPS)
    gl = jnp.where(is_group, logits, neg)
    gmax, gidx = top(gl)
    p_top = 1.0 / jnp.sum(jnp.exp(gl - gmax), axis=-1, keepdims=True)
    g_lo = (gidx - n_experts) * EXPERTS_PER_GROUP
    el = jnp.where((lane >= g_lo) & (lane < g_lo + EXPERTS_PER_GROUP), logits, neg)
    v1, e1 = top(el)
    v2, e2 = top(jnp.where(lane == e1, neg, el))
    t = jnp.exp(v2 - v1)
    w1 = p_top / (1.0 + t)
    w2 = p_top * t / (1.0 + t)
    hot = ((lane == e1) | (lane == e2)).astype(F32)
    before = run_ref[...] + jnp.dot(tri_ref[...], hot.astype(BF16), preferred_element_type=F32)
    r1 = jnp.sum(jnp.where(lane == e1, before, 0.0), axis=-1, keepdims=True)
    r2 = jnp.sum(jnp.where(lane == e2, before, 0.0), axis=-1, keepdims=True)
    run_ref[...] = run_ref[...] + jnp.sum(hot, axis=0, keepdims=True)
    cnt_ref[...] = run_ref[...]
    info = jnp.where(lane == 0, e1.astype(F32), jnp.where(lane == 1, e2.astype(F32),
                     jnp.where(lane == 2, r1, jnp.where(lane == 3, r2, 0.0))))
    ri_ref[...] = info.T[0:SUBLANES, :].astype(jnp.int32)
    rw_ref[...] = jnp.where(lane == 0, w1, jnp.where(lane == 1, w2, 0.0))


def _outproj(c_p, c_s, o_p, o_s, xp, xs, w_out_bf16, norm_g, w_router3, b_router, n_experts):
    n_p, d = xp.shape
    n_s = xs.shape[0]
    tm = min(TOKEN_TILE, n_s)
    npt, nst = n_p // tm, n_s // tm
    n = n_p + n_s
    ch = c_p.shape[1]
    hv = o_p.shape[1]
    wc, wo = w_out_bf16[:ch], w_out_bf16[ch:]
    tri = jnp.asarray(np.tril(np.ones((tm, tm), np.float32), -1), BF16)
    pidx = lambda i: (jnp.minimum(i, npt - 1), 0)
    sidx = lambda i: (jnp.maximum(i - npt, 0), 0)
    row = lambda i: (i, 0)
    c2 = lambda i: (0, 0)
    return pl.pallas_call(
        functools.partial(_outproj_kernel, n_prompt_tiles=npt, n_experts=n_experts),
        grid=(npt + nst,),
        in_specs=[
            pl.BlockSpec((tm, ch), pidx), pl.BlockSpec((tm, ch), sidx),
            pl.BlockSpec((tm, hv), pidx), pl.BlockSpec((tm, hv), sidx),
            pl.BlockSpec((tm, d), pidx), pl.BlockSpec((tm, d), sidx),
            pl.BlockSpec((ch, d), c2), pl.BlockSpec((hv, d), c2),
            pl.BlockSpec((1, d), c2),
            pl.BlockSpec((d, LANES), c2),
            pl.BlockSpec((1, LANES), c2),
            pl.BlockSpec((tm, tm), c2),
        ],
        out_specs=[pl.BlockSpec((tm, d), row), pl.BlockSpec((tm, d), row),
                   pl.BlockSpec((SUBLANES, tm), lambda i: (0, i)), pl.BlockSpec((tm, LANES), row),
                   pl.BlockSpec((1, LANES), c2)],
        out_shape=[jax.ShapeDtypeStruct((n, d), F32), jax.ShapeDtypeStruct((n, d), F32),
                   jax.ShapeDtypeStruct((SUBLANES, n), jnp.int32), jax.ShapeDtypeStruct((n, LANES), F32),
                   jax.ShapeDtypeStruct((1, LANES), F32)],
        scratch_shapes=[pltpu.VMEM((1, LANES), F32)],
        compiler_params=_params(("arbitrary",)),
        name="outproj",
    )(c_p, c_s, o_p, o_s, xp, xs, wc, wo, norm_g, w_router3, b_router, tri)


def _experts_kernel(src_ref, te_ref, tf_ref, tn_ref, ts_ref, nu_ref, h2_ref, wg_ref, wu_ref, wd_ref, y_ref,
                    x_ref, wgf_ref, wuf_ref, wdf_ref, wgb_ref, wub_ref, wdb_ref, sem, wsem):
    i = pl.program_id(0)
    n_used = nu_ref[0]
    rows = x_ref.shape[1]

    def row_copy(tile, slot, r):
        return pltpu.make_async_copy(h2_ref.at[pl.ds(src_ref[tile * rows + r], 1)],
                                     x_ref.at[slot, pl.ds(r, 1)], sem.at[slot])

    def gather(tile, slot):
        def start(g, carry):
            for u in range(GATHER_UNROLL):
                row_copy(tile, slot, g * GATHER_UNROLL + u).start()
            return carry
        lax.fori_loop(0, rows // GATHER_UNROLL, start, 0)

    def weight_copies(e, slot):
        return [pltpu.make_async_copy(src.at[e], dst.at[slot], wsem.at[slot])
                for src, dst in ((wg_ref, wgf_ref), (wu_ref, wuf_ref), (wd_ref, wdf_ref))]

    @pl.when(i == 0)
    def _():
        for c in weight_copies(te_ref[0], 0):
            c.start(priority=1)
        gather(0, 0)

    @pl.when(tf_ref[i] == 1)
    def _():
        wslot = ts_ref[i]

        @pl.when(tn_ref[i] >= 0)
        def _():
            for c in weight_copies(tn_ref[i], 1 - wslot):
                c.start(priority=1)
        for c in weight_copies(te_ref[i], wslot):
            c.wait()
        wgb_ref[...] = wgf_ref[wslot].astype(BF16)
        wub_ref[...] = wuf_ref[wslot].astype(BF16)
        wdb_ref[...] = wdf_ref[wslot].astype(BF16)

    def tile_step(prefetch):
        slot = i % 2

        def wait(g, carry):
            for u in range(GATHER_UNROLL):
                row_copy(i, slot, g * GATHER_UNROLL + u).wait()
            return carry
        lax.fori_loop(0, rows // GATHER_UNROLL, wait, 0)

        f = wgb_ref.shape[1]
        d = wdb_ref.shape[1]
        fc, dc = min(MXU_COLS, f), min(MXU_COLS, d)
        n_pieces = 2 * (f // fc) + d // dc
        per_piece = -(-rows // n_pieces)
        issued = [0]

        def issue_share():
            if not prefetch:
                return
            for r in range(issued[0], min(issued[0] + per_piece, rows)):
                row_copy(i + 1, 1 - slot, r).start()
            issued[0] = min(issued[0] + per_piece, rows)

        x = x_ref[slot].astype(BF16)
        hid = []
        for j in range(f // fc):
            cs = slice(j * fc, (j + 1) * fc)
            hg = jnp.dot(x, wgb_ref[:, cs], preferred_element_type=F32)
            issue_share()
            hu = jnp.dot(x, wub_ref[:, cs], preferred_element_type=F32)
            issue_share()
            hid.append((_silu(hg) * hu).astype(BF16))
        hid = jnp.concatenate(hid, axis=1)
        for j in range(d // dc):
            cs = slice(j * dc, (j + 1) * dc)
            y_ref[:, cs] = jnp.dot(hid, wdb_ref[:, cs], preferred_element_type=F32)
            issue_share()

    pl.when(i + 1 < n_used)(lambda: tile_step(True))
    pl.when(i + 1 == n_used)(lambda: tile_step(False))

    @pl.when(i >= n_used)
    def _():
        y_ref[...] = jnp.zeros_like(y_ref)


def _experts(src_token, tile_expert, tile_first, tile_next, tile_slot, n_used, h2, w_gate, w_up, w_down):
    n_rows = src_token.shape[0]
    d = h2.shape[1]
    f = w_gate.shape[2]
    n_tiles = n_rows // EXPERT_TILE
    any_space = pl.BlockSpec(memory_space=pl.ANY)
    return pl.pallas_call(
        _experts_kernel,
        grid_spec=pltpu.PrefetchScalarGridSpec(
            num_scalar_prefetch=6, grid=(n_tiles,),
            in_specs=[any_space, any_space, any_space, any_space],
            out_specs=pl.BlockSpec((EXPERT_TILE, d), lambda i, *_: (i, 0)),
            scratch_shapes=[pltpu.VMEM((2, EXPERT_TILE, d), F32),
                            pltpu.VMEM((2, d, f), F32), pltpu.VMEM((2, d, f), F32), pltpu.VMEM((2, f, d), F32),
                            pltpu.VMEM((d, f), BF16), pltpu.VMEM((d, f), BF16), pltpu.VMEM((f, d), BF16),
                            pltpu.SemaphoreType.DMA((2,)), pltpu.SemaphoreType.DMA((2,))],
        ),
        out_shape=jax.ShapeDtypeStruct((n_rows, d), F32),
        compiler_params=_params(("arbitrary",)),
        name="experts",
    )(src_token, tile_expert, tile_first, tile_next, tile_slot, n_used, h2, w_gate, w_up, w_down)


def _combine_kernel(dest_ref, ys_ref, x1_ref, rw_ref, g_ref, yp_ref, ysm_ref, buf_ref, sem, *, n_prompt_tiles):
    i = pl.program_id(0)
    tm = x1_ref.shape[0]

    n_tokens = dest_ref.shape[0] // 2
    n_steps = pl.num_programs(0)

    def copy(tile, r, slot):
        return pltpu.make_async_copy(ys_ref.at[pl.ds(dest_ref[slot * n_tokens + tile * tm + r], 1)],
                                     buf_ref.at[tile % 2, slot, pl.ds(r, 1)], sem.at[tile % 2])

    def gather(tile):
        def start(r, carry):
            copy(tile, r, 0).start(priority=0)
            copy(tile, r, 1).start(priority=1)
            return carry
        lax.fori_loop(0, tm, start, 0, unroll=GATHER_UNROLL)

    @pl.when(i == 0)
    def _():
        gather(0)

    @pl.when(i + 1 < n_steps)
    def _():
        gather(i + 1)

    def wait(r, carry):
        copy(i, r, 0).wait()
        copy(i, r, 1).wait()
        return carry

    lax.fori_loop(0, tm, wait, 0, unroll=GATHER_UNROLL)
    rw = rw_ref[...]
    par = i % 2
    x2 = x1_ref[...] + rw[:, 0:1] * buf_ref[par, 0] + rw[:, 1:2] * buf_ref[par, 1]
    y = _rms(x2, g_ref[...])

    @pl.when(i < n_prompt_tiles)
    def _():
        yp_ref[...] = y

    @pl.when(i >= n_prompt_tiles)
    def _():
        ysm_ref[...] = y


def _combine(dest_flat, ys, x1, rw, norm_g, n_p, n_s):
    n, d = x1.shape
    tm = min(CONV_TILE, n_s)
    npt, nst = n_p // tm, n_s // tm
    row = lambda i, dest: (i, 0)
    return pl.pallas_call(
        functools.partial(_combine_kernel, n_prompt_tiles=npt),
        grid_spec=pltpu.PrefetchScalarGridSpec(
            num_scalar_prefetch=1, grid=(npt + nst,),
            in_specs=[pl.BlockSpec(memory_space=pl.ANY),
                      pl.BlockSpec((tm, d), row), pl.BlockSpec((tm, LANES), row),
                      pl.BlockSpec((1, d), lambda i, dest: (0, 0))],
            out_specs=[pl.BlockSpec((tm, d), lambda i, dest: (jnp.minimum(i, npt - 1), 0)),
                       pl.BlockSpec((tm, d), lambda i, dest: (jnp.maximum(i - npt, 0), 0))],
            scratch_shapes=[pltpu.VMEM((2, 2, tm, d), F32), pltpu.SemaphoreType.DMA((2,))],
        ),
        out_shape=[jax.ShapeDtypeStruct((n_p, d), F32), jax.ShapeDtypeStruct((n_s, d), F32)],
        compiler_params=_params(("arbitrary",)),
        name="combine",
    )(dest_flat, ys, x1, rw, norm_g)


def kernel(x_prompt, x_sample, state_conv, state_hgrn, norm_mix, w_in, w_dw, b_dw, ln_conv_g, ln_conv_b, lb_logits, hgrn_norm_g, w_out, norm_ffn, w_router_group, b_router_group, w_router_expert, b_router_expert, w_exp_gate, w_exp_up, w_exp_down, norm_final):
    assert w_in.shape[0] == 1, "single-layer trunk"
    n_batch, seq, d = x_prompt.shape
    s_batch, steps, _ = x_sample.shape
    ch = w_dw.shape[-1]
    hk = lb_logits.shape[-1]
    heads = hk // HEAD_DIM
    n_experts = w_exp_gate.shape[1]
    assert n_experts == N_GROUPS * EXPERTS_PER_GROUP and n_experts + N_GROUPS <= LANES // 2
    n_p, n_s = n_batch * seq, s_batch * steps
    n = n_p + n_s
    xp = x_prompt.reshape(n_p, d)
    xs = x_sample.reshape(n_s, d)

    u, q, k, lf, v, gs = _inproj(xp, xs, norm_mix, w_in[0].astype(BF16), lb_logits, ch, hk)

    c_p = _conv_prompt(u, n_batch, seq, w_dw[0], b_dw, ln_conv_g, ln_conv_b)
    u_t = u[n_p:].reshape(s_batch, steps, ch).transpose(1, 0, 2)
    c_t, new_conv_t = _conv_sample(u_t, state_conv[0].transpose(1, 0, 2), w_dw[0], b_dw, ln_conv_g, ln_conv_b)
    c_s = c_t.transpose(1, 0, 2).reshape(n_s, ch)
    new_conv_sample = new_conv_t.transpose(1, 0, 2)[None]
    hist = state_conv.shape[2]
    new_conv_prompt = jnp.stack([u[(b + 1) * seq - hist:(b + 1) * seq] for b in range(n_batch)])

    o_p, hgrn_p = _hgrn_prompt(q, k, lf, v, gs, hgrn_norm_g, n_batch, seq, heads)
    o_s, hgrn_s = _hgrn_sample(q, k, lf, v, gs, hgrn_norm_g, state_hgrn[0], n_p, steps)

    w_r = jnp.concatenate([w_router_expert[0], w_router_group[0]], axis=1)
    w_r = jnp.pad(w_r, ((0, 0), (0, LANES // 2 - w_r.shape[1])))
    r1 = w_r.astype(BF16)
    r2 = (w_r - r1.astype(F32)).astype(BF16)
    b_r = jnp.pad(jnp.concatenate([b_router_expert[0], b_router_group[0]]), (0, LANES - n_experts - N_GROUPS))[None]
    x1, h2, ri, rw, counts = _outproj(c_p, c_s, o_p, o_s, xp, xs, w_out[0].astype(BF16), norm_ffn,
                                      jnp.concatenate([r1, r2], axis=1), b_r, n_experts)

    cnt = counts[0, :n_experts].astype(jnp.int32)
    tiles_per = (cnt + EXPERT_TILE - 1) // EXPERT_TILE
    tile_end = jnp.cumsum(tiles_per)
    row_start = (tile_end - tiles_per) * EXPERT_TILE
    n_tiles = (2 * n) // EXPERT_TILE + n_experts
    is_e = ri[0:2, :, None] == jnp.arange(n_experts, dtype=jnp.int32)
    dest = jnp.sum(jnp.where(is_e, row_start, 0), axis=-1) + ri[2:4]
    dest_flat = dest.reshape(-1)
    n_used = tile_end[-1:]
    tid = jnp.minimum(jnp.arange(n_tiles, dtype=jnp.int32), n_used - 1)
    tile_expert = jnp.sum((tile_end[None, :] <= tid[:, None]).astype(jnp.int32), axis=1)
    prev = jnp.concatenate([jnp.full((1,), -1, jnp.int32), tile_expert[:-1]])
    tile_first = ((tile_expert != prev) & (jnp.arange(n_tiles) < n_used)).astype(jnp.int32)
    eid = jnp.arange(n_experts, dtype=jnp.int32)
    used = tiles_per > 0
    later = used[None, :] & (eid[None, :] > eid[:, None])
    next_expert = jnp.min(jnp.where(later, eid[None, :], n_experts), axis=1)
    next_expert = jnp.where(next_expert == n_experts, -1, next_expert)
    parity = (jnp.cumsum(used.astype(jnp.int32)) - 1) % 2
    tile_is = tile_expert[:, None] == eid[None, :]
    tile_next = jnp.sum(jnp.where(tile_is, next_expert[None, :], 0), axis=1).astype(jnp.int32)
    tile_slot = jnp.sum(jnp.where(tile_is, parity[None, :], 0), axis=1).astype(jnp.int32)
    token = jnp.tile(jnp.arange(n, dtype=jnp.int32), 2)
    filler = jnp.arange(n_tiles * EXPERT_TILE, dtype=jnp.int32) % n
    src_token = filler.at[dest_flat].set(token, unique_indices=True)

    ys_sorted = _experts(src_token, tile_expert, tile_first, tile_next, tile_slot, n_used.astype(jnp.int32), h2,
                         w_exp_gate[0], w_exp_up[0], w_exp_down[0])
    y_p, y_s = _combine(dest_flat, ys_sorted, x1, rw, norm_final[None], n_p, n_s)

    return (y_p.reshape(n_batch, seq, d), y_s.reshape(s_batch, steps, d),
            new_conv_prompt[None], hgrn_p[None], new_conv_sample, hgrn_s[None])
```

```python
import functools

import numpy as np
import jax
import jax.numpy as jnp
from jax import lax
from jax.experimental import pallas as pl
from jax.experimental.pallas import tpu as pltpu

F32 = jnp.float32
BF16 = jnp.bfloat16
EPS = 1e-6
LANES = 128
SUBLANES = 8
HEAD_DIM = 128
HGRN_CHUNK = 64
HGRN_HEADS_PER_STEP = 2
HGRN_CHUNK_GROUP = 8
FAST_DECAY_LIMIT = -60.0
N_GROUPS = 4
EXPERTS_PER_GROUP = 8
VMEM_LIMIT = 56 * 1024 * 1024
TOKEN_TILE = 256
INPROJ_TILE = 256
WEIGHT_STAGE_COLS = 512
CONV_TILE = 256
CONV_HALO = 32
CONV_ROWS = 64
EXPERT_TILE = 256
SAMPLE_BATCH_BLOCK = 8
SAMPLE_CONV_BATCH_BLOCK = 16
GATHER_UNROLL = 8
MXU_COLS = 256


def _sigmoid(x):
    return 1.0 / (1.0 + jnp.exp(-x))


def _silu(x):
    return x * _sigmoid(x)


def _rms(x, g):
    return x * lax.rsqrt(jnp.mean(x * x, axis=-1, keepdims=True) + EPS) * g


def _split3(x):
    h1 = x.astype(BF16)
    r1 = x - h1.astype(F32)
    h2 = r1.astype(BF16)
    h3 = (r1 - h2.astype(F32)).astype(BF16)
    return h1, h2, h3


def _params(sem, flags=None):
    return pltpu.CompilerParams(dimension_semantics=sem, vmem_limit_bytes=VMEM_LIMIT, flags=flags)


def _load_weight_as_bf16(w_hbm, w_ref, stage_ref, wsem):
    cw = stage_ref.shape[2]
    n_cols = w_ref.shape[1] // cw
    copies = [pltpu.make_async_copy(w_hbm.at[:, pl.ds(c * cw, cw)], stage_ref.at[c % 2], wsem.at[c % 2])
              for c in range(n_cols)]
    copies[0].start()
    for c in range(n_cols):
        if c + 1 < n_cols:
            copies[c + 1].start()
        copies[c].wait()
        w_ref[:, c * cw:(c + 1) * cw] = stage_ref[c % 2].astype(BF16)


def _inproj_kernel(xp_ref, xs_ref, g_ref, w_hbm, lbl_ref,
                   u_ref, q_ref, k_ref, lf_ref, v_ref, gs_ref, h_ref, w_ref, stage_ref, wsem,
                   *, n_prompt_tiles, ch, hk):
    s = pl.program_id(0)
    n_tiles = pl.num_programs(0) - 1

    def normalise():
        x = jnp.where(s < n_prompt_tiles, xp_ref[...], xs_ref[...])
        h_ref[s % 2] = _rms(x, g_ref[...]).astype(BF16)

    def project():
        h = h_ref[(s + 1) % 2]

        def proj(c0, width):
            return jnp.dot(h, w_ref[:, c0:c0 + width], preferred_element_type=F32)

        a = proj(0, ch)
        ga = proj(ch, ch)
        u_ref[...] = a * _sigmoid(ga)
        q = proj(2 * ch, hk)
        q_ref[...] = _silu(q).astype(BF16)
        f = proj(2 * ch + hk, hk)
        lbl = lbl_ref[...]
        e = jnp.exp(lbl - jnp.max(lbl, axis=0, keepdims=True))
        lb = e[0:1, :] / jnp.sum(e, axis=0, keepdims=True)
        fg = lb + (1.0 - lb) * _sigmoid(f)
        k_ref[...] = (1.0 - fg).astype(BF16)
        lf_ref[...] = jnp.log(fg)
        g = proj(2 * ch + 3 * hk, hk)
        gs_ref[...] = _silu(g).astype(BF16)
        v_ref[...] = proj(2 * ch + 2 * hk, hk).astype(BF16)

    @pl.when(s == 0)
    def _():
        _load_weight_as_bf16(w_hbm, w_ref, stage_ref, wsem)
        normalise()

    @pl.when((s > 0) & (s < n_tiles))
    def _():
        normalise()
        project()

    pl.when(s == n_tiles)(project)


def _inproj(xp, xs, norm_g, w_in, lb_logits, ch, hk):
    n_p, d = xp.shape
    n_s = xs.shape[0]
    tm = min(INPROJ_TILE, n_s)
    assert n_p % tm == 0 and n_s % tm == 0
    npt, nst = n_p // tm, n_s // tm
    n = n_p + n_s
    cols = w_in.shape[1]
    cw = min(WEIGHT_STAGE_COLS, cols)
    assert cols % cw == 0
    row = lambda s: (jnp.maximum(s - 1, 0), 0)
    const = lambda s: (0, 0)
    outs = [jax.ShapeDtypeStruct((n, ch), F32)] + [
        jax.ShapeDtypeStruct((n, hk), dt) for dt in (BF16, BF16, F32, BF16, BF16)]
    return pl.pallas_call(
        functools.partial(_inproj_kernel, n_prompt_tiles=npt, ch=ch, hk=hk),
        grid=(npt + nst + 1,),
        in_specs=[
            pl.BlockSpec((tm, d), lambda s: (jnp.minimum(s, npt - 1), 0)),
            pl.BlockSpec((tm, d), lambda s: (jnp.clip(s - npt, 0, nst - 1), 0)),
            pl.BlockSpec((1, d), const),
            pl.BlockSpec(memory_space=pl.ANY),
            pl.BlockSpec(lb_logits.shape, const),
        ],
        out_specs=[pl.BlockSpec((tm, ch), row)] + [pl.BlockSpec((tm, hk), row)] * 5,
        out_shape=outs,
        scratch_shapes=[pltpu.VMEM((2, tm, d), BF16), pltpu.VMEM((d, cols), BF16),
                        pltpu.VMEM((2, d, cw), F32), pltpu.SemaphoreType.DMA((2,))],
        compiler_params=_params(("arbitrary",)),
        name="inproj",
    )(xp, xs, norm_g, w_in, lb_logits)


def _ln_silu(c, g, b):
    mu = jnp.mean(c, axis=-1, keepdims=True)
    d = c - mu
    var = jnp.mean(d * d, axis=-1, keepdims=True)
    return _silu(d * lax.rsqrt(var + EPS) * g + b)


def _conv_prompt_kernel(halo_ref, cur_ref, w_ref, b_ref, lg_ref, lb_ref, c_ref, ext_ref, acc_ref, *, width):
    t = pl.program_id(1)
    tt, ch = cur_ref.shape
    halo = halo_ref[...]
    ext_ref[0:CONV_HALO, :] = jnp.where(t == 0, jnp.zeros_like(halo), halo)
    ext_ref[CONV_HALO:, :] = cur_ref[...]
    off = CONV_HALO - (width - 1)
    rows = min(CONV_ROWS, tt)
    for l in range(ch // LANES):
        ls = slice(l * LANES, (l + 1) * LANES)
        wl = w_ref[:, ls]
        bl = b_ref[:, ls]
        for r0 in range(0, tt, rows):
            acc = jnp.broadcast_to(bl, (rows, LANES))
            for res in range(SUBLANES):
                extra = SUBLANES if res else 0
                part = None
                for a in range((off + width - 1) // SUBLANES + 1):
                    j = SUBLANES * a + res - off
                    if 0 <= j < width:
                        lo = r0 + SUBLANES * a
                        term = ext_ref[lo:lo + rows + extra, ls] * wl[j:j + 1, :]
                        part = term if part is None else part + term
                acc = acc + part[res:res + rows]
            acc_ref[r0:r0 + rows, ls] = acc
    c_ref[...] = _ln_silu(acc_ref[...], lg_ref[...], lb_ref[...]).astype(BF16)


def _conv_prompt(u, n_batch, seq, w_dw, b_dw, ln_g, ln_b):
    ch = u.shape[1]
    width = w_dw.shape[0]
    tt = min(CONV_TILE, seq)
    assert seq % tt == 0 and tt % CONV_HALO == 0 and width - 1 <= CONV_HALO
    nt = seq // tt
    hb = tt // CONV_HALO
    const = lambda b, t: (0, 0)
    return pl.pallas_call(
        functools.partial(_conv_prompt_kernel, width=width),
        grid=(n_batch, nt),
        in_specs=[
            pl.BlockSpec((CONV_HALO, ch), lambda b, t: (jnp.maximum((b * nt + t) * hb - 1, 0), 0)),
            pl.BlockSpec((tt, ch), lambda b, t: (b * nt + t, 0)),
            pl.BlockSpec((width, ch), const),
            pl.BlockSpec((1, ch), const),
            pl.BlockSpec((1, ch), const),
            pl.BlockSpec((1, ch), const),
        ],
        out_specs=pl.BlockSpec((tt, ch), lambda b, t: (b * nt + t, 0)),
        out_shape=jax.ShapeDtypeStruct((n_batch * seq, ch), BF16),
        scratch_shapes=[pltpu.VMEM((CONV_HALO + tt, ch), F32), pltpu.VMEM((tt, ch), F32)],
        compiler_params=_params(("arbitrary", "arbitrary")),
        name="conv_prompt",
    )(u, u, w_dw, b_dw, ln_g, ln_b)


def _conv_sample_kernel(state_ref, u_ref, w_ref, b_ref, lg_ref, lb_ref, c_ref, new_ref):
    hist = state_ref.shape[0]
    steps = u_ref.shape[0]
    for r in range(hist - steps):
        new_ref[r] = state_ref[r + steps]
    for s in range(steps):
        new_ref[hist - steps + s] = u_ref[s]
    for t in range(steps):
        acc = jnp.broadcast_to(b_ref[...], u_ref.shape[1:])
        for r in range(t, hist):
            acc = acc + state_ref[r] * w_ref[r - t:r - t + 1, :]
        for s in range(t + 1):
            acc = acc + u_ref[s] * w_ref[hist - t + s:hist - t + s + 1, :]
        c_ref[t] = _ln_silu(acc, lg_ref[...], lb_ref[...]).astype(BF16)


def _conv_sample(u_t, state_t, w_dw, b_dw, ln_g, ln_b):
    hist, n_batch, ch = state_t.shape
    width = w_dw.shape[0]
    steps = u_t.shape[0]
    assert hist == width - 1 and steps <= hist
    bb = min(SAMPLE_CONV_BATCH_BLOCK, n_batch)
    assert n_batch % bb == 0
    c2 = lambda i: (0, 0)
    blk = lambda i: (0, i, 0)
    return pl.pallas_call(
        _conv_sample_kernel,
        grid=(n_batch // bb,),
        in_specs=[
            pl.BlockSpec((hist, bb, ch), blk),
            pl.BlockSpec((steps, bb, ch), blk),
            pl.BlockSpec((width, ch), c2),
            pl.BlockSpec((1, ch), c2),
            pl.BlockSpec((1, ch), c2),
            pl.BlockSpec((1, ch), c2),
        ],
        out_specs=[pl.BlockSpec((steps, bb, ch), blk), pl.BlockSpec((hist, bb, ch), blk)],
        out_shape=[jax.ShapeDtypeStruct((steps, n_batch, ch), BF16),
                   jax.ShapeDtypeStruct((hist, n_batch, ch), F32)],
        compiler_params=_params(("arbitrary",)),
        name="conv_sample",
    )(state_t, u_t, w_dw, b_dw, ln_g, ln_b)


def _chunk_constants(c):
    levels = int(np.log2(c))
    assert 2 ** levels == c
    t = np.arange(c)[:, None]
    j = np.arange(c)[None, :]
    mats = [(j <= t)]
    right = np.zeros((levels, c, LANES), np.float32)
    left = np.zeros((levels, c, LANES), np.float32)
    same = np.zeros((levels, c, c), np.float32)
    for lv in range(levels):
        half = c >> (lv + 1)
        split = (t // (2 * half)) * (2 * half) + half
        is_right = t >= split
        mats.append(np.where(is_right, (j >= split) & (j <= t), (j > t) & (j < split)))
        right[lv] = is_right
        left[lv] = ~is_right
        same[lv] = (t // (2 * half)) == (j // (2 * half))
    mats.append(j > t)
    dm = np.concatenate(mats, axis=0).astype(np.float32)
    dm = np.concatenate([dm, dm, dm], axis=1)
    return (jnp.asarray(dm, BF16), jnp.asarray(right), jnp.asarray(left), jnp.asarray(same))


def _hgrn_prompt_kernel(q_ref, k_ref, lf_ref, v_ref, gs_ref, gn_ref, dm_ref, rm_ref, lm_ref, bm_ref,
                        o_ref, s_ref, oi_ref, qe_ref, dec_ref, kv_ref, st_ref, *, chunk):
    seq, width = q_ref.shape
    hp = width // HEAD_DIM
    levels = rm_ref.shape[0]
    n_chunks = seq // chunk
    nt = (((1,), (1,)), ((), ()))
    tn = (((0,), (0,)), ((), ()))
    row_i = lax.broadcasted_iota(jnp.int32, (chunk, chunk), 0)
    col_i = lax.broadcasted_iota(jnp.int32, (chunk, chunk), 1)

    def intra(chains, robust):
        n = len(chains)
        hs = [slice(h * HEAD_DIM, (h + 1) * HEAD_DIM) for _, h in chains]
        rows = [pl.ds(pl.multiple_of(c * chunk, chunk), chunk) for c, _ in chains]
        q = [q_ref[rows[i], hs[i]].astype(F32) for i in range(n)]
        k = [k_ref[rows[i], hs[i]].astype(F32) for i in range(n)]
        v = [v_ref[rows[i], hs[i]] for i in range(n)]
        lsp = [jnp.concatenate(_split3(lf_ref[rows[i], hs[i]]), axis=0) for i in range(n)]

        def decay_sums(i, dm):
            return jnp.dot(dm, lsp[i], preferred_element_type=F32)

        if robust:
            ex = [jnp.exp(decay_sums(i, dm_ref[...])) for i in range(n)]
            e_cum = [e[0:chunk] for e in ex]
            e_tail = [e[(levels + 1) * chunk:(levels + 2) * chunk] for e in ex]
            att = [jnp.where(row_i == col_i, jnp.sum(q[i] * k[i], axis=-1, keepdims=True), 0.0) for i in range(n)]
            for lv in range(levels):
                for i in range(n):
                    e = ex[i][(lv + 1) * chunk:(lv + 2) * chunk]
                    ql = (q[i] * e * rm_ref[lv]).astype(BF16)
                    kl = (k[i] * e * lm_ref[lv]).astype(BF16)
                    att[i] = att[i] + bm_ref[lv] * lax.dot_general(ql, kl, nt, preferred_element_type=F32)
            qe = [(q[i] * e_cum[i]).astype(BF16) for i in range(n)]
        else:
            sums = [decay_sums(i, dm_ref[0:chunk, :]) for i in range(n)]
            tails = [s[chunk - 1:chunk, :] - s for s in sums]
            e_cum = [jnp.exp(s) for s in sums]
            e_tail = [jnp.exp(t) for t in tails]
            qe = [(q[i] * e_cum[i]).astype(BF16) for i in range(n)]
            kn = [(k[i] * jnp.exp(-sums[i])).astype(BF16) for i in range(n)]
            att = [lax.dot_general(qe[i], kn[i], nt, preferred_element_type=F32) for i in range(n)]
            att = [jnp.where(row_i >= col_i, a, 0.0) for a in att]
        kv = [lax.dot_general(v[i], (k[i] * e_tail[i]).astype(BF16), tn, preferred_element_type=F32) for i in range(n)]
        oi = [jnp.dot(att[i].astype(BF16), v[i], preferred_element_type=F32) for i in range(n)]
        for i, (c, h) in enumerate(chains):
            kv_ref[c, h] = kv[i]
            oi_ref[rows[i], hs[i]] = oi[i]
            qe_ref[rows[i], hs[i]] = qe[i]
            dec_ref[c, :, hs[i]] = e_cum[i][chunk - 1:chunk, :]

    def finish(chains):
        n = len(chains)
        hs = [slice(h * HEAD_DIM, (h + 1) * HEAD_DIM) for _, h in chains]
        rows = [pl.ds(pl.multiple_of(c * chunk, chunk), chunk) for c, _ in chains]
        inter = [lax.dot_general(qe_ref[rows[i], hs[i]], st_ref[c, h], nt, preferred_element_type=F32)
                 for i, (c, h) in enumerate(chains)]
        o = [oi_ref[rows[i], hs[i]] + inter[i] for i in range(n)]
        o = [_rms(o[i], gn_ref[...]) * gs_ref[rows[i], hs[i]].astype(F32) for i in range(n)]
        for i in range(n):
            o_ref[rows[i], hs[i]] = o[i].astype(BF16)

    def run(robust):
        group = min(HGRN_CHUNK_GROUP, n_chunks)

        def intra_body(g, carry):
            intra([(g * group + u, h) for u in range(group) for h in range(hp)], robust)
            return carry
        lax.fori_loop(0, n_chunks // group, intra_body, 0)

        def scan_body(c, sts):
            new = []
            for h in range(hp):
                st_ref[c, h] = sts[h].astype(BF16)
                new.append(sts[h] * dec_ref[c, :, h * HEAD_DIM:(h + 1) * HEAD_DIM] + kv_ref[c, h])
            return tuple(new)
        sts = lax.fori_loop(0, n_chunks, scan_body, tuple(jnp.zeros((HEAD_DIM, HEAD_DIM), F32) for _ in range(hp)))
        for h in range(hp):
            s_ref[0, h] = sts[h].T

        def finish_body(g, carry):
            finish([(g * group + u, h) for u in range(group) for h in range(hp)])
            return carry
        lax.fori_loop(0, n_chunks // group, finish_body, 0)

    lf_all = lf_ref[...].reshape(n_chunks, chunk, width)
    slowest = jnp.min(jnp.sum(lf_all, axis=1))
    fast = slowest >= FAST_DECAY_LIMIT
    pl.when(fast)(lambda: run(False))
    pl.when(jnp.logical_not(fast))(lambda: run(True))


def _hgrn_prompt(q, k, lf, v, gs, g_norm, n_batch, seq, heads):
    chunk = HGRN_CHUNK if seq % HGRN_CHUNK == 0 else seq
    dm, rm, lm, bm = _chunk_constants(chunk)
    hp = min(HGRN_HEADS_PER_STEP, heads)
    assert heads % hp == 0
    n_chunks = seq // chunk
    tok = pl.BlockSpec((seq, hp * HEAD_DIM), lambda b, h: (b, h))
    c2 = lambda b, h: (0, 0)
    c3 = lambda b, h: (0, 0, 0)
    return pl.pallas_call(
        functools.partial(_hgrn_prompt_kernel, chunk=chunk),
        grid=(n_batch, heads // hp),
        in_specs=[tok, tok, tok, tok, tok,
                  pl.BlockSpec((1, HEAD_DIM), c2),
                  pl.BlockSpec(dm.shape, c2),
                  pl.BlockSpec(rm.shape, c3), pl.BlockSpec(lm.shape, c3), pl.BlockSpec(bm.shape, c3)],
        out_specs=[tok, pl.BlockSpec((1, hp, HEAD_DIM, HEAD_DIM), lambda b, h: (b, h, 0, 0))],
        out_shape=[jax.ShapeDtypeStruct((n_batch * seq, heads * HEAD_DIM), BF16),
                   jax.ShapeDtypeStruct((n_batch, heads, HEAD_DIM, HEAD_DIM), F32)],
        scratch_shapes=[pltpu.VMEM((seq, hp * HEAD_DIM), F32),
                        pltpu.VMEM((seq, hp * HEAD_DIM), BF16),
                        pltpu.VMEM((n_chunks, 1, hp * HEAD_DIM), F32),
                        pltpu.VMEM((n_chunks, hp, HEAD_DIM, HEAD_DIM), F32),
                        pltpu.VMEM((n_chunks, hp, HEAD_DIM, HEAD_DIM), BF16)],
        compiler_params=_params(("arbitrary", "arbitrary")),
        name="hgrn_prompt",
    )(q, k, lf, v, gs, g_norm, dm, rm, lm, bm)


def _hgrn_sample_kernel(q_ref, k_ref, lf_ref, v_ref, gs_ref, gn_ref, s0_ref, o_ref, s_ref, inter_ref, *, steps):
    rows, width = q_ref.shape
    heads = width // HEAD_DIM
    q = q_ref[...].astype(F32)
    k = k_ref[...].astype(F32)
    v = v_ref[...].astype(F32)
    lf = lf_ref[...]
    step = lax.broadcasted_iota(jnp.int32, (rows, 1), 0) & (steps - 1)

    def back(x, d):
        return pltpu.roll(x, d, 0)

    cum = lf
    for d in range(1, steps):
        cum = cum + jnp.where(step >= d, back(lf, d), 0.0)
    tail = jnp.zeros_like(lf)
    for d in range(1, steps):
        tail = tail + jnp.where(step + d < steps, pltpu.roll(lf, rows - d, 0), 0.0)

    def head_sum(x):
        return [jnp.sum(x[:, h * HEAD_DIM:(h + 1) * HEAD_DIM], axis=-1, keepdims=True) for h in range(heads)]

    def head_scale(cols, x):
        return jnp.concatenate([cols[h] * x[:, h * HEAD_DIM:(h + 1) * HEAD_DIM] for h in range(heads)], axis=1)

    intra = head_scale(head_sum(q * k), v)
    for d in range(1, steps):
        ok = step >= d
        rel = jnp.where(ok, cum - back(cum, d), 0.0)
        w = jnp.where(ok, q * back(k, d) * jnp.exp(rel), 0.0)
        intra = intra + head_scale(head_sum(w), back(v, d))

    qe = (q * jnp.exp(cum)).astype(BF16)
    kd = k * jnp.exp(tail)
    total = jnp.exp(cum)
    per = SUBLANES // steps
    grp = lax.broadcasted_iota(jnp.int32, (SUBLANES, 1), 0)
    tn = (((0,), (0,)), ((), ()))
    for b in range(rows // steps):
        r8 = (b // per) * SUBLANES
        lo = (b % per) * steps
        mine = (grp >= lo) & (grp < lo + steps)
        spare = (lo + steps) % SUBLANES
        for h in range(heads):
            hs = slice(h * HEAD_DIM, (h + 1) * HEAD_DIM)
            s0 = s0_ref[b, h]
            res = jnp.dot(qe[r8:r8 + SUBLANES, hs], s0.astype(BF16), preferred_element_type=F32)
            inter_ref[b * steps:(b + 1) * steps, hs] = res[lo:lo + steps]
            d1, d2, d3 = _split3(total[r8 + lo + steps - 1:r8 + lo + steps, hs])
            dec = jnp.where(grp == spare, d1.astype(F32),
                            jnp.where(grp == spare + 1, d2.astype(F32),
                                      jnp.where(grp == spare + 2, d3.astype(F32), 0.0)))
            lhs = jnp.where(mine, kd[r8:r8 + SUBLANES, hs], dec).astype(BF16)
            vb = jnp.where(mine, v[r8:r8 + SUBLANES, hs], 0.0)
            ones = jnp.where(mine, 0.0, 1.0) * jnp.ones((SUBLANES, HEAD_DIM), F32)
            rhs = jnp.concatenate([vb, ones], axis=1).astype(BF16)
            upd = lax.dot_general(lhs, rhs, tn, preferred_element_type=F32)
            s_ref[b, h] = upd[:, HEAD_DIM:] * s0 + upd[:, :HEAD_DIM]
    o = intra + inter_ref[...]
    gn = gn_ref[...]
    o = jnp.concatenate([_rms(o[:, h * HEAD_DIM:(h + 1) * HEAD_DIM], gn) for h in range(heads)], axis=1)
    o_ref[...] = (o * gs_ref[...].astype(F32)).astype(BF16)


def _hgrn_sample(q, k, lf, v, gs, g_norm, s0, row0, steps):
    n_batch, heads = s0.shape[:2]
    width = heads * HEAD_DIM
    assert steps & (steps - 1) == 0 and SUBLANES - steps >= 3
    bb = min(SAMPLE_BATCH_BLOCK, n_batch)
    rows = bb * steps
    assert n_batch % bb == 0 and rows % SUBLANES == 0 and row0 % rows == 0
    blk0 = row0 // rows
    tok = pl.BlockSpec((rows, width), lambda i: (blk0 + i, 0))
    st = pl.BlockSpec((bb, heads, HEAD_DIM, HEAD_DIM), lambda i: (i, 0, 0, 0))
    return pl.pallas_call(
        functools.partial(_hgrn_sample_kernel, steps=steps),
        grid=(n_batch // bb,),
        in_specs=[tok, tok, tok, tok, tok, pl.BlockSpec((1, HEAD_DIM), lambda i: (0, 0)), st],
        out_specs=[pl.BlockSpec((rows, width), lambda i: (i, 0)), st],
        out_shape=[jax.ShapeDtypeStruct((n_batch * steps, width), BF16),
                   jax.ShapeDtypeStruct(s0.shape, F32)],
        scratch_shapes=[pltpu.VMEM((rows, width), F32)],
        compiler_params=_params(("arbitrary",)),
        name="hgrn_sample",
    )(q, k, lf, v, gs, g_norm, s0)


def _outproj_kernel(cp_ref, cs_ref, op_ref, os_ref, xp_ref, xs_ref, w_hbm, g_ref, wr_ref, br_ref, tri_ref,
                    x1_ref, h2_ref, ri_ref, rw_ref, cnt_ref, run_ref, w_ref, stage_ref, wsem,
                    *, n_prompt_tiles, n_experts):
    i = pl.program_id(0)
    is_p = i < n_prompt_tiles
    ch = cp_ref.shape[1]

    @pl.when(i == 0)
    def _():
        run_ref[...] = jnp.zeros_like(run_ref)
        _load_weight_as_bf16(w_hbm, w_ref, stage_ref, wsem)

    def mix(c_ref, o_ref, x_ref):
        y = jnp.dot(c_ref[...], w_ref[0:ch, :], preferred_element_type=F32)
        y = y + jnp.dot(o_ref[...], w_ref[ch:, :], preferred_element_type=F32)
        x1_ref[...] = x_ref[...] + y

    pl.when(is_p)(lambda: mix(cp_ref, op_ref, xp_ref))
    pl.when(jnp.logical_not(is_p))(lambda: mix(cs_ref, os_ref, xs_ref))

    h2 = _rms(x1_ref[...], g_ref[...])
    h2_ref[...] = h2
    a1, a2, _ = _split3(h2)
    p1 = jnp.dot(a1, wr_ref[...], preferred_element_type=F32)
    p2 = jnp.dot(a2, wr_ref[...], preferred_element_type=F32)
    logits = p1 + pltpu.roll(p1, LANES // 2, 1) + p2 + br_ref[...]
    tm = logits.shape[0]
    lane = lax.broadcasted_iota(jnp.int32, (tm, LANES), 1)
    lane_f = lane.astype(F32)
    neg = jnp.float32(-jnp.inf)

    def top(x):
        m = jnp.max(x, axis=-1, keepdims=True)
        idx = jnp.min(jnp.where(x == m, lane_f, float(LANES)), axis=-1, keepdims=True)
        return m, idx.astype(jnp.int32)

    is_group = (lane >= n_experts) & (lane < n_experts + N_GROUPS)
    gl = jnp.where(is_group, logits, neg)
    gmax, gidx = top(gl)
    p_top = 1.0 / jnp.sum(jnp.exp(gl - gmax), axis=-1, keepdims=True)
    g_lo = (gidx - n_experts) * EXPERTS_PER_GROUP
    el = jnp.where((lane >= g_lo) & (lane < g_lo + EXPERTS_PER_GROUP), logits, neg)
    v1, e1 = top(el)
    v2, e2 = top(jnp.where(lane == e1, neg, el))
    t = jnp.exp(v2 - v1)
    w1 = p_top / (1.0 + t)
    w2 = p_top * t / (1.0 + t)
    hot = ((lane == e1) | (lane == e2)).astype(F32)
    before = run_ref[...] + jnp.dot(tri_ref[...], hot.astype(BF16), preferred_element_type=F32)
    r1 = jnp.sum(jnp.where(lane == e1, before, 0.0), axis=-1, keepdims=True)
    r2 = jnp.sum(jnp.where(lane == e2, before, 0.0), axis=-1, keepdims=True)
    run_ref[...] = run_ref[...] + jnp.sum(hot, axis=0, keepdims=True)
    cnt_ref[...] = run_ref[...]
    info = jnp.where(lane == 0, e1.astype(F32), jnp.where(lane == 1, e2.astype(F32),
                     jnp.where(lane == 2, r1, jnp.where(lane == 3, r2, 0.0))))
    ri_ref[...] = info.T[0:SUBLANES, :].astype(jnp.int32)
    rw_ref[...] = jnp.where(lane == 0, w1, jnp.where(lane == 1, w2, 0.0))


def _outproj(c_p, c_s, o_p, o_s, xp, xs, w_out, norm_g, w_router3, b_router, n_experts):
    n_p, d = xp.shape
    n_s = xs.shape[0]
    tm = min(TOKEN_TILE, n_s)
    npt, nst = n_p // tm, n_s // tm
    n = n_p + n_s
    ch = c_p.shape[1]
    hv = o_p.shape[1]
    cw = min(WEIGHT_STAGE_COLS, d)
    assert w_out.shape == (ch + hv, d) and d % cw == 0
    tri = jnp.asarray(np.tril(np.ones((tm, tm), np.float32), -1), BF16)
    pidx = lambda i: (jnp.minimum(i, npt - 1), 0)
    sidx = lambda i: (jnp.maximum(i - npt, 0), 0)
    row = lambda i: (i, 0)
    c2 = lambda i: (0, 0)
    return pl.pallas_call(
        functools.partial(_outproj_kernel, n_prompt_tiles=npt, n_experts=n_experts),
        grid=(npt + nst,),
        in_specs=[
            pl.BlockSpec((tm, ch), pidx), pl.BlockSpec((tm, ch), sidx),
            pl.BlockSpec((tm, hv), pidx), pl.BlockSpec((tm, hv), sidx),
            pl.BlockSpec((tm, d), pidx), pl.BlockSpec((tm, d), sidx),
            pl.BlockSpec(memory_space=pl.ANY),
            pl.BlockSpec((1, d), c2),
            pl.BlockSpec((d, LANES), c2),
            pl.BlockSpec((1, LANES), c2),
            pl.BlockSpec((tm, tm), c2),
        ],
        out_specs=[pl.BlockSpec((tm, d), row), pl.BlockSpec((tm, d), row),
                   pl.BlockSpec((SUBLANES, tm), lambda i: (0, i)), pl.BlockSpec((tm, LANES), row),
                   pl.BlockSpec((1, LANES), c2)],
        out_shape=[jax.ShapeDtypeStruct((n, d), F32), jax.ShapeDtypeStruct((n, d), F32),
                   jax.ShapeDtypeStruct((SUBLANES, n), jnp.int32), jax.ShapeDtypeStruct((n, LANES), F32),
                   jax.ShapeDtypeStruct((1, LANES), F32)],
        scratch_shapes=[pltpu.VMEM((1, LANES), F32), pltpu.VMEM((ch + hv, d), BF16),
                        pltpu.VMEM((2, ch + hv, cw), F32), pltpu.SemaphoreType.DMA((2,))],
        compiler_params=_params(("arbitrary",)),
        name="outproj",
    )(c_p, c_s, o_p, o_s, xp, xs, w_out, norm_g, w_router3, b_router, tri)


def _experts_kernel(dest_ref, te_ref, tf_ref, tn_ref, ts_ref, nu_ref, h2_ref, wg_ref, wu_ref, wd_ref, y_ref,
                    x_ref, wgf_ref, wuf_ref, wdf_ref, wgb_ref, wub_ref, wdb_ref, src_ref, sem, wsem):
    i = pl.program_id(0)
    n_used = nu_ref[0]
    rows = x_ref.shape[1]

    def invert_routing():
        n_tokens = dest_ref.shape[0] // 2
        spread = (1 << (n_tokens.bit_length() - 1)) - 1

        def fill(g, carry):
            token = (g * GATHER_UNROLL) & spread
            for u in range(GATHER_UNROLL):
                src_ref[g * GATHER_UNROLL + u] = token
            return carry
        lax.fori_loop(0, src_ref.shape[0] // GATHER_UNROLL, fill, 0)

        def place(g, carry):
            for u in range(GATHER_UNROLL):
                t = g * GATHER_UNROLL + u
                src_ref[dest_ref[t]] = t
                src_ref[dest_ref[n_tokens + t]] = t
            return carry
        lax.fori_loop(0, n_tokens // GATHER_UNROLL, place, 0)

    def row_copy(tile, slot, r):
        return pltpu.make_async_copy(h2_ref.at[pl.ds(src_ref[tile * rows + r], 1)],
                                     x_ref.at[slot, pl.ds(r, 1)], sem.at[slot])

    def gather(tile, slot):
        def start(g, carry):
            for u in range(GATHER_UNROLL):
                row_copy(tile, slot, g * GATHER_UNROLL + u).start()
            return carry
        lax.fori_loop(0, rows // GATHER_UNROLL, start, 0)

    def weight_copies(e, slot):
        return [pltpu.make_async_copy(src.at[e], dst.at[slot], wsem.at[slot])
                for src, dst in ((wg_ref, wgf_ref), (wu_ref, wuf_ref), (wd_ref, wdf_ref))]

    @pl.when(i == 0)
    def _():
        for c in weight_copies(te_ref[0], 0):
            c.start(priority=1)
        invert_routing()
        gather(0, 0)

    @pl.when(tf_ref[i] == 1)
    def _():
        wslot = ts_ref[i]

        @pl.when(tn_ref[i] >= 0)
        def _():
            for c in weight_copies(tn_ref[i], 1 - wslot):
                c.start(priority=1)
        for c in weight_copies(te_ref[i], wslot):
            c.wait()
        wgb_ref[...] = wgf_ref[wslot].astype(BF16)
        wub_ref[...] = wuf_ref[wslot].astype(BF16)
        wdb_ref[...] = wdf_ref[wslot].astype(BF16)

    def tile_step(prefetch):
        slot = i % 2

        def wait(g, carry):
            for u in range(GATHER_UNROLL):
                row_copy(i, slot, g * GATHER_UNROLL + u).wait()
            return carry
        lax.fori_loop(0, rows // GATHER_UNROLL, wait, 0)

        f = wgb_ref.shape[1]
        d = wdb_ref.shape[1]
        fc, dc = min(MXU_COLS, f), min(MXU_COLS, d)
        n_pieces = 2 * (f // fc) + d // dc
        per_piece = -(-rows // n_pieces)
        issued = [0]

        def issue_share():
            if not prefetch:
                return
            for r in range(issued[0], min(issued[0] + per_piece, rows)):
                row_copy(i + 1, 1 - slot, r).start()
            issued[0] = min(issued[0] + per_piece, rows)

        x = x_ref[slot].astype(BF16)
        hid = []
        for j in range(f // fc):
            cs = slice(j * fc, (j + 1) * fc)
            hg = jnp.dot(x, wgb_ref[:, cs], preferred_element_type=F32)
            issue_share()
            hu = jnp.dot(x, wub_ref[:, cs], preferred_element_type=F32)
            issue_share()
            hid.append((_silu(hg) * hu).astype(BF16))
        hid = jnp.concatenate(hid, axis=1)
        for j in range(d // dc):
            cs = slice(j * dc, (j + 1) * dc)
            y_ref[:, cs] = jnp.dot(hid, wdb_ref[:, cs], preferred_element_type=F32)
            issue_share()

    pl.when(i + 1 < n_used)(lambda: tile_step(True))
    pl.when(i + 1 == n_used)(lambda: tile_step(False))

    @pl.when(i >= n_used)
    def _():
        y_ref[...] = jnp.zeros_like(y_ref)


def _experts(dest_flat, n_rows, tile_expert, tile_first, tile_next, tile_slot, n_used, h2, w_gate, w_up, w_down):
    d = h2.shape[1]
    f = w_gate.shape[2]
    n_tiles = n_rows // EXPERT_TILE
    any_space = pl.BlockSpec(memory_space=pl.ANY)
    return pl.pallas_call(
        _experts_kernel,
        grid_spec=pltpu.PrefetchScalarGridSpec(
            num_scalar_prefetch=6, grid=(n_tiles,),
            in_specs=[any_space, any_space, any_space, any_space],
            out_specs=pl.BlockSpec((EXPERT_TILE, d), lambda i, *_: (i, 0)),
            scratch_shapes=[pltpu.VMEM((2, EXPERT_TILE, d), F32),
                            pltpu.VMEM((2, d, f), F32), pltpu.VMEM((2, d, f), F32), pltpu.VMEM((2, f, d), F32),
                            pltpu.VMEM((d, f), BF16), pltpu.VMEM((d, f), BF16), pltpu.VMEM((f, d), BF16),
                            pltpu.SMEM((n_rows,), jnp.int32),
                            pltpu.SemaphoreType.DMA((2,)), pltpu.SemaphoreType.DMA((2,))],
        ),
        out_shape=jax.ShapeDtypeStruct((n_rows, d), F32),
        compiler_params=_params(("arbitrary",)),
        name="experts",
    )(dest_flat, tile_expert, tile_first, tile_next, tile_slot, n_used, h2, w_gate, w_up, w_down)


def _combine_kernel(dest_ref, ys_ref, x1_ref, rw_ref, g_ref, yp_ref, ysm_ref, buf_ref, sem, *, n_prompt_tiles):
    i = pl.program_id(0)
    tm = x1_ref.shape[0]

    n_tokens = dest_ref.shape[0] // 2
    n_steps = pl.num_programs(0)

    def copy(tile, r, slot):
        return pltpu.make_async_copy(ys_ref.at[pl.ds(dest_ref[slot * n_tokens + tile * tm + r], 1)],
                                     buf_ref.at[tile % 2, slot, pl.ds(r, 1)], sem.at[tile % 2])

    def gather(tile):
        def start(r, carry):
            copy(tile, r, 0).start(priority=0)
            copy(tile, r, 1).start(priority=1)
            return carry
        lax.fori_loop(0, tm, start, 0, unroll=GATHER_UNROLL)

    @pl.when(i == 0)
    def _():
        gather(0)

    @pl.when(i + 1 < n_steps)
    def _():
        gather(i + 1)

    def wait(r, carry):
        copy(i, r, 0).wait()
        copy(i, r, 1).wait()
        return carry

    lax.fori_loop(0, tm, wait, 0, unroll=GATHER_UNROLL)
    rw = rw_ref[...]
    par = i % 2
    x2 = x1_ref[...] + rw[:, 0:1] * buf_ref[par, 0] + rw[:, 1:2] * buf_ref[par, 1]
    y = _rms(x2, g_ref[...])

    @pl.when(i < n_prompt_tiles)
    def _():
        yp_ref[...] = y

    @pl.when(i >= n_prompt_tiles)
    def _():
        ysm_ref[...] = y


def _combine(dest_flat, ys, x1, rw, norm_g, n_p, n_s):
    n, d = x1.shape
    tm = min(CONV_TILE, n_s)
    npt, nst = n_p // tm, n_s // tm
    row = lambda i, dest: (i, 0)
    return pl.pallas_call(
        functools.partial(_combine_kernel, n_prompt_tiles=npt),
        grid_spec=pltpu.PrefetchScalarGridSpec(
            num_scalar_prefetch=1, grid=(npt + nst,),
            in_specs=[pl.BlockSpec(memory_space=pl.ANY),
                      pl.BlockSpec((tm, d), row), pl.BlockSpec((tm, LANES), row),
                      pl.BlockSpec((1, d), lambda i, dest: (0, 0))],
            out_specs=[pl.BlockSpec((tm, d), lambda i, dest: (jnp.minimum(i, npt - 1), 0)),
                       pl.BlockSpec((tm, d), lambda i, dest: (jnp.maximum(i - npt, 0), 0))],
            scratch_shapes=[pltpu.VMEM((2, 2, tm, d), F32), pltpu.SemaphoreType.DMA((2,))],
        ),
        out_shape=[jax.ShapeDtypeStruct((n_p, d), F32), jax.ShapeDtypeStruct((n_s, d), F32)],
        compiler_params=_params(("arbitrary",)),
        name="combine",
    )(dest_flat, ys, x1, rw, norm_g)


def kernel(x_prompt, x_sample, state_conv, state_hgrn, norm_mix, w_in, w_dw, b_dw, ln_conv_g, ln_conv_b, lb_logits, hgrn_norm_g, w_out, norm_ffn, w_router_group, b_router_group, w_router_expert, b_router_expert, w_exp_gate, w_exp_up, w_exp_down, norm_final):
    assert w_in.shape[0] == 1, "single-layer trunk"
    n_batch, seq, d = x_prompt.shape
    s_batch, steps, _ = x_sample.shape
    ch = w_dw.shape[-1]
    hk = lb_logits.shape[-1]
    heads = hk // HEAD_DIM
    n_experts = w_exp_gate.shape[1]
    assert n_experts == N_GROUPS * EXPERTS_PER_GROUP and n_experts + N_GROUPS <= LANES // 2
    n_p, n_s = n_batch * seq, s_batch * steps
    n = n_p + n_s
    xp = x_prompt.reshape(n_p, d)
    xs = x_sample.reshape(n_s, d)

    u, q, k, lf, v, gs = _inproj(xp, xs, norm_mix, w_in[0], lb_logits, ch, hk)

    c_p = _conv_prompt(u, n_batch, seq, w_dw[0], b_dw, ln_conv_g, ln_conv_b)
    u_t = u[n_p:].reshape(s_batch, steps, ch).transpose(1, 0, 2)
    c_t, new_conv_t = _conv_sample(u_t, state_conv[0].transpose(1, 0, 2), w_dw[0], b_dw, ln_conv_g, ln_conv_b)
    c_s = c_t.transpose(1, 0, 2).reshape(n_s, ch)
    new_conv_sample = new_conv_t.transpose(1, 0, 2)[None]
    hist = state_conv.shape[2]
    new_conv_prompt = jnp.stack([u[(b + 1) * seq - hist:(b + 1) * seq] for b in range(n_batch)])

    o_p, hgrn_p = _hgrn_prompt(q, k, lf, v, gs, hgrn_norm_g, n_batch, seq, heads)
    o_s, hgrn_s = _hgrn_sample(q, k, lf, v, gs, hgrn_norm_g, state_hgrn[0], n_p, steps)

    w_r = jnp.concatenate([w_router_expert[0], w_router_group[0]], axis=1)
    w_r = jnp.pad(w_r, ((0, 0), (0, LANES // 2 - w_r.shape[1])))
    r1 = w_r.astype(BF16)
    r2 = (w_r - r1.astype(F32)).astype(BF16)
    b_r = jnp.pad(jnp.concatenate([b_router_expert[0], b_router_group[0]]), (0, LANES - n_experts - N_GROUPS))[None]
    x1, h2, ri, rw, counts = _outproj(c_p, c_s, o_p, o_s, xp, xs, w_out[0], norm_ffn,
                                      jnp.concatenate([r1, r2], axis=1), b_r, n_experts)

    cnt = counts[0, :n_experts].astype(jnp.int32)
    tiles_per = (cnt + EXPERT_TILE - 1) // EXPERT_TILE
    tile_end = jnp.cumsum(tiles_per)
    row_start = (tile_end - tiles_per) * EXPERT_TILE
    n_tiles = (2 * n) // EXPERT_TILE + n_experts
    is_e = ri[0:2, :, None] == jnp.arange(n_experts, dtype=jnp.int32)
    dest = jnp.sum(jnp.where(is_e, row_start, 0), axis=-1) + ri[2:4]
    dest_flat = dest.reshape(-1)
    n_used = tile_end[-1:]
    tid = jnp.minimum(jnp.arange(n_tiles, dtype=jnp.int32), n_used - 1)
    tile_expert = jnp.sum((tile_end[None, :] <= tid[:, None]).astype(jnp.int32), axis=1)
    prev = jnp.concatenate([jnp.full((1,), -1, jnp.int32), tile_expert[:-1]])
    tile_first = ((tile_expert != prev) & (jnp.arange(n_tiles) < n_used)).astype(jnp.int32)
    eid = jnp.arange(n_experts, dtype=jnp.int32)
    used = tiles_per > 0
    later = used[None, :] & (eid[None, :] > eid[:, None])
    next_expert = jnp.min(jnp.where(later, eid[None, :], n_experts), axis=1)
    next_expert = jnp.where(next_expert == n_experts, -1, next_expert)
    parity = (jnp.cumsum(used.astype(jnp.int32)) - 1) % 2
    tile_is = tile_expert[:, None] == eid[None, :]
    tile_next = jnp.sum(jnp.where(tile_is, next_expert[None, :], 0), axis=1).astype(jnp.int32)
    tile_slot = jnp.sum(jnp.where(tile_is, parity[None, :], 0), axis=1).astype(jnp.int32)

    ys_sorted = _experts(dest_flat, n_tiles * EXPERT_TILE, tile_expert, tile_first, tile_next, tile_slot,
                         n_used.astype(jnp.int32), h2, w_exp_gate[0], w_exp_up[0], w_exp_down[0])
    y_p, y_s = _combine(dest_flat, ys_sorted, x1, rw, norm_final[None], n_p, n_s)

    return (y_p.reshape(n_batch, seq, d), y_s.reshape(s_batch, steps, d),
            new_conv_prompt[None], hgrn_p[None], new_conv_sample, hgrn_s[None])
```

```python
import functools

import numpy as np
import jax
import jax.numpy as jnp
from jax import lax
from jax.experimental import pallas as pl
from jax.experimental.pallas import tpu as pltpu

F32 = jnp.float32
BF16 = jnp.bfloat16
EPS = 1e-6
LANES = 128
SUBLANES = 8
HEAD_DIM = 128
HGRN_CHUNK = 64
HGRN_HEADS_PER_STEP = 2
HGRN_CHUNK_GROUP = 8
FAST_DECAY_LIMIT = -60.0
N_GROUPS = 4
EXPERTS_PER_GROUP = 8
VMEM_LIMIT = 56 * 1024 * 1024
TOKEN_TILE = 256
INPROJ_TILE = 256
WEIGHT_STAGE_COLS = 512
CONV_TILE = 256
CONV_HALO = 32
CONV_ROWS = 64
EXPERT_TILE = 256
SAMPLE_BATCH_BLOCK = 8
SAMPLE_CONV_BATCH_BLOCK = 16
GATHER_UNROLL = 8
MXU_COLS = 256


def _sigmoid(x):
    return 1.0 / (1.0 + jnp.exp(-x))


def _silu(x):
    return x * _sigmoid(x)


def _rms(x, g):
    return x * lax.rsqrt(jnp.mean(x * x, axis=-1, keepdims=True) + EPS) * g


def _split3(x):
    h1 = x.astype(BF16)
    r1 = x - h1.astype(F32)
    h2 = r1.astype(BF16)
    h3 = (r1 - h2.astype(F32)).astype(BF16)
    return h1, h2, h3


def _params(sem, flags=None):
    return pltpu.CompilerParams(dimension_semantics=sem, vmem_limit_bytes=VMEM_LIMIT, flags=flags)


def _load_weight_as_bf16(w_hbm, w_ref, stage_ref, wsem):
    cw = stage_ref.shape[2]
    n_cols = w_ref.shape[1] // cw
    copies = [pltpu.make_async_copy(w_hbm.at[:, pl.ds(c * cw, cw)], stage_ref.at[c % 2], wsem.at[c % 2])
              for c in range(n_cols)]
    copies[0].start()
    for c in range(n_cols):
        if c + 1 < n_cols:
            copies[c + 1].start()
        copies[c].wait()
        w_ref[:, c * cw:(c + 1) * cw] = stage_ref[c % 2].astype(BF16)


def _inproj_kernel(xp_ref, xs_ref, g_ref, w_hbm, lbl_ref,
                   u_ref, q_ref, k_ref, lf_ref, v_ref, gs_ref, h_ref, w_ref, stage_ref, wsem,
                   *, n_prompt_tiles, ch, hk):
    i = pl.program_id(0)
    pl.when(i == 0)(lambda: _load_weight_as_bf16(w_hbm, w_ref, stage_ref, wsem))

    @pl.when(i < n_prompt_tiles)
    def _():
        h_ref[...] = _rms(xp_ref[...], g_ref[...]).astype(BF16)

    @pl.when(i >= n_prompt_tiles)
    def _():
        h_ref[...] = _rms(xs_ref[...], g_ref[...]).astype(BF16)

    h = h_ref[...]

    def proj(c0, width):
        return jnp.dot(h, w_ref[:, c0:c0 + width], preferred_element_type=F32)

    a = proj(0, ch)
    ga = proj(ch, ch)
    u_ref[...] = a * _sigmoid(ga)
    q = proj(2 * ch, hk)
    q_ref[...] = _silu(q).astype(BF16)
    f = proj(2 * ch + hk, hk)
    lbl = lbl_ref[...]
    e = jnp.exp(lbl - jnp.max(lbl, axis=0, keepdims=True))
    lb = e[0:1, :] / jnp.sum(e, axis=0, keepdims=True)
    fg = lb + (1.0 - lb) * _sigmoid(f)
    k_ref[...] = (1.0 - fg).astype(BF16)
    lf_ref[...] = jnp.log(fg)
    g = proj(2 * ch + 3 * hk, hk)
    gs_ref[...] = _silu(g).astype(BF16)
    v_ref[...] = proj(2 * ch + 2 * hk, hk).astype(BF16)


def _inproj(xp, xs, norm_g, w_in, lb_logits, ch, hk):
    n_p, d = xp.shape
    n_s = xs.shape[0]
    tm = min(INPROJ_TILE, n_s)
    assert n_p % tm == 0 and n_s % tm == 0
    npt, nst = n_p // tm, n_s // tm
    n = n_p + n_s
    cols = w_in.shape[1]
    cw = min(WEIGHT_STAGE_COLS, cols)
    assert cols % cw == 0
    row = lambda i: (i, 0)
    const = lambda i: (0, 0)
    outs = [jax.ShapeDtypeStruct((n, ch), F32)] + [
        jax.ShapeDtypeStruct((n, hk), dt) for dt in (BF16, BF16, F32, BF16, BF16)]
    return pl.pallas_call(
        functools.partial(_inproj_kernel, n_prompt_tiles=npt, ch=ch, hk=hk),
        grid=(npt + nst,),
        in_specs=[
            pl.BlockSpec((tm, d), lambda i: (jnp.minimum(i, npt - 1), 0)),
            pl.BlockSpec((tm, d), lambda i: (jnp.maximum(i - npt, 0), 0)),
            pl.BlockSpec((1, d), const),
            pl.BlockSpec(memory_space=pl.ANY),
            pl.BlockSpec(lb_logits.shape, const),
        ],
        out_specs=[pl.BlockSpec((tm, ch), row)] + [pl.BlockSpec((tm, hk), row)] * 5,
        out_shape=outs,
        scratch_shapes=[pltpu.VMEM((tm, d), BF16), pltpu.VMEM((d, cols), BF16),
                        pltpu.VMEM((2, d, cw), F32), pltpu.SemaphoreType.DMA((2,))],
        compiler_params=_params(("arbitrary",)),
        name="inproj",
    )(xp, xs, norm_g, w_in, lb_logits)


def _ln_silu(c, g, b):
    mu = jnp.mean(c, axis=-1, keepdims=True)
    d = c - mu
    var = jnp.mean(d * d, axis=-1, keepdims=True)
    return _silu(d * lax.rsqrt(var + EPS) * g + b)


def _conv_prompt_kernel(halo_ref, cur_ref, w_ref, b_ref, lg_ref, lb_ref, c_ref, ext_ref, acc_ref, *, width):
    t = pl.program_id(1)
    tt, ch = cur_ref.shape
    halo = halo_ref[...]
    ext_ref[0:CONV_HALO, :] = jnp.where(t == 0, jnp.zeros_like(halo), halo)
    ext_ref[CONV_HALO:, :] = cur_ref[...]
    off = CONV_HALO - (width - 1)
    rows = min(CONV_ROWS, tt)
    for l in range(ch // LANES):
        ls = slice(l * LANES, (l + 1) * LANES)
        wl = w_ref[:, ls]
        bl = b_ref[:, ls]
        for r0 in range(0, tt, rows):
            acc = jnp.broadcast_to(bl, (rows, LANES))
            for res in range(SUBLANES):
                extra = SUBLANES if res else 0
                part = None
                for a in range((off + width - 1) // SUBLANES + 1):
                    j = SUBLANES * a + res - off
                    if 0 <= j < width:
                        lo = r0 + SUBLANES * a
                        term = ext_ref[lo:lo + rows + extra, ls] * wl[j:j + 1, :]
                        part = term if part is None else part + term
                acc = acc + part[res:res + rows]
            acc_ref[r0:r0 + rows, ls] = acc
    c_ref[...] = _ln_silu(acc_ref[...], lg_ref[...], lb_ref[...]).astype(BF16)


def _conv_prompt(u, n_batch, seq, w_dw, b_dw, ln_g, ln_b):
    ch = u.shape[1]
    width = w_dw.shape[0]
    tt = min(CONV_TILE, seq)
    assert seq % tt == 0 and tt % CONV_HALO == 0 and width - 1 <= CONV_HALO
    nt = seq // tt
    hb = tt // CONV_HALO
    const = lambda b, t: (0, 0)
    return pl.pallas_call(
        functools.partial(_conv_prompt_kernel, width=width),
        grid=(n_batch, nt),
        in_specs=[
            pl.BlockSpec((CONV_HALO, ch), lambda b, t: (jnp.maximum((b * nt + t) * hb - 1, 0), 0)),
            pl.BlockSpec((tt, ch), lambda b, t: (b * nt + t, 0)),
            pl.BlockSpec((width, ch), const),
            pl.BlockSpec((1, ch), const),
            pl.BlockSpec((1, ch), const),
            pl.BlockSpec((1, ch), const),
        ],
        out_specs=pl.BlockSpec((tt, ch), lambda b, t: (b * nt + t, 0)),
        out_shape=jax.ShapeDtypeStruct((n_batch * seq, ch), BF16),
        scratch_shapes=[pltpu.VMEM((CONV_HALO + tt, ch), F32), pltpu.VMEM((tt, ch), F32)],
        compiler_params=_params(("arbitrary", "arbitrary")),
        name="conv_prompt",
    )(u, u, w_dw, b_dw, ln_g, ln_b)


def _conv_sample_kernel(state_ref, u_ref, w_ref, b_ref, lg_ref, lb_ref, c_ref, new_ref):
    hist = state_ref.shape[0]
    steps = u_ref.shape[0]
    for r in range(hist - steps):
        new_ref[r] = state_ref[r + steps]
    for s in range(steps):
        new_ref[hist - steps + s] = u_ref[s]
    for t in range(steps):
        acc = jnp.broadcast_to(b_ref[...], u_ref.shape[1:])
        for r in range(t, hist):
            acc = acc + state_ref[r] * w_ref[r - t:r - t + 1, :]
        for s in range(t + 1):
            acc = acc + u_ref[s] * w_ref[hist - t + s:hist - t + s + 1, :]
        c_ref[t] = _ln_silu(acc, lg_ref[...], lb_ref[...]).astype(BF16)


def _conv_sample(u_t, state_t, w_dw, b_dw, ln_g, ln_b):
    hist, n_batch, ch = state_t.shape
    width = w_dw.shape[0]
    steps = u_t.shape[0]
    assert hist == width - 1 and steps <= hist
    bb = min(SAMPLE_CONV_BATCH_BLOCK, n_batch)
    assert n_batch % bb == 0
    c2 = lambda i: (0, 0)
    blk = lambda i: (0, i, 0)
    return pl.pallas_call(
        _conv_sample_kernel,
        grid=(n_batch // bb,),
        in_specs=[
            pl.BlockSpec((hist, bb, ch), blk),
            pl.BlockSpec((steps, bb, ch), blk),
            pl.BlockSpec((width, ch), c2),
            pl.BlockSpec((1, ch), c2),
            pl.BlockSpec((1, ch), c2),
            pl.BlockSpec((1, ch), c2),
        ],
        out_specs=[pl.BlockSpec((steps, bb, ch), blk), pl.BlockSpec((hist, bb, ch), blk)],
        out_shape=[jax.ShapeDtypeStruct((steps, n_batch, ch), BF16),
                   jax.ShapeDtypeStruct((hist, n_batch, ch), F32)],
        compiler_params=_params(("arbitrary",)),
        name="conv_sample",
    )(state_t, u_t, w_dw, b_dw, ln_g, ln_b)


def _chunk_constants(c):
    levels = int(np.log2(c))
    assert 2 ** levels == c
    t = np.arange(c)[:, None]
    j = np.arange(c)[None, :]
    mats = [(j <= t)]
    right = np.zeros((levels, c, LANES), np.float32)
    left = np.zeros((levels, c, LANES), np.float32)
    same = np.zeros((levels, c, c), np.float32)
    for lv in range(levels):
        half = c >> (lv + 1)
        split = (t // (2 * half)) * (2 * half) + half
        is_right = t >= split
        mats.append(np.where(is_right, (j >= split) & (j <= t), (j > t) & (j < split)))
        right[lv] = is_right
        left[lv] = ~is_right
        same[lv] = (t // (2 * half)) == (j // (2 * half))
    mats.append(j > t)
    dm = np.concatenate(mats, axis=0).astype(np.float32)
    dm = np.concatenate([dm, dm, dm], axis=1)
    return (jnp.asarray(dm, BF16), jnp.asarray(right), jnp.asarray(left), jnp.asarray(same))


def _hgrn_prompt_kernel(q_ref, k_ref, lf_ref, v_ref, gs_ref, gn_ref, dm_ref, rm_ref, lm_ref, bm_ref,
                        o_ref, s_ref, oi_ref, qe_ref, dec_ref, kv_ref, st_ref, *, chunk):
    seq, width = q_ref.shape
    hp = width // HEAD_DIM
    levels = rm_ref.shape[0]
    n_chunks = seq // chunk
    nt = (((1,), (1,)), ((), ()))
    tn = (((0,), (0,)), ((), ()))
    row_i = lax.broadcasted_iota(jnp.int32, (chunk, chunk), 0)
    col_i = lax.broadcasted_iota(jnp.int32, (chunk, chunk), 1)

    def intra(chains, robust):
        n = len(chains)
        hs = [slice(h * HEAD_DIM, (h + 1) * HEAD_DIM) for _, h in chains]
        rows = [pl.ds(pl.multiple_of(c * chunk, chunk), chunk) for c, _ in chains]
        q = [q_ref[rows[i], hs[i]].astype(F32) for i in range(n)]
        k = [k_ref[rows[i], hs[i]].astype(F32) for i in range(n)]
        v = [v_ref[rows[i], hs[i]] for i in range(n)]
        lsp = [jnp.concatenate(_split3(lf_ref[rows[i], hs[i]]), axis=0) for i in range(n)]

        def decay_sums(i, dm):
            return jnp.dot(dm, lsp[i], preferred_element_type=F32)

        if robust:
            ex = [jnp.exp(decay_sums(i, dm_ref[...])) for i in range(n)]
            e_cum = [e[0:chunk] for e in ex]
            e_tail = [e[(levels + 1) * chunk:(levels + 2) * chunk] for e in ex]
            att = [jnp.where(row_i == col_i, jnp.sum(q[i] * k[i], axis=-1, keepdims=True), 0.0) for i in range(n)]
            for lv in range(levels):
                for i in range(n):
                    e = ex[i][(lv + 1) * chunk:(lv + 2) * chunk]
                    ql = (q[i] * e * rm_ref[lv]).astype(BF16)
                    kl = (k[i] * e * lm_ref[lv]).astype(BF16)
                    att[i] = att[i] + bm_ref[lv] * lax.dot_general(ql, kl, nt, preferred_element_type=F32)
            qe = [(q[i] * e_cum[i]).astype(BF16) for i in range(n)]
        else:
            sums = [decay_sums(i, dm_ref[0:chunk, :]) for i in range(n)]
            tails = [s[chunk - 1:chunk, :] - s for s in sums]
            e_cum = [jnp.exp(s) for s in sums]
            e_tail = [jnp.exp(t) for t in tails]
            qe = [(q[i] * e_cum[i]).astype(BF16) for i in range(n)]
            kn = [(k[i] * jnp.exp(-sums[i])).astype(BF16) for i in range(n)]
            att = [lax.dot_general(qe[i], kn[i], nt, preferred_element_type=F32) for i in range(n)]
            att = [jnp.where(row_i >= col_i, a, 0.0) for a in att]
        kv = [lax.dot_general(v[i], (k[i] * e_tail[i]).astype(BF16), tn, preferred_element_type=F32) for i in range(n)]
        oi = [jnp.dot(att[i].astype(BF16), v[i], preferred_element_type=F32) for i in range(n)]
        for i, (c, h) in enumerate(chains):
            kv_ref[c, h] = kv[i]
            oi_ref[rows[i], hs[i]] = oi[i]
            qe_ref[rows[i], hs[i]] = qe[i]
            dec_ref[c, :, hs[i]] = e_cum[i][chunk - 1:chunk, :]

    def finish(chains):
        n = len(chains)
        hs = [slice(h * HEAD_DIM, (h + 1) * HEAD_DIM) for _, h in chains]
        rows = [pl.ds(pl.multiple_of(c * chunk, chunk), chunk) for c, _ in chains]
        inter = [lax.dot_general(qe_ref[rows[i], hs[i]], st_ref[c, h], nt, preferred_element_type=F32)
                 for i, (c, h) in enumerate(chains)]
        o = [oi_ref[rows[i], hs[i]] + inter[i] for i in range(n)]
        o = [_rms(o[i], gn_ref[...]) * gs_ref[rows[i], hs[i]].astype(F32) for i in range(n)]
        for i in range(n):
            o_ref[rows[i], hs[i]] = o[i].astype(BF16)

    def run(robust):
        group = min(HGRN_CHUNK_GROUP, n_chunks)

        def intra_body(g, carry):
            intra([(g * group + u, h) for u in range(group) for h in range(hp)], robust)
            return carry
        lax.fori_loop(0, n_chunks // group, intra_body, 0)

        def scan_body(c, sts):
            new = []
            for h in range(hp):
                st_ref[c, h] = sts[h].astype(BF16)
                new.append(sts[h] * dec_ref[c, :, h * HEAD_DIM:(h + 1) * HEAD_DIM] + kv_ref[c, h])
            return tuple(new)
        sts = lax.fori_loop(0, n_chunks, scan_body, tuple(jnp.zeros((HEAD_DIM, HEAD_DIM), F32) for _ in range(hp)))
        for h in range(hp):
            s_ref[0, h] = sts[h].T

        def finish_body(g, carry):
            finish([(g * group + u, h) for u in range(group) for h in range(hp)])
            return carry
        lax.fori_loop(0, n_chunks // group, finish_body, 0)

    lf_all = lf_ref[...].reshape(n_chunks, chunk, width)
    slowest = jnp.min(jnp.sum(lf_all, axis=1))
    fast = slowest >= FAST_DECAY_LIMIT
    pl.when(fast)(lambda: run(False))
    pl.when(jnp.logical_not(fast))(lambda: run(True))


def _hgrn_prompt(q, k, lf, v, gs, g_norm, n_batch, seq, heads):
    chunk = HGRN_CHUNK if seq % HGRN_CHUNK == 0 else seq
    dm, rm, lm, bm = _chunk_constants(chunk)
    hp = min(HGRN_HEADS_PER_STEP, heads)
    assert heads % hp == 0
    n_chunks = seq // chunk
    tok = pl.BlockSpec((seq, hp * HEAD_DIM), lambda b, h: (b, h))
    c2 = lambda b, h: (0, 0)
    c3 = lambda b, h: (0, 0, 0)
    return pl.pallas_call(
        functools.partial(_hgrn_prompt_kernel, chunk=chunk),
        grid=(n_batch, heads // hp),
        in_specs=[tok, tok, tok, tok, tok,
                  pl.BlockSpec((1, HEAD_DIM), c2),
                  pl.BlockSpec(dm.shape, c2),
                  pl.BlockSpec(rm.shape, c3), pl.BlockSpec(lm.shape, c3), pl.BlockSpec(bm.shape, c3)],
        out_specs=[tok, pl.BlockSpec((1, hp, HEAD_DIM, HEAD_DIM), lambda b, h: (b, h, 0, 0))],
        out_shape=[jax.ShapeDtypeStruct((n_batch * seq, heads * HEAD_DIM), BF16),
                   jax.ShapeDtypeStruct((n_batch, heads, HEAD_DIM, HEAD_DIM), F32)],
        scratch_shapes=[pltpu.VMEM((seq, hp * HEAD_DIM), F32),
                        pltpu.VMEM((seq, hp * HEAD_DIM), BF16),
                        pltpu.VMEM((n_chunks, 1, hp * HEAD_DIM), F32),
                        pltpu.VMEM((n_chunks, hp, HEAD_DIM, HEAD_DIM), F32),
                        pltpu.VMEM((n_chunks, hp, HEAD_DIM, HEAD_DIM), BF16)],
        compiler_params=_params(("arbitrary", "arbitrary")),
        name="hgrn_prompt",
    )(q, k, lf, v, gs, g_norm, dm, rm, lm, bm)


def _hgrn_sample_kernel(q_ref, k_ref, lf_ref, v_ref, gs_ref, gn_ref, s0_ref, o_ref, s_ref, inter_ref, *, steps):
    rows, width = q_ref.shape
    heads = width // HEAD_DIM
    q = q_ref[...].astype(F32)
    k = k_ref[...].astype(F32)
    v = v_ref[...].astype(F32)
    lf = lf_ref[...]
    step = lax.broadcasted_iota(jnp.int32, (rows, 1), 0) & (steps - 1)

    def back(x, d):
        return pltpu.roll(x, d, 0)

    cum = lf
    for d in range(1, steps):
        cum = cum + jnp.where(step >= d, back(lf, d), 0.0)
    tail = jnp.zeros_like(lf)
    for d in range(1, steps):
        tail = tail + jnp.where(step + d < steps, pltpu.roll(lf, rows - d, 0), 0.0)

    def head_sum(x):
        return [jnp.sum(x[:, h * HEAD_DIM:(h + 1) * HEAD_DIM], axis=-1, keepdims=True) for h in range(heads)]

    def head_scale(cols, x):
        return jnp.concatenate([cols[h] * x[:, h * HEAD_DIM:(h + 1) * HEAD_DIM] for h in range(heads)], axis=1)

    intra = head_scale(head_sum(q * k), v)
    for d in range(1, steps):
        ok = step >= d
        rel = jnp.where(ok, cum - back(cum, d), 0.0)
        w = jnp.where(ok, q * back(k, d) * jnp.exp(rel), 0.0)
        intra = intra + head_scale(head_sum(w), back(v, d))

    qe = (q * jnp.exp(cum)).astype(BF16)
    kd = k * jnp.exp(tail)
    total = jnp.exp(cum)
    per = SUBLANES // steps
    grp = lax.broadcasted_iota(jnp.int32, (SUBLANES, 1), 0)
    tn = (((0,), (0,)), ((), ()))
    for b in range(rows // steps):
        r8 = (b // per) * SUBLANES
        lo = (b % per) * steps
        mine = (grp >= lo) & (grp < lo + steps)
        spare = (lo + steps) % SUBLANES
        for h in range(heads):
            hs = slice(h * HEAD_DIM, (h + 1) * HEAD_DIM)
            s0 = s0_ref[b, h]
            res = jnp.dot(qe[r8:r8 + SUBLANES, hs], s0.astype(BF16), preferred_element_type=F32)
            inter_ref[b * steps:(b + 1) * steps, hs] = res[lo:lo + steps]
            d1, d2, d3 = _split3(total[r8 + lo + steps - 1:r8 + lo + steps, hs])
            dec = jnp.where(grp == spare, d1.astype(F32),
                            jnp.where(grp == spare + 1, d2.astype(F32),
                                      jnp.where(grp == spare + 2, d3.astype(F32), 0.0)))
            lhs = jnp.where(mine, kd[r8:r8 + SUBLANES, hs], dec).astype(BF16)
            vb = jnp.where(mine, v[r8:r8 + SUBLANES, hs], 0.0)
            ones = jnp.where(mine, 0.0, 1.0) * jnp.ones((SUBLANES, HEAD_DIM), F32)
            rhs = jnp.concatenate([vb, ones], axis=1).astype(BF16)
            upd = lax.dot_general(lhs, rhs, tn, preferred_element_type=F32)
            s_ref[b, h] = upd[:, HEAD_DIM:] * s0 + upd[:, :HEAD_DIM]
    o = intra + inter_ref[...]
    gn = gn_ref[...]
    o = jnp.concatenate([_rms(o[:, h * HEAD_DIM:(h + 1) * HEAD_DIM], gn) for h in range(heads)], axis=1)
    o_ref[...] = (o * gs_ref[...].astype(F32)).astype(BF16)


def _hgrn_sample(q, k, lf, v, gs, g_norm, s0, row0, steps):
    n_batch, heads = s0.shape[:2]
    width = heads * HEAD_DIM
    assert steps & (steps - 1) == 0 and SUBLANES - steps >= 3
    bb = min(SAMPLE_BATCH_BLOCK, n_batch)
    rows = bb * steps
    assert n_batch % bb == 0 and rows % SUBLANES == 0 and row0 % rows == 0
    blk0 = row0 // rows
    tok = pl.BlockSpec((rows, width), lambda i: (blk0 + i, 0))
    st = pl.BlockSpec((bb, heads, HEAD_DIM, HEAD_DIM), lambda i: (i, 0, 0, 0))
    return pl.pallas_call(
        functools.partial(_hgrn_sample_kernel, steps=steps),
        grid=(n_batch // bb,),
        in_specs=[tok, tok, tok, tok, tok, pl.BlockSpec((1, HEAD_DIM), lambda i: (0, 0)), st],
        out_specs=[pl.BlockSpec((rows, width), lambda i: (i, 0)), st],
        out_shape=[jax.ShapeDtypeStruct((n_batch * steps, width), BF16),
                   jax.ShapeDtypeStruct(s0.shape, F32)],
        scratch_shapes=[pltpu.VMEM((rows, width), F32)],
        compiler_params=_params(("arbitrary",)),
        name="hgrn_sample",
    )(q, k, lf, v, gs, g_norm, s0)


def _outproj_kernel(cp_ref, cs_ref, op_ref, os_ref, xp_ref, xs_ref, w_hbm, g_ref, wr_ref, br_ref, tri_ref,
                    x1_ref, h2_ref, ri_ref, rw_ref, cnt_ref, run_ref, w_ref, stage_ref, wsem,
                    *, n_prompt_tiles, n_experts):
    i = pl.program_id(0)
    is_p = i < n_prompt_tiles
    ch = cp_ref.shape[1]

    @pl.when(i == 0)
    def _():
        run_ref[...] = jnp.zeros_like(run_ref)
        _load_weight_as_bf16(w_hbm, w_ref, stage_ref, wsem)

    def mix(c_ref, o_ref, x_ref):
        y = jnp.dot(c_ref[...], w_ref[0:ch, :], preferred_element_type=F32)
        y = y + jnp.dot(o_ref[...], w_ref[ch:, :], preferred_element_type=F32)
        x1_ref[...] = x_ref[...] + y

    pl.when(is_p)(lambda: mix(cp_ref, op_ref, xp_ref))
    pl.when(jnp.logical_not(is_p))(lambda: mix(cs_ref, os_ref, xs_ref))

    h2 = _rms(x1_ref[...], g_ref[...])
    h2_ref[...] = h2
    a1, a2, _ = _split3(h2)
    p1 = jnp.dot(a1, wr_ref[...], preferred_element_type=F32)
    p2 = jnp.dot(a2, wr_ref[...], preferred_element_type=F32)
    logits = p1 + pltpu.roll(p1, LANES // 2, 1) + p2 + br_ref[...]
    tm = logits.shape[0]
    lane = lax.broadcasted_iota(jnp.int32, (tm, LANES), 1)
    lane_f = lane.astype(F32)
    neg = jnp.float32(-jnp.inf)

    def top(x):
        m = jnp.max(x, axis=-1, keepdims=True)
        idx = jnp.min(jnp.where(x == m, lane_f, float(LANES)), axis=-1, keepdims=True)
        return m, idx.astype(jnp.int32)

    is_group = (lane >= n_experts) & (lane < n_experts + N_GROUPS)
    gl = jnp.where(is_group, logits, neg)
    gmax, gidx = top(gl)
    p_top = 1.0 / jnp.sum(jnp.exp(gl - gmax), axis=-1, keepdims=True)
    g_lo = (gidx - n_experts) * EXPERTS_PER_GROUP
    el = jnp.where((lane >= g_lo) & (lane < g_lo + EXPERTS_PER_GROUP), logits, neg)
    v1, e1 = top(el)
    v2, e2 = top(jnp.where(lane == e1, neg, el))
    t = jnp.exp(v2 - v1)
    w1 = p_top / (1.0 + t)
    w2 = p_top * t / (1.0 + t)
    hot = ((lane == e1) | (lane == e2)).astype(F32)
    before = run_ref[...] + jnp.dot(tri_ref[...], hot.astype(BF16), preferred_element_type=F32)
    r1 = jnp.sum(jnp.where(lane == e1, before, 0.0), axis=-1, keepdims=True)
    r2 = jnp.sum(jnp.where(lane == e2, before, 0.0), axis=-1, keepdims=True)
    run_ref[...] = run_ref[...] + jnp.sum(hot, axis=0, keepdims=True)
    cnt_ref[...] = run_ref[...]
    info = jnp.where(lane == 0, e1.astype(F32), jnp.where(lane == 1, e2.astype(F32),
                     jnp.where(lane == 2, r1, jnp.where(lane == 3, r2, 0.0))))
    ri_ref[...] = info.T[0:SUBLANES, :].astype(jnp.int32)
    rw_ref[...] = jnp.where(lane == 0, w1, jnp.where(lane == 1, w2, 0.0))


def _outproj(c_p, c_s, o_p, o_s, xp, xs, w_out, norm_g, w_router3, b_router, n_experts):
    n_p, d = xp.shape
    n_s = xs.shape[0]
    tm = min(TOKEN_TILE, n_s)
    npt, nst = n_p // tm, n_s // tm
    n = n_p + n_s
    ch = c_p.shape[1]
    hv = o_p.shape[1]
    cw = min(WEIGHT_STAGE_COLS, d)
    assert w_out.shape == (ch + hv, d) and d % cw == 0
    tri = jnp.asarray(np.tril(np.ones((tm, tm), np.float32), -1), BF16)
    pidx = lambda i: (jnp.minimum(i, npt - 1), 0)
    sidx = lambda i: (jnp.maximum(i - npt, 0), 0)
    row = lambda i: (i, 0)
    c2 = lambda i: (0, 0)
    return pl.pallas_call(
        functools.partial(_outproj_kernel, n_prompt_tiles=npt, n_experts=n_experts),
        grid=(npt + nst,),
        in_specs=[
            pl.BlockSpec((tm, ch), pidx), pl.BlockSpec((tm, ch), sidx),
            pl.BlockSpec((tm, hv), pidx), pl.BlockSpec((tm, hv), sidx),
            pl.BlockSpec((tm, d), pidx), pl.BlockSpec((tm, d), sidx),
            pl.BlockSpec(memory_space=pl.ANY),
            pl.BlockSpec((1, d), c2),
            pl.BlockSpec((d, LANES), c2),
            pl.BlockSpec((1, LANES), c2),
            pl.BlockSpec((tm, tm), c2),
        ],
        out_specs=[pl.BlockSpec((tm, d), row), pl.BlockSpec((tm, d), row),
                   pl.BlockSpec((SUBLANES, tm), lambda i: (0, i)), pl.BlockSpec((tm, LANES), row),
                   pl.BlockSpec((1, LANES), c2)],
        out_shape=[jax.ShapeDtypeStruct((n, d), F32), jax.ShapeDtypeStruct((n, d), F32),
                   jax.ShapeDtypeStruct((SUBLANES, n), jnp.int32), jax.ShapeDtypeStruct((n, LANES), F32),
                   jax.ShapeDtypeStruct((1, LANES), F32)],
        scratch_shapes=[pltpu.VMEM((1, LANES), F32), pltpu.VMEM((ch + hv, d), BF16),
                        pltpu.VMEM((2, ch + hv, cw), F32), pltpu.SemaphoreType.DMA((2,))],
        compiler_params=_params(("arbitrary",)),
        name="outproj",
    )(c_p, c_s, o_p, o_s, xp, xs, w_out, norm_g, w_router3, b_router, tri)


def _experts_kernel(dest_ref, pad_ref, te_ref, tf_ref, tn_ref, ts_ref, nu_ref, h2_ref, wg_ref, wu_ref, wd_ref, y_ref,
                    x_ref, wgf_ref, wuf_ref, wdf_ref, wgb_ref, wub_ref, wdb_ref, src_ref, sem, wsem):
    i = pl.program_id(0)
    n_used = nu_ref[0]
    rows = x_ref.shape[1]

    def invert_routing():
        n_tokens = dest_ref.shape[0] // 2
        n_exp = pad_ref.shape[0] // 2
        spread = (1 << (n_tokens.bit_length() - 1)) - 1

        def fill_expert(e, carry):
            def fill(p, c):
                src_ref[p] = p & spread
                return c
            lax.fori_loop(pad_ref[e], pad_ref[n_exp + e], fill, 0)
            return carry
        lax.fori_loop(0, n_exp, fill_expert, 0)

        def place(g, carry):
            for u in range(GATHER_UNROLL):
                t = g * GATHER_UNROLL + u
                src_ref[dest_ref[t]] = t
                src_ref[dest_ref[n_tokens + t]] = t
            return carry
        lax.fori_loop(0, n_tokens // GATHER_UNROLL, place, 0)

    def row_copy(tile, slot, r):
        return pltpu.make_async_copy(h2_ref.at[pl.ds(src_ref[tile * rows + r], 1)],
                                     x_ref.at[slot, pl.ds(r, 1)], sem.at[slot])

    def gather(tile, slot):
        def start(g, carry):
            for u in range(GATHER_UNROLL):
                row_copy(tile, slot, g * GATHER_UNROLL + u).start()
            return carry
        lax.fori_loop(0, rows // GATHER_UNROLL, start, 0)

    def weight_copies(e, slot):
        return [pltpu.make_async_copy(src.at[e], dst.at[slot], wsem.at[slot])
                for src, dst in ((wg_ref, wgf_ref), (wu_ref, wuf_ref), (wd_ref, wdf_ref))]

    @pl.when(i == 0)
    def _():
        for c in weight_copies(te_ref[0], 0):
            c.start(priority=1)
        invert_routing()
        gather(0, 0)

    @pl.when(tf_ref[i] == 1)
    def _():
        wslot = ts_ref[i]

        @pl.when(tn_ref[i] >= 0)
        def _():
            for c in weight_copies(tn_ref[i], 1 - wslot):
                c.start(priority=1)
        for c in weight_copies(te_ref[i], wslot):
            c.wait()
        wgb_ref[...] = wgf_ref[wslot].astype(BF16)
        wub_ref[...] = wuf_ref[wslot].astype(BF16)
        wdb_ref[...] = wdf_ref[wslot].astype(BF16)

    def tile_step(prefetch):
        slot = i % 2

        def wait(g, carry):
            for u in range(GATHER_UNROLL):
                row_copy(i, slot, g * GATHER_UNROLL + u).wait()
            return carry
        lax.fori_loop(0, rows // GATHER_UNROLL, wait, 0)

        f = wgb_ref.shape[1]
        d = wdb_ref.shape[1]
        fc, dc = min(MXU_COLS, f), min(MXU_COLS, d)
        n_pieces = 2 * (f // fc) + d // dc
        per_piece = -(-rows // n_pieces)
        issued = [0]

        def issue_share():
            if not prefetch:
                return
            for r in range(issued[0], min(issued[0] + per_piece, rows)):
                row_copy(i + 1, 1 - slot, r).start()
            issued[0] = min(issued[0] + per_piece, rows)

        x = x_ref[slot].astype(BF16)
        hid = []
        for j in range(f // fc):
            cs = slice(j * fc, (j + 1) * fc)
            hg = jnp.dot(x, wgb_ref[:, cs], preferred_element_type=F32)
            issue_share()
            hu = jnp.dot(x, wub_ref[:, cs], preferred_element_type=F32)
            issue_share()
            hid.append((_silu(hg) * hu).astype(BF16))
        hid = jnp.concatenate(hid, axis=1)
        for j in range(d // dc):
            cs = slice(j * dc, (j + 1) * dc)
            y_ref[:, cs] = jnp.dot(hid, wdb_ref[:, cs], preferred_element_type=F32)
            issue_share()

    pl.when(i + 1 < n_used)(lambda: tile_step(True))
    pl.when(i + 1 == n_used)(lambda: tile_step(False))

    @pl.when(i >= n_used)
    def _():
        y_ref[...] = jnp.zeros_like(y_ref)


def _experts(dest_flat, pad_rows, n_rows, tile_expert, tile_first, tile_next, tile_slot, n_used, h2,
             w_gate, w_up, w_down):
    d = h2.shape[1]
    f = w_gate.shape[2]
    n_tiles = n_rows // EXPERT_TILE
    any_space = pl.BlockSpec(memory_space=pl.ANY)
    return pl.pallas_call(
        _experts_kernel,
        grid_spec=pltpu.PrefetchScalarGridSpec(
            num_scalar_prefetch=7, grid=(n_tiles,),
            in_specs=[any_space, any_space, any_space, any_space],
            out_specs=pl.BlockSpec((EXPERT_TILE, d), lambda i, *_: (i, 0)),
            scratch_shapes=[pltpu.VMEM((2, EXPERT_TILE, d), F32),
                            pltpu.VMEM((2, d, f), F32), pltpu.VMEM((2, d, f), F32), pltpu.VMEM((2, f, d), F32),
                            pltpu.VMEM((d, f), BF16), pltpu.VMEM((d, f), BF16), pltpu.VMEM((f, d), BF16),
                            pltpu.SMEM((n_rows,), jnp.int32),
                            pltpu.SemaphoreType.DMA((2,)), pltpu.SemaphoreType.DMA((2,))],
        ),
        out_shape=jax.ShapeDtypeStruct((n_rows, d), F32),
        compiler_params=_params(("arbitrary",)),
        name="experts",
    )(dest_flat, pad_rows, tile_expert, tile_first, tile_next, tile_slot, n_used, h2, w_gate, w_up, w_down)


def _combine_kernel(dest_ref, ys_ref, x1_ref, rw_ref, g_ref, yp_ref, ysm_ref, buf_ref, sem, *, n_prompt_tiles):
    i = pl.program_id(0)
    tm = x1_ref.shape[0]

    n_tokens = dest_ref.shape[0] // 2
    n_steps = pl.num_programs(0)

    def copy(tile, r, slot):
        return pltpu.make_async_copy(ys_ref.at[pl.ds(dest_ref[slot * n_tokens + tile * tm + r], 1)],
                                     buf_ref.at[tile % 2, slot, pl.ds(r, 1)], sem.at[tile % 2])

    def gather(tile):
        def start(r, carry):
            copy(tile, r, 0).start(priority=0)
            copy(tile, r, 1).start(priority=1)
            return carry
        lax.fori_loop(0, tm, start, 0, unroll=GATHER_UNROLL)

    @pl.when(i == 0)
    def _():
        gather(0)

    @pl.when(i + 1 < n_steps)
    def _():
        gather(i + 1)

    def wait(r, carry):
        copy(i, r, 0).wait()
        copy(i, r, 1).wait()
        return carry

    lax.fori_loop(0, tm, wait, 0, unroll=GATHER_UNROLL)
    rw = rw_ref[...]
    par = i % 2
    x2 = x1_ref[...] + rw[:, 0:1] * buf_ref[par, 0] + rw[:, 1:2] * buf_ref[par, 1]
    y = _rms(x2, g_ref[...])

    @pl.when(i < n_prompt_tiles)
    def _():
        yp_ref[...] = y

    @pl.when(i >= n_prompt_tiles)
    def _():
        ysm_ref[...] = y


def _combine(dest_flat, ys, x1, rw, norm_g, n_p, n_s):
    n, d = x1.shape
    tm = min(CONV_TILE, n_s)
    npt, nst = n_p // tm, n_s // tm
    row = lambda i, dest: (i, 0)
    return pl.pallas_call(
        functools.partial(_combine_kernel, n_prompt_tiles=npt),
        grid_spec=pltpu.PrefetchScalarGridSpec(
            num_scalar_prefetch=1, grid=(npt + nst,),
            in_specs=[pl.BlockSpec(memory_space=pl.ANY),
                      pl.BlockSpec((tm, d), row), pl.BlockSpec((tm, LANES), row),
                      pl.BlockSpec((1, d), lambda i, dest: (0, 0))],
            out_specs=[pl.BlockSpec((tm, d), lambda i, dest: (jnp.minimum(i, npt - 1), 0)),
                       pl.BlockSpec((tm, d), lambda i, dest: (jnp.maximum(i - npt, 0), 0))],
            scratch_shapes=[pltpu.VMEM((2, 2, tm, d), F32), pltpu.SemaphoreType.DMA((2,))],
        ),
        out_shape=[jax.ShapeDtypeStruct((n_p, d), F32), jax.ShapeDtypeStruct((n_s, d), F32)],
        compiler_params=_params(("arbitrary",)),
        name="combine",
    )(dest_flat, ys, x1, rw, norm_g)


def kernel(x_prompt, x_sample, state_conv, state_hgrn, norm_mix, w_in, w_dw, b_dw, ln_conv_g, ln_conv_b, lb_logits, hgrn_norm_g, w_out, norm_ffn, w_router_group, b_router_group, w_router_expert, b_router_expert, w_exp_gate, w_exp_up, w_exp_down, norm_final):
    assert w_in.shape[0] == 1, "single-layer trunk"
    n_batch, seq, d = x_prompt.shape
    s_batch, steps, _ = x_sample.shape
    ch = w_dw.shape[-1]
    hk = lb_logits.shape[-1]
    heads = hk // HEAD_DIM
    n_experts = w_exp_gate.shape[1]
    assert n_experts == N_GROUPS * EXPERTS_PER_GROUP and n_experts + N_GROUPS <= LANES // 2
    n_p, n_s = n_batch * seq, s_batch * steps
    n = n_p + n_s
    xp = x_prompt.reshape(n_p, d)
    xs = x_sample.reshape(n_s, d)

    u, q, k, lf, v, gs = _inproj(xp, xs, norm_mix, w_in[0], lb_logits, ch, hk)

    c_p = _conv_prompt(u, n_batch, seq, w_dw[0], b_dw, ln_conv_g, ln_conv_b)
    u_t = u[n_p:].reshape(s_batch, steps, ch).transpose(1, 0, 2)
    c_t, new_conv_t = _conv_sample(u_t, state_conv[0].transpose(1, 0, 2), w_dw[0], b_dw, ln_conv_g, ln_conv_b)
    c_s = c_t.transpose(1, 0, 2).reshape(n_s, ch)
    new_conv_sample = new_conv_t.transpose(1, 0, 2)[None]
    hist = state_conv.shape[2]
    new_conv_prompt = jnp.stack([u[(b + 1) * seq - hist:(b + 1) * seq] for b in range(n_batch)])

    o_p, hgrn_p = _hgrn_prompt(q, k, lf, v, gs, hgrn_norm_g, n_batch, seq, heads)
    o_s, hgrn_s = _hgrn_sample(q, k, lf, v, gs, hgrn_norm_g, state_hgrn[0], n_p, steps)

    w_r = jnp.concatenate([w_router_expert[0], w_router_group[0]], axis=1)
    w_r = jnp.pad(w_r, ((0, 0), (0, LANES // 2 - w_r.shape[1])))
    r1 = w_r.astype(BF16)
    r2 = (w_r - r1.astype(F32)).astype(BF16)
    b_r = jnp.pad(jnp.concatenate([b_router_expert[0], b_router_group[0]]), (0, LANES - n_experts - N_GROUPS))[None]
    x1, h2, ri, rw, counts = _outproj(c_p, c_s, o_p, o_s, xp, xs, w_out[0], norm_ffn,
                                      jnp.concatenate([r1, r2], axis=1), b_r, n_experts)

    cnt = counts[0, :n_experts].astype(jnp.int32)
    tiles_per = (cnt + EXPERT_TILE - 1) // EXPERT_TILE
    tile_end = jnp.cumsum(tiles_per)
    row_start = (tile_end - tiles_per) * EXPERT_TILE
    n_tiles = (2 * n) // EXPERT_TILE + n_experts
    is_e = ri[0:2, :, None] == jnp.arange(n_experts, dtype=jnp.int32)
    dest = jnp.sum(jnp.where(is_e, row_start, 0), axis=-1) + ri[2:4]
    dest_flat = dest.reshape(-1)
    n_used = tile_end[-1:]
    tid = jnp.minimum(jnp.arange(n_tiles, dtype=jnp.int32), n_used - 1)
    tile_expert = jnp.sum((tile_end[None, :] <= tid[:, None]).astype(jnp.int32), axis=1)
    prev = jnp.concatenate([jnp.full((1,), -1, jnp.int32), tile_expert[:-1]])
    tile_first = ((tile_expert != prev) & (jnp.arange(n_tiles) < n_used)).astype(jnp.int32)
    eid = jnp.arange(n_experts, dtype=jnp.int32)
    used = tiles_per > 0
    later = used[None, :] & (eid[None, :] > eid[:, None])
    next_expert = jnp.min(jnp.where(later, eid[None, :], n_experts), axis=1)
    next_expert = jnp.where(next_expert == n_experts, -1, next_expert)
    parity = (jnp.cumsum(used.astype(jnp.int32)) - 1) % 2
    tile_is = tile_expert[:, None] == eid[None, :]
    tile_next = jnp.sum(jnp.where(tile_is, next_expert[None, :], 0), axis=1).astype(jnp.int32)
    tile_slot = jnp.sum(jnp.where(tile_is, parity[None, :], 0), axis=1).astype(jnp.int32)

    pad_rows = jnp.concatenate([row_start + cnt, tile_end * EXPERT_TILE]).astype(jnp.int32)
    ys_sorted = _experts(dest_flat, pad_rows, n_tiles * EXPERT_TILE, tile_expert, tile_first, tile_next, tile_slot,
                         n_used.astype(jnp.int32), h2, w_exp_gate[0], w_exp_up[0], w_exp_down[0])
    y_p, y_s = _combine(dest_flat, ys_sorted, x1, rw, norm_final[None], n_p, n_s)

    return (y_p.reshape(n_batch, seq, d), y_s.reshape(s_batch, steps, d),
            new_conv_prompt[None], hgrn_p[None], new_conv_sample, hgrn_s[None])
```

```python
import functools

import numpy as np
import jax
import jax.numpy as jnp
from jax import lax
from jax.experimental import pallas as pl
from jax.experimental.pallas import tpu as pltpu

F32 = jnp.float32
BF16 = jnp.bfloat16
EPS = 1e-6
LANES = 128
SUBLANES = 8
HEAD_DIM = 128
HGRN_CHUNK = 64
HGRN_HEADS_PER_STEP = 2
HGRN_FINISH_GROUP = 16
HGRN_CHUNK_GROUP = 8
FAST_DECAY_LIMIT = -60.0
N_GROUPS = 4
EXPERTS_PER_GROUP = 8
VMEM_LIMIT = 56 * 1024 * 1024
TOKEN_TILE = 256
INPROJ_TILE = 256
WEIGHT_STAGE_COLS = 512
CONV_TILE = 256
CONV_HALO = 32
CONV_ROWS = 64
EXPERT_TILE = 256
SAMPLE_BATCH_BLOCK = 8
SAMPLE_CONV_BATCH_BLOCK = 16
GATHER_UNROLL = 8
MXU_COLS = 256


def _sigmoid(x):
    return 1.0 / (1.0 + jnp.exp(-x))


def _silu(x):
    return x * _sigmoid(x)


def _rms(x, g):
    return x * lax.rsqrt(jnp.mean(x * x, axis=-1, keepdims=True) + EPS) * g


def _split3(x):
    h1 = x.astype(BF16)
    r1 = x - h1.astype(F32)
    h2 = r1.astype(BF16)
    h3 = (r1 - h2.astype(F32)).astype(BF16)
    return h1, h2, h3


def _params(sem, flags=None):
    return pltpu.CompilerParams(dimension_semantics=sem, vmem_limit_bytes=VMEM_LIMIT, flags=flags)


def _load_weight_as_bf16(w_hbm, w_ref, stage_ref, wsem):
    cw = stage_ref.shape[2]
    n_cols = w_ref.shape[1] // cw
    copies = [pltpu.make_async_copy(w_hbm.at[:, pl.ds(c * cw, cw)], stage_ref.at[c % 2], wsem.at[c % 2])
              for c in range(n_cols)]
    copies[0].start()
    for c in range(n_cols):
        if c + 1 < n_cols:
            copies[c + 1].start()
        copies[c].wait()
        w_ref[:, c * cw:(c + 1) * cw] = stage_ref[c % 2].astype(BF16)


def _inproj_kernel(xp_ref, xs_ref, g_ref, w_hbm, lbl_ref,
                   u_ref, q_ref, k_ref, lf_ref, v_ref, gs_ref, h_ref, w_ref, stage_ref, wsem,
                   *, n_prompt_tiles, ch, hk):
    i = pl.program_id(0)
    pl.when(i == 0)(lambda: _load_weight_as_bf16(w_hbm, w_ref, stage_ref, wsem))

    @pl.when(i < n_prompt_tiles)
    def _():
        h_ref[...] = _rms(xp_ref[...], g_ref[...]).astype(BF16)

    @pl.when(i >= n_prompt_tiles)
    def _():
        h_ref[...] = _rms(xs_ref[...], g_ref[...]).astype(BF16)

    h = h_ref[...]

    def proj(c0, width):
        return jnp.dot(h, w_ref[:, c0:c0 + width], preferred_element_type=F32)

    a = proj(0, ch)
    ga = proj(ch, ch)
    u_ref[...] = a * _sigmoid(ga)
    q = proj(2 * ch, hk)
    q_ref[...] = _silu(q).astype(BF16)
    f = proj(2 * ch + hk, hk)
    lbl = lbl_ref[...]
    e = jnp.exp(lbl - jnp.max(lbl, axis=0, keepdims=True))
    lb = e[0:1, :] / jnp.sum(e, axis=0, keepdims=True)
    fg = lb + (1.0 - lb) * _sigmoid(f)
    k_ref[...] = (1.0 - fg).astype(BF16)
    lf_ref[...] = jnp.log(fg)
    g = proj(2 * ch + 3 * hk, hk)
    gs_ref[...] = _silu(g).astype(BF16)
    v_ref[...] = proj(2 * ch + 2 * hk, hk).astype(BF16)


def _inproj(xp, xs, norm_g, w_in, lb_logits, ch, hk):
    n_p, d = xp.shape
    n_s = xs.shape[0]
    tm = min(INPROJ_TILE, n_s)
    assert n_p % tm == 0 and n_s % tm == 0
    npt, nst = n_p // tm, n_s // tm
    n = n_p + n_s
    cols = w_in.shape[1]
    cw = min(WEIGHT_STAGE_COLS, cols)
    assert cols % cw == 0
    row = lambda i: (i, 0)
    const = lambda i: (0, 0)
    outs = [jax.ShapeDtypeStruct((n, ch), F32)] + [
        jax.ShapeDtypeStruct((n, hk), dt) for dt in (BF16, BF16, F32, BF16, BF16)]
    return pl.pallas_call(
        functools.partial(_inproj_kernel, n_prompt_tiles=npt, ch=ch, hk=hk),
        grid=(npt + nst,),
        in_specs=[
            pl.BlockSpec((tm, d), lambda i: (jnp.minimum(i, npt - 1), 0)),
            pl.BlockSpec((tm, d), lambda i: (jnp.maximum(i - npt, 0), 0)),
            pl.BlockSpec((1, d), const),
            pl.BlockSpec(memory_space=pl.ANY),
            pl.BlockSpec(lb_logits.shape, const),
        ],
        out_specs=[pl.BlockSpec((tm, ch), row)] + [pl.BlockSpec((tm, hk), row)] * 5,
        out_shape=outs,
        scratch_shapes=[pltpu.VMEM((tm, d), BF16), pltpu.VMEM((d, cols), BF16),
                        pltpu.VMEM((2, d, cw), F32), pltpu.SemaphoreType.DMA((2,))],
        compiler_params=_params(("arbitrary",)),
        name="inproj",
    )(xp, xs, norm_g, w_in, lb_logits)


def _ln_silu(c, g, b):
    mu = jnp.mean(c, axis=-1, keepdims=True)
    d = c - mu
    var = jnp.mean(d * d, axis=-1, keepdims=True)
    return _silu(d * lax.rsqrt(var + EPS) * g + b)


def _conv_prompt_kernel(halo_ref, cur_ref, w_ref, b_ref, lg_ref, lb_ref, c_ref, ext_ref, acc_ref, *, width):
    t = pl.program_id(1)
    tt, ch = cur_ref.shape
    halo = halo_ref[...]
    ext_ref[0:CONV_HALO, :] = jnp.where(t == 0, jnp.zeros_like(halo), halo)
    ext_ref[CONV_HALO:, :] = cur_ref[...]
    off = CONV_HALO - (width - 1)
    rows = min(CONV_ROWS, tt)
    for l in range(ch // LANES):
        ls = slice(l * LANES, (l + 1) * LANES)
        wl = w_ref[:, ls]
        bl = b_ref[:, ls]
        for r0 in range(0, tt, rows):
            acc = jnp.broadcast_to(bl, (rows, LANES))
            for res in range(SUBLANES):
                extra = SUBLANES if res else 0
                part = None
                for a in range((off + width - 1) // SUBLANES + 1):
                    j = SUBLANES * a + res - off
                    if 0 <= j < width:
                        lo = r0 + SUBLANES * a
                        term = ext_ref[lo:lo + rows + extra, ls] * wl[j:j + 1, :]
                        part = term if part is None else part + term
                acc = acc + part[res:res + rows]
            acc_ref[r0:r0 + rows, ls] = acc
    c_ref[...] = _ln_silu(acc_ref[...], lg_ref[...], lb_ref[...]).astype(BF16)


def _conv_prompt(u, n_batch, seq, w_dw, b_dw, ln_g, ln_b):
    ch = u.shape[1]
    width = w_dw.shape[0]
    tt = min(CONV_TILE, seq)
    assert seq % tt == 0 and tt % CONV_HALO == 0 and width - 1 <= CONV_HALO
    nt = seq // tt
    hb = tt // CONV_HALO
    const = lambda b, t: (0, 0)
    return pl.pallas_call(
        functools.partial(_conv_prompt_kernel, width=width),
        grid=(n_batch, nt),
        in_specs=[
            pl.BlockSpec((CONV_HALO, ch), lambda b, t: (jnp.maximum((b * nt + t) * hb - 1, 0), 0)),
            pl.BlockSpec((tt, ch), lambda b, t: (b * nt + t, 0)),
            pl.BlockSpec((width, ch), const),
            pl.BlockSpec((1, ch), const),
            pl.BlockSpec((1, ch), const),
            pl.BlockSpec((1, ch), const),
        ],
        out_specs=pl.BlockSpec((tt, ch), lambda b, t: (b * nt + t, 0)),
        out_shape=jax.ShapeDtypeStruct((n_batch * seq, ch), BF16),
        scratch_shapes=[pltpu.VMEM((CONV_HALO + tt, ch), F32), pltpu.VMEM((tt, ch), F32)],
        compiler_params=_params(("arbitrary", "arbitrary")),
        name="conv_prompt",
    )(u, u, w_dw, b_dw, ln_g, ln_b)


def _conv_sample_kernel(state_ref, u_ref, w_ref, b_ref, lg_ref, lb_ref, c_ref, new_ref):
    hist = state_ref.shape[0]
    steps = u_ref.shape[0]
    for r in range(hist - steps):
        new_ref[r] = state_ref[r + steps]
    for s in range(steps):
        new_ref[hist - steps + s] = u_ref[s]
    for t in range(steps):
        acc = jnp.broadcast_to(b_ref[...], u_ref.shape[1:])
        for r in range(t, hist):
            acc = acc + state_ref[r] * w_ref[r - t:r - t + 1, :]
        for s in range(t + 1):
            acc = acc + u_ref[s] * w_ref[hist - t + s:hist - t + s + 1, :]
        c_ref[t] = _ln_silu(acc, lg_ref[...], lb_ref[...]).astype(BF16)


def _conv_sample(u_t, state_t, w_dw, b_dw, ln_g, ln_b):
    hist, n_batch, ch = state_t.shape
    width = w_dw.shape[0]
    steps = u_t.shape[0]
    assert hist == width - 1 and steps <= hist
    bb = min(SAMPLE_CONV_BATCH_BLOCK, n_batch)
    assert n_batch % bb == 0
    c2 = lambda i: (0, 0)
    blk = lambda i: (0, i, 0)
    return pl.pallas_call(
        _conv_sample_kernel,
        grid=(n_batch // bb,),
        in_specs=[
            pl.BlockSpec((hist, bb, ch), blk),
            pl.BlockSpec((steps, bb, ch), blk),
            pl.BlockSpec((width, ch), c2),
            pl.BlockSpec((1, ch), c2),
            pl.BlockSpec((1, ch), c2),
            pl.BlockSpec((1, ch), c2),
        ],
        out_specs=[pl.BlockSpec((steps, bb, ch), blk), pl.BlockSpec((hist, bb, ch), blk)],
        out_shape=[jax.ShapeDtypeStruct((steps, n_batch, ch), BF16),
                   jax.ShapeDtypeStruct((hist, n_batch, ch), F32)],
        compiler_params=_params(("arbitrary",)),
        name="conv_sample",
    )(state_t, u_t, w_dw, b_dw, ln_g, ln_b)


def _chunk_constants(c):
    levels = int(np.log2(c))
    assert 2 ** levels == c
    t = np.arange(c)[:, None]
    j = np.arange(c)[None, :]
    mats = [(j <= t)]
    right = np.zeros((levels, c, LANES), np.float32)
    left = np.zeros((levels, c, LANES), np.float32)
    same = np.zeros((levels, c, c), np.float32)
    for lv in range(levels):
        half = c >> (lv + 1)
        split = (t // (2 * half)) * (2 * half) + half
        is_right = t >= split
        mats.append(np.where(is_right, (j >= split) & (j <= t), (j > t) & (j < split)))
        right[lv] = is_right
        left[lv] = ~is_right
        same[lv] = (t // (2 * half)) == (j // (2 * half))
    mats.append(j > t)
    dm = np.concatenate(mats, axis=0).astype(np.float32)
    dm = np.concatenate([dm, dm, dm], axis=1)
    return (jnp.asarray(dm, BF16), jnp.asarray(right), jnp.asarray(left), jnp.asarray(same))


def _hgrn_prompt_kernel(q_ref, k_ref, lf_ref, v_ref, gs_ref, gn_ref, dm_ref, rm_ref, lm_ref, bm_ref,
                        o_ref, s_ref, oi_ref, qe_ref, dec_ref, kv_ref, st_ref, *, chunk):
    seq, width = q_ref.shape
    hp = width // HEAD_DIM
    levels = rm_ref.shape[0]
    n_chunks = seq // chunk
    nt = (((1,), (1,)), ((), ()))
    tn = (((0,), (0,)), ((), ()))
    row_i = lax.broadcasted_iota(jnp.int32, (chunk, chunk), 0)
    col_i = lax.broadcasted_iota(jnp.int32, (chunk, chunk), 1)

    def intra(chains, robust):
        n = len(chains)
        hs = [slice(h * HEAD_DIM, (h + 1) * HEAD_DIM) for _, h in chains]
        rows = [pl.ds(pl.multiple_of(c * chunk, chunk), chunk) for c, _ in chains]
        q = [q_ref[rows[i], hs[i]].astype(F32) for i in range(n)]
        k = [k_ref[rows[i], hs[i]].astype(F32) for i in range(n)]
        v = [v_ref[rows[i], hs[i]] for i in range(n)]
        lsp = [jnp.concatenate(_split3(lf_ref[rows[i], hs[i]]), axis=0) for i in range(n)]

        def decay_sums(i, dm):
            return jnp.dot(dm, lsp[i], preferred_element_type=F32)

        if robust:
            ex = [jnp.exp(decay_sums(i, dm_ref[...])) for i in range(n)]
            e_cum = [e[0:chunk] for e in ex]
            e_tail = [e[(levels + 1) * chunk:(levels + 2) * chunk] for e in ex]
            att = [jnp.where(row_i == col_i, jnp.sum(q[i] * k[i], axis=-1, keepdims=True), 0.0) for i in range(n)]
            for lv in range(levels):
                for i in range(n):
                    e = ex[i][(lv + 1) * chunk:(lv + 2) * chunk]
                    ql = (q[i] * e * rm_ref[lv]).astype(BF16)
                    kl = (k[i] * e * lm_ref[lv]).astype(BF16)
                    att[i] = att[i] + bm_ref[lv] * lax.dot_general(ql, kl, nt, preferred_element_type=F32)
            qe = [(q[i] * e_cum[i]).astype(BF16) for i in range(n)]
        else:
            sums = [decay_sums(i, dm_ref[0:chunk, :]) for i in range(n)]
            tails = [s[chunk - 1:chunk, :] - s for s in sums]
            e_cum = [jnp.exp(s) for s in sums]
            e_tail = [jnp.exp(t) for t in tails]
            qe = [(q[i] * e_cum[i]).astype(BF16) for i in range(n)]
            kn = [(k[i] * jnp.exp(-sums[i])).astype(BF16) for i in range(n)]
            att = [lax.dot_general(qe[i], kn[i], nt, preferred_element_type=F32) for i in range(n)]
            att = [jnp.where(row_i >= col_i, a, 0.0) for a in att]
        kv = [lax.dot_general(v[i], (k[i] * e_tail[i]).astype(BF16), tn, preferred_element_type=F32) for i in range(n)]
        oi = [jnp.dot(att[i].astype(BF16), v[i], preferred_element_type=F32) for i in range(n)]
        for i, (c, h) in enumerate(chains):
            kv_ref[c, h] = kv[i]
            oi_ref[rows[i], hs[i]] = oi[i]
            qe_ref[rows[i], hs[i]] = qe[i]
            dec_ref[c, :, hs[i]] = e_cum[i][chunk - 1:chunk, :]

    def finish(chains):
        n = len(chains)
        hs = [slice(h * HEAD_DIM, (h + 1) * HEAD_DIM) for _, h in chains]
        rows = [pl.ds(pl.multiple_of(c * chunk, chunk), chunk) for c, _ in chains]
        inter = [lax.dot_general(qe_ref[rows[i], hs[i]], st_ref[c, h], nt, preferred_element_type=F32)
                 for i, (c, h) in enumerate(chains)]
        o = [oi_ref[rows[i], hs[i]] + inter[i] for i in range(n)]
        o = [_rms(o[i], gn_ref[...]) * gs_ref[rows[i], hs[i]].astype(F32) for i in range(n)]
        for i in range(n):
            o_ref[rows[i], hs[i]] = o[i].astype(BF16)

    def run(robust):
        group = min(HGRN_CHUNK_GROUP, n_chunks)

        def intra_body(g, carry):
            intra([(g * group + u, h) for u in range(group) for h in range(hp)], robust)
            return carry
        lax.fori_loop(0, n_chunks // group, intra_body, 0)

        def scan_body(c, sts):
            new = []
            for h in range(hp):
                st_ref[c, h] = sts[h].astype(BF16)
                new.append(sts[h] * dec_ref[c, :, h * HEAD_DIM:(h + 1) * HEAD_DIM] + kv_ref[c, h])
            return tuple(new)
        sts = lax.fori_loop(0, n_chunks, scan_body, tuple(jnp.zeros((HEAD_DIM, HEAD_DIM), F32) for _ in range(hp)))
        for h in range(hp):
            s_ref[0, h] = sts[h].T

        fgroup = min(HGRN_FINISH_GROUP, n_chunks)

        def finish_body(g, carry):
            finish([(g * fgroup + u, h) for u in range(fgroup) for h in range(hp)])
            return carry
        lax.fori_loop(0, n_chunks // fgroup, finish_body, 0)

    lf_all = lf_ref[...].reshape(n_chunks, chunk, width)
    slowest = jnp.min(jnp.sum(lf_all, axis=1))
    fast = slowest >= FAST_DECAY_LIMIT
    pl.when(fast)(lambda: run(False))
    pl.when(jnp.logical_not(fast))(lambda: run(True))


def _hgrn_prompt(q, k, lf, v, gs, g_norm, n_batch, seq, heads):
    chunk = HGRN_CHUNK if seq % HGRN_CHUNK == 0 else seq
    dm, rm, lm, bm = _chunk_constants(chunk)
    hp = min(HGRN_HEADS_PER_STEP, heads)
    assert heads % hp == 0
    n_chunks = seq // chunk
    tok = pl.BlockSpec((seq, hp * HEAD_DIM), lambda b, h: (b, h))
    c2 = lambda b, h: (0, 0)
    c3 = lambda b, h: (0, 0, 0)
    return pl.pallas_call(
        functools.partial(_hgrn_prompt_kernel, chunk=chunk),
        grid=(n_batch, heads // hp),
        in_specs=[tok, tok, tok, tok, tok,
                  pl.BlockSpec((1, HEAD_DIM), c2),
                  pl.BlockSpec(dm.shape, c2),
                  pl.BlockSpec(rm.shape, c3), pl.BlockSpec(lm.shape, c3), pl.BlockSpec(bm.shape, c3)],
        out_specs=[tok, pl.BlockSpec((1, hp, HEAD_DIM, HEAD_DIM), lambda b, h: (b, h, 0, 0))],
        out_shape=[jax.ShapeDtypeStruct((n_batch * seq, heads * HEAD_DIM), BF16),
                   jax.ShapeDtypeStruct((n_batch, heads, HEAD_DIM, HEAD_DIM), F32)],
        scratch_shapes=[pltpu.VMEM((seq, hp * HEAD_DIM), F32),
                        pltpu.VMEM((seq, hp * HEAD_DIM), BF16),
                        pltpu.VMEM((n_chunks, 1, hp * HEAD_DIM), F32),
                        pltpu.VMEM((n_chunks, hp, HEAD_DIM, HEAD_DIM), F32),
                        pltpu.VMEM((n_chunks, hp, HEAD_DIM, HEAD_DIM), BF16)],
        compiler_params=_params(("arbitrary", "arbitrary")),
        name="hgrn_prompt",
    )(q, k, lf, v, gs, g_norm, dm, rm, lm, bm)


def _hgrn_sample_kernel(q_ref, k_ref, lf_ref, v_ref, gs_ref, gn_ref, s0_ref, o_ref, s_ref, inter_ref, *, steps):
    rows, width = q_ref.shape
    heads = width // HEAD_DIM
    q = q_ref[...].astype(F32)
    k = k_ref[...].astype(F32)
    v = v_ref[...].astype(F32)
    lf = lf_ref[...]
    step = lax.broadcasted_iota(jnp.int32, (rows, 1), 0) & (steps - 1)

    def back(x, d):
        return pltpu.roll(x, d, 0)

    cum = lf
    for d in range(1, steps):
        cum = cum + jnp.where(step >= d, back(lf, d), 0.0)
    tail = jnp.zeros_like(lf)
    for d in range(1, steps):
        tail = tail + jnp.where(step + d < steps, pltpu.roll(lf, rows - d, 0), 0.0)

    def head_sum(x):
        return [jnp.sum(x[:, h * HEAD_DIM:(h + 1) * HEAD_DIM], axis=-1, keepdims=True) for h in range(heads)]

    def head_scale(cols, x):
        return jnp.concatenate([cols[h] * x[:, h * HEAD_DIM:(h + 1) * HEAD_DIM] for h in range(heads)], axis=1)

    intra = head_scale(head_sum(q * k), v)
    for d in range(1, steps):
        ok = step >= d
        rel = jnp.where(ok, cum - back(cum, d), 0.0)
        w = jnp.where(ok, q * back(k, d) * jnp.exp(rel), 0.0)
        intra = intra + head_scale(head_sum(w), back(v, d))

    qe = (q * jnp.exp(cum)).astype(BF16)
    kd = k * jnp.exp(tail)
    total = jnp.exp(cum)
    per = SUBLANES // steps
    grp = lax.broadcasted_iota(jnp.int32, (SUBLANES, 1), 0)
    tn = (((0,), (0,)), ((), ()))
    for b in range(rows // steps):
        r8 = (b // per) * SUBLANES
        lo = (b % per) * steps
        mine = (grp >= lo) & (grp < lo + steps)
        spare = (lo + steps) % SUBLANES
        for h in range(heads):
            hs = slice(h * HEAD_DIM, (h + 1) * HEAD_DIM)
            s0 = s0_ref[b, h]
            res = jnp.dot(qe[r8:r8 + SUBLANES, hs], s0.astype(BF16), preferred_element_type=F32)
            inter_ref[b * steps:(b + 1) * steps, hs] = res[lo:lo + steps]
            d1, d2, d3 = _split3(total[r8 + lo + steps - 1:r8 + lo + steps, hs])
            dec = jnp.where(grp == spare, d1.astype(F32),
                            jnp.where(grp == spare + 1, d2.astype(F32),
                                      jnp.where(grp == spare + 2, d3.astype(F32), 0.0)))
            lhs = jnp.where(mine, kd[r8:r8 + SUBLANES, hs], dec).astype(BF16)
            vb = jnp.where(mine, v[r8:r8 + SUBLANES, hs], 0.0)
            ones = jnp.where(mine, 0.0, 1.0) * jnp.ones((SUBLANES, HEAD_DIM), F32)
            rhs = jnp.concatenate([vb, ones], axis=1).astype(BF16)
            upd = lax.dot_general(lhs, rhs, tn, preferred_element_type=F32)
            s_ref[b, h] = upd[:, HEAD_DIM:] * s0 + upd[:, :HEAD_DIM]
    o = intra + inter_ref[...]
    gn = gn_ref[...]
    o = jnp.concatenate([_rms(o[:, h * HEAD_DIM:(h + 1) * HEAD_DIM], gn) for h in range(heads)], axis=1)
    o_ref[...] = (o * gs_ref[...].astype(F32)).astype(BF16)


def _hgrn_sample(q, k, lf, v, gs, g_norm, s0, row0, steps):
    n_batch, heads = s0.shape[:2]
    width = heads * HEAD_DIM
    assert steps & (steps - 1) == 0 and SUBLANES - steps >= 3
    bb = min(SAMPLE_BATCH_BLOCK, n_batch)
    rows = bb * steps
    assert n_batch % bb == 0 and rows % SUBLANES == 0 and row0 % rows == 0
    blk0 = row0 // rows
    tok = pl.BlockSpec((rows, width), lambda i: (blk0 + i, 0))
    st = pl.BlockSpec((bb, heads, HEAD_DIM, HEAD_DIM), lambda i: (i, 0, 0, 0))
    return pl.pallas_call(
        functools.partial(_hgrn_sample_kernel, steps=steps),
        grid=(n_batch // bb,),
        in_specs=[tok, tok, tok, tok, tok, pl.BlockSpec((1, HEAD_DIM), lambda i: (0, 0)), st],
        out_specs=[pl.BlockSpec((rows, width), lambda i: (i, 0)), st],
        out_shape=[jax.ShapeDtypeStruct((n_batch * steps, width), BF16),
                   jax.ShapeDtypeStruct(s0.shape, F32)],
        scratch_shapes=[pltpu.VMEM((rows, width), F32)],
        compiler_params=_params(("arbitrary",)),
        name="hgrn_sample",
    )(q, k, lf, v, gs, g_norm, s0)


def _outproj_kernel(cp_ref, cs_ref, op_ref, os_ref, xp_ref, xs_ref, w_hbm, g_ref, wr_ref, br_ref, tri_ref,
                    x1_ref, h2_ref, ri_ref, rw_ref, cnt_ref, run_ref, w_ref, stage_ref, wsem,
                    *, n_prompt_tiles, n_experts):
    i = pl.program_id(0)
    is_p = i < n_prompt_tiles
    ch = cp_ref.shape[1]

    @pl.when(i == 0)
    def _():
        run_ref[...] = jnp.zeros_like(run_ref)
        _load_weight_as_bf16(w_hbm, w_ref, stage_ref, wsem)

    def mix(c_ref, o_ref, x_ref):
        y = jnp.dot(c_ref[...], w_ref[0:ch, :], preferred_element_type=F32)
        y = y + jnp.dot(o_ref[...], w_ref[ch:, :], preferred_element_type=F32)
        x1_ref[...] = x_ref[...] + y

    pl.when(is_p)(lambda: mix(cp_ref, op_ref, xp_ref))
    pl.when(jnp.logical_not(is_p))(lambda: mix(cs_ref, os_ref, xs_ref))

    h2 = _rms(x1_ref[...], g_ref[...])
    h2_ref[...] = h2
    a1, a2, _ = _split3(h2)
    p1 = jnp.dot(a1, wr_ref[...], preferred_element_type=F32)
    p2 = jnp.dot(a2, wr_ref[...], preferred_element_type=F32)
    logits = p1 + pltpu.roll(p1, LANES // 2, 1) + p2 + br_ref[...]
    tm = logits.shape[0]
    lane = lax.broadcasted_iota(jnp.int32, (tm, LANES), 1)
    lane_f = lane.astype(F32)
    neg = jnp.float32(-jnp.inf)

    def top(x):
        m = jnp.max(x, axis=-1, keepdims=True)
        idx = jnp.min(jnp.where(x == m, lane_f, float(LANES)), axis=-1, keepdims=True)
        return m, idx.astype(jnp.int32)

    is_group = (lane >= n_experts) & (lane < n_experts + N_GROUPS)
    gl = jnp.where(is_group, logits, neg)
    gmax, gidx = top(gl)
    p_top = 1.0 / jnp.sum(jnp.exp(gl - gmax), axis=-1, keepdims=True)
    g_lo = (gidx - n_experts) * EXPERTS_PER_GROUP
    el = jnp.where((lane >= g_lo) & (lane < g_lo + EXPERTS_PER_GROUP), logits, neg)
    v1, e1 = top(el)
    v2, e2 = top(jnp.where(lane == e1, neg, el))
    t = jnp.exp(v2 - v1)
    w1 = p_top / (1.0 + t)
    w2 = p_top * t / (1.0 + t)
    hot = ((lane == e1) | (lane == e2)).astype(F32)
    before = run_ref[...] + jnp.dot(tri_ref[...], hot.astype(BF16), preferred_element_type=F32)
    r1 = jnp.sum(jnp.where(lane == e1, before, 0.0), axis=-1, keepdims=True)
    r2 = jnp.sum(jnp.where(lane == e2, before, 0.0), axis=-1, keepdims=True)
    run_ref[...] = run_ref[...] + jnp.sum(hot, axis=0, keepdims=True)
    cnt_ref[...] = run_ref[...]
    info = jnp.where(lane == 0, e1.astype(F32), jnp.where(lane == 1, e2.astype(F32),
                     jnp.where(lane == 2, r1, jnp.where(lane == 3, r2, 0.0))))
    ri_ref[...] = info.T[0:SUBLANES, :].astype(jnp.int32)
    rw_ref[...] = jnp.where(lane == 0, w1, jnp.where(lane == 1, w2, 0.0))


def _outproj(c_p, c_s, o_p, o_s, xp, xs, w_out, norm_g, w_router3, b_router, n_experts):
    n_p, d = xp.shape
    n_s = xs.shape[0]
    tm = min(TOKEN_TILE, n_s)
    npt, nst = n_p // tm, n_s // tm
    n = n_p + n_s
    ch = c_p.shape[1]
    hv = o_p.shape[1]
    cw = min(WEIGHT_STAGE_COLS, d)
    assert w_out.shape == (ch + hv, d) and d % cw == 0
    tri = jnp.asarray(np.tril(np.ones((tm, tm), np.float32), -1), BF16)
    pidx = lambda i: (jnp.minimum(i, npt - 1), 0)
    sidx = lambda i: (jnp.maximum(i - npt, 0), 0)
    row = lambda i: (i, 0)
    c2 = lambda i: (0, 0)
    return pl.pallas_call(
        functools.partial(_outproj_kernel, n_prompt_tiles=npt, n_experts=n_experts),
        grid=(npt + nst,),
        in_specs=[
            pl.BlockSpec((tm, ch), pidx), pl.BlockSpec((tm, ch), sidx),
            pl.BlockSpec((tm, hv), pidx), pl.BlockSpec((tm, hv), sidx),
            pl.BlockSpec((tm, d), pidx), pl.BlockSpec((tm, d), sidx),
            pl.BlockSpec(memory_space=pl.ANY),
            pl.BlockSpec((1, d), c2),
            pl.BlockSpec((d, LANES), c2),
            pl.BlockSpec((1, LANES), c2),
            pl.BlockSpec((tm, tm), c2),
        ],
        out_specs=[pl.BlockSpec((tm, d), row), pl.BlockSpec((tm, d), row),
                   pl.BlockSpec((SUBLANES, tm), lambda i: (0, i)), pl.BlockSpec((tm, LANES), row),
                   pl.BlockSpec((1, LANES), c2)],
        out_shape=[jax.ShapeDtypeStruct((n, d), F32), jax.ShapeDtypeStruct((n, d), F32),
                   jax.ShapeDtypeStruct((SUBLANES, n), jnp.int32), jax.ShapeDtypeStruct((n, LANES), F32),
                   jax.ShapeDtypeStruct((1, LANES), F32)],
        scratch_shapes=[pltpu.VMEM((1, LANES), F32), pltpu.VMEM((ch + hv, d), BF16),
                        pltpu.VMEM((2, ch + hv, cw), F32), pltpu.SemaphoreType.DMA((2,))],
        compiler_params=_params(("arbitrary",)),
        name="outproj",
    )(c_p, c_s, o_p, o_s, xp, xs, w_out, norm_g, w_router3, b_router, tri)


def _experts_kernel(dest_ref, pad_ref, te_ref, tf_ref, tn_ref, ts_ref, nu_ref, h2_ref, wg_ref, wu_ref, wd_ref, y_ref,
                    x_ref, wgf_ref, wuf_ref, wdf_ref, wgb_ref, wub_ref, wdb_ref, src_ref, sem, wsem):
    i = pl.program_id(0)
    n_used = nu_ref[0]
    rows = x_ref.shape[1]

    def invert_routing():
        n_tokens = dest_ref.shape[0] // 2
        n_exp = pad_ref.shape[0] // 2
        spread = (1 << (n_tokens.bit_length() - 1)) - 1

        def fill_expert(e, carry):
            def fill(p, c):
                src_ref[p] = p & spread
                return c
            lax.fori_loop(pad_ref[e], pad_ref[n_exp + e], fill, 0)
            return carry
        lax.fori_loop(0, n_exp, fill_expert, 0)

        def place(g, carry):
            for u in range(GATHER_UNROLL):
                t = g * GATHER_UNROLL + u
                src_ref[dest_ref[t]] = t
                src_ref[dest_ref[n_tokens + t]] = t
            return carry
        lax.fori_loop(0, n_tokens // GATHER_UNROLL, place, 0)

    def row_copy(tile, slot, r):
        return pltpu.make_async_copy(h2_ref.at[pl.ds(src_ref[tile * rows + r], 1)],
                                     x_ref.at[slot, pl.ds(r, 1)], sem.at[slot])

    def gather(tile, slot):
        def start(g, carry):
            for u in range(GATHER_UNROLL):
                row_copy(tile, slot, g * GATHER_UNROLL + u).start()
            return carry
        lax.fori_loop(0, rows // GATHER_UNROLL, start, 0)

    def weight_copies(e, slot):
        return [pltpu.make_async_copy(src.at[e], dst.at[slot], wsem.at[slot])
                for src, dst in ((wg_ref, wgf_ref), (wu_ref, wuf_ref), (wd_ref, wdf_ref))]

    @pl.when(i == 0)
    def _():
        for c in weight_copies(te_ref[0], 0):
            c.start(priority=1)
        invert_routing()
        gather(0, 0)

    @pl.when(tf_ref[i] == 1)
    def _():
        wslot = ts_ref[i]

        @pl.when(tn_ref[i] >= 0)
        def _():
            for c in weight_copies(tn_ref[i], 1 - wslot):
                c.start(priority=1)
        for c in weight_copies(te_ref[i], wslot):
            c.wait()
        wgb_ref[...] = wgf_ref[wslot].astype(BF16)
        wub_ref[...] = wuf_ref[wslot].astype(BF16)
        wdb_ref[...] = wdf_ref[wslot].astype(BF16)

    def tile_step(prefetch):
        slot = i % 2

        def wait(g, carry):
            for u in range(GATHER_UNROLL):
                row_copy(i, slot, g * GATHER_UNROLL + u).wait()
            return carry
        lax.fori_loop(0, rows // GATHER_UNROLL, wait, 0)

        f = wgb_ref.shape[1]
        d = wdb_ref.shape[1]
        fc, dc = min(MXU_COLS, f), min(MXU_COLS, d)
        n_pieces = 2 * (f // fc) + d // dc
        per_piece = -(-rows // n_pieces)
        issued = [0]

        def issue_share():
            if not prefetch:
                return
            for r in range(issued[0], min(issued[0] + per_piece, rows)):
                row_copy(i + 1, 1 - slot, r).start()
            issued[0] = min(issued[0] + per_piece, rows)

        x = x_ref[slot].astype(BF16)
        hid = []
        for j in range(f // fc):
            cs = slice(j * fc, (j + 1) * fc)
            hg = jnp.dot(x, wgb_ref[:, cs], preferred_element_type=F32)
            issue_share()
            hu = jnp.dot(x, wub_ref[:, cs], preferred_element_type=F32)
            issue_share()
            hid.append((_silu(hg) * hu).astype(BF16))
        hid = jnp.concatenate(hid, axis=1)
        for j in range(d // dc):
            cs = slice(j * dc, (j + 1) * dc)
            y_ref[:, cs] = jnp.dot(hid, wdb_ref[:, cs], preferred_element_type=F32)
            issue_share()

    pl.when(i + 1 < n_used)(lambda: tile_step(True))
    pl.when(i + 1 == n_used)(lambda: tile_step(False))

    @pl.when(i >= n_used)
    def _():
        y_ref[...] = jnp.zeros_like(y_ref)


def _experts(dest_flat, pad_rows, n_rows, tile_expert, tile_first, tile_next, tile_slot, n_used, h2,
             w_gate, w_up, w_down):
    d = h2.shape[1]
    f = w_gate.shape[2]
    n_tiles = n_rows // EXPERT_TILE
    any_space = pl.BlockSpec(memory_space=pl.ANY)
    return pl.pallas_call(
        _experts_kernel,
        grid_spec=pltpu.PrefetchScalarGridSpec(
            num_scalar_prefetch=7, grid=(n_tiles,),
            in_specs=[any_space, any_space, any_space, any_space],
            out_specs=pl.BlockSpec((EXPERT_TILE, d), lambda i, *_: (i, 0)),
            scratch_shapes=[pltpu.VMEM((2, EXPERT_TILE, d), F32),
                            pltpu.VMEM((2, d, f), F32), pltpu.VMEM((2, d, f), F32), pltpu.VMEM((2, f, d), F32),
                            pltpu.VMEM((d, f), BF16), pltpu.VMEM((d, f), BF16), pltpu.VMEM((f, d), BF16),
                            pltpu.SMEM((n_rows,), jnp.int32),
                            pltpu.SemaphoreType.DMA((2,)), pltpu.SemaphoreType.DMA((2,))],
        ),
        out_shape=jax.ShapeDtypeStruct((n_rows, d), F32),
        compiler_params=_params(("arbitrary",)),
        name="experts",
    )(dest_flat, pad_rows, tile_expert, tile_first, tile_next, tile_slot, n_used, h2, w_gate, w_up, w_down)


def _combine_kernel(dest_ref, ys_ref, x1_ref, rw_ref, g_ref, yp_ref, ysm_ref, buf_ref, sem, *, n_prompt_tiles):
    i = pl.program_id(0)
    tm = x1_ref.shape[0]

    n_tokens = dest_ref.shape[0] // 2
    n_steps = pl.num_programs(0)

    def copy(tile, r, slot):
        return pltpu.make_async_copy(ys_ref.at[pl.ds(dest_ref[slot * n_tokens + tile * tm + r], 1)],
                                     buf_ref.at[tile % 2, slot, pl.ds(r, 1)], sem.at[tile % 2])

    def gather(tile):
        def start(r, carry):
            copy(tile, r, 0).start(priority=0)
            copy(tile, r, 1).start(priority=1)
            return carry
        lax.fori_loop(0, tm, start, 0, unroll=GATHER_UNROLL)

    @pl.when(i == 0)
    def _():
        gather(0)

    @pl.when(i + 1 < n_steps)
    def _():
        gather(i + 1)

    def wait(r, carry):
        copy(i, r, 0).wait()
        copy(i, r, 1).wait()
        return carry

    lax.fori_loop(0, tm, wait, 0, unroll=GATHER_UNROLL)
    rw = rw_ref[...]
    par = i % 2
    x2 = x1_ref[...] + rw[:, 0:1] * buf_ref[par, 0] + rw[:, 1:2] * buf_ref[par, 1]
    y = _rms(x2, g_ref[...])

    @pl.when(i < n_prompt_tiles)
    def _():
        yp_ref[...] = y

    @pl.when(i >= n_prompt_tiles)
    def _():
        ysm_ref[...] = y


def _combine(dest_flat, ys, x1, rw, norm_g, n_p, n_s):
    n, d = x1.shape
    tm = min(CONV_TILE, n_s)
    npt, nst = n_p // tm, n_s // tm
    row = lambda i, dest: (i, 0)
    return pl.pallas_call(
        functools.partial(_combine_kernel, n_prompt_tiles=npt),
        grid_spec=pltpu.PrefetchScalarGridSpec(
            num_scalar_prefetch=1, grid=(npt + nst,),
            in_specs=[pl.BlockSpec(memory_space=pl.ANY),
                      pl.BlockSpec((tm, d), row), pl.BlockSpec((tm, LANES), row),
                      pl.BlockSpec((1, d), lambda i, dest: (0, 0))],
            out_specs=[pl.BlockSpec((tm, d), lambda i, dest: (jnp.minimum(i, npt - 1), 0)),
                       pl.BlockSpec((tm, d), lambda i, dest: (jnp.maximum(i - npt, 0), 0))],
            scratch_shapes=[pltpu.VMEM((2, 2, tm, d), F32), pltpu.SemaphoreType.DMA((2,))],
        ),
        out_shape=[jax.ShapeDtypeStruct((n_p, d), F32), jax.ShapeDtypeStruct((n_s, d), F32)],
        compiler_params=_params(("arbitrary",)),
        name="combine",
    )(dest_flat, ys, x1, rw, norm_g)


def kernel(x_prompt, x_sample, state_conv, state_hgrn, norm_mix, w_in, w_dw, b_dw, ln_conv_g, ln_conv_b, lb_logits, hgrn_norm_g, w_out, norm_ffn, w_router_group, b_router_group, w_router_expert, b_router_expert, w_exp_gate, w_exp_up, w_exp_down, norm_final):
    assert w_in.shape[0] == 1, "single-layer trunk"
    n_batch, seq, d = x_prompt.shape
    s_batch, steps, _ = x_sample.shape
    ch = w_dw.shape[-1]
    hk = lb_logits.shape[-1]
    heads = hk // HEAD_DIM
    n_experts = w_exp_gate.shape[1]
    assert n_experts == N_GROUPS * EXPERTS_PER_GROUP and n_experts + N_GROUPS <= LANES // 2
    n_p, n_s = n_batch * seq, s_batch * steps
    n = n_p + n_s
    xp = x_prompt.reshape(n_p, d)
    xs = x_sample.reshape(n_s, d)

    u, q, k, lf, v, gs = _inproj(xp, xs, norm_mix, w_in[0], lb_logits, ch, hk)

    c_p = _conv_prompt(u, n_batch, seq, w_dw[0], b_dw, ln_conv_g, ln_conv_b)
    u_t = u[n_p:].reshape(s_batch, steps, ch).transpose(1, 0, 2)
    c_t, new_conv_t = _conv_sample(u_t, state_conv[0].transpose(1, 0, 2), w_dw[0], b_dw, ln_conv_g, ln_conv_b)
    c_s = c_t.transpose(1, 0, 2).reshape(n_s, ch)
    new_conv_sample = new_conv_t.transpose(1, 0, 2)[None]
    hist = state_conv.shape[2]
    new_conv_prompt = jnp.stack([u[(b + 1) * seq - hist:(b + 1) * seq] for b in range(n_batch)])

    o_p, hgrn_p = _hgrn_prompt(q, k, lf, v, gs, hgrn_norm_g, n_batch, seq, heads)
    o_s, hgrn_s = _hgrn_sample(q, k, lf, v, gs, hgrn_norm_g, state_hgrn[0], n_p, steps)

    w_r = jnp.concatenate([w_router_expert[0], w_router_group[0]], axis=1)
    w_r = jnp.pad(w_r, ((0, 0), (0, LANES // 2 - w_r.shape[1])))
    r1 = w_r.astype(BF16)
    r2 = (w_r - r1.astype(F32)).astype(BF16)
    b_r = jnp.pad(jnp.concatenate([b_router_expert[0], b_router_group[0]]), (0, LANES - n_experts - N_GROUPS))[None]
    x1, h2, ri, rw, counts = _outproj(c_p, c_s, o_p, o_s, xp, xs, w_out[0], norm_ffn,
                                      jnp.concatenate([r1, r2], axis=1), b_r, n_experts)

    cnt = counts[0, :n_experts].astype(jnp.int32)
    tiles_per = (cnt + EXPERT_TILE - 1) // EXPERT_TILE
    tile_end = jnp.cumsum(tiles_per)
    row_start = (tile_end - tiles_per) * EXPERT_TILE
    n_tiles = (2 * n) // EXPERT_TILE + n_experts
    is_e = ri[0:2, :, None] == jnp.arange(n_experts, dtype=jnp.int32)
    dest = jnp.sum(jnp.where(is_e, row_start, 0), axis=-1) + ri[2:4]
    dest_flat = dest.reshape(-1)
    n_used = tile_end[-1:]
    tid = jnp.minimum(jnp.arange(n_tiles, dtype=jnp.int32), n_used - 1)
    tile_expert = jnp.sum((tile_end[None, :] <= tid[:, None]).astype(jnp.int32), axis=1)
    prev = jnp.concatenate([jnp.full((1,), -1, jnp.int32), tile_expert[:-1]])
    tile_first = ((tile_expert != prev) & (jnp.arange(n_tiles) < n_used)).astype(jnp.int32)
    eid = jnp.arange(n_experts, dtype=jnp.int32)
    used = tiles_per > 0
    later = used[None, :] & (eid[None, :] > eid[:, None])
    next_expert = jnp.min(jnp.where(later, eid[None, :], n_experts), axis=1)
    next_expert = jnp.where(next_expert == n_experts, -1, next_expert)
    parity = (jnp.cumsum(used.astype(jnp.int32)) - 1) % 2
    tile_is = tile_expert[:, None] == eid[None, :]
    tile_next = jnp.sum(jnp.where(tile_is, next_expert[None, :], 0), axis=1).astype(jnp.int32)
    tile_slot = jnp.sum(jnp.where(tile_is, parity[None, :], 0), axis=1).astype(jnp.int32)

    pad_rows = jnp.concatenate([row_start + cnt, tile_end * EXPERT_TILE]).astype(jnp.int32)
    ys_sorted = _experts(dest_flat, pad_rows, n_tiles * EXPERT_TILE, tile_expert, tile_first, tile_next, tile_slot,
                         n_used.astype(jnp.int32), h2, w_exp_gate[0], w_exp_up[0], w_exp_down[0])
    y_p, y_s = _combine(dest_flat, ys_sorted, x1, rw, norm_final[None], n_p, n_s)

    return (y_p.reshape(n_batch, seq, d), y_s.reshape(s_batch, steps, d),
            new_conv_prompt[None], hgrn_p[None], new_conv_sample, hgrn_s[None])
```

```python
import functools

import numpy as np
import jax
import jax.numpy as jnp
from jax import lax
from jax.experimental import pallas as pl
from jax.experimental.pallas import tpu as pltpu

F32 = jnp.float32
BF16 = jnp.bfloat16
EPS = 1e-6
LANES = 128
SUBLANES = 8
HEAD_DIM = 128
HGRN_CHUNK = 64
HGRN_HEADS_PER_STEP = 2
HGRN_FINISH_GROUP = 16
HGRN_CHUNK_GROUP = 8
FAST_DECAY_LIMIT = -60.0
N_GROUPS = 4
EXPERTS_PER_GROUP = 8
VMEM_LIMIT = 56 * 1024 * 1024
TOKEN_TILE = 256
INPROJ_TILE = 256
WEIGHT_STAGE_COLS = 512
CONV_TILE = 256
CONV_HALO = 32
CONV_ROWS = 64
EXPERT_TILE = 256
SAMPLE_BATCH_BLOCK = 8
SAMPLE_CONV_BATCH_BLOCK = 16
GATHER_UNROLL = 8
MXU_COLS = 256


def _sigmoid(x):
    return 1.0 / (1.0 + jnp.exp(-x))


def _silu(x):
    return x * _sigmoid(x)


def _rms(x, g):
    return x * lax.rsqrt(jnp.mean(x * x, axis=-1, keepdims=True) + EPS) * g


def _split3(x):
    h1 = x.astype(BF16)
    r1 = x - h1.astype(F32)
    h2 = r1.astype(BF16)
    h3 = (r1 - h2.astype(F32)).astype(BF16)
    return h1, h2, h3


def _params(sem, flags=None):
    return pltpu.CompilerParams(dimension_semantics=sem, vmem_limit_bytes=VMEM_LIMIT, flags=flags)


def _load_weight_as_bf16(w_hbm, w_ref, stage_ref, wsem):
    cw = stage_ref.shape[2]
    n_cols = w_ref.shape[1] // cw
    copies = [pltpu.make_async_copy(w_hbm.at[:, pl.ds(c * cw, cw)], stage_ref.at[c % 2], wsem.at[c % 2])
              for c in range(n_cols)]
    copies[0].start()
    for c in range(n_cols):
        if c + 1 < n_cols:
            copies[c + 1].start()
        copies[c].wait()
        w_ref[:, c * cw:(c + 1) * cw] = stage_ref[c % 2].astype(BF16)


def _inproj_kernel(xp_ref, xs_ref, g_ref, w_hbm, lbl_ref,
                   u_ref, q_ref, k_ref, lf_ref, v_ref, gs_ref, h_ref, w_ref, stage_ref, wsem,
                   *, n_prompt_tiles, ch, hk):
    i = pl.program_id(0)
    pl.when(i == 0)(lambda: _load_weight_as_bf16(w_hbm, w_ref, stage_ref, wsem))

    @pl.when(i < n_prompt_tiles)
    def _():
        h_ref[...] = _rms(xp_ref[...], g_ref[...]).astype(BF16)

    @pl.when(i >= n_prompt_tiles)
    def _():
        h_ref[...] = _rms(xs_ref[...], g_ref[...]).astype(BF16)

    h = h_ref[...]

    def proj(c0, width):
        return jnp.dot(h, w_ref[:, c0:c0 + width], preferred_element_type=F32)

    a = proj(0, ch)
    ga = proj(ch, ch)
    u_ref[...] = a * _sigmoid(ga)
    q = proj(2 * ch, hk)
    q_ref[...] = _silu(q).astype(BF16)
    f = proj(2 * ch + hk, hk)
    lbl = lbl_ref[...]
    e = jnp.exp(lbl - jnp.max(lbl, axis=0, keepdims=True))
    lb = e[0:1, :] / jnp.sum(e, axis=0, keepdims=True)
    fg = lb + (1.0 - lb) * _sigmoid(f)
    k_ref[...] = (1.0 - fg).astype(BF16)
    lf_ref[...] = jnp.log(fg)
    g = proj(2 * ch + 3 * hk, hk)
    gs_ref[...] = _silu(g).astype(BF16)
    v_ref[...] = proj(2 * ch + 2 * hk, hk).astype(BF16)


def _inproj(xp, xs, norm_g, w_in, lb_logits, ch, hk):
    n_p, d = xp.shape
    n_s = xs.shape[0]
    tm = min(INPROJ_TILE, n_s)
    assert n_p % tm == 0 and n_s % tm == 0
    npt, nst = n_p // tm, n_s // tm
    n = n_p + n_s
    cols = w_in.shape[1]
    cw = min(WEIGHT_STAGE_COLS, cols)
    assert cols % cw == 0
    row = lambda i: (i, 0)
    const = lambda i: (0, 0)
    outs = [jax.ShapeDtypeStruct((n, ch), F32)] + [
        jax.ShapeDtypeStruct((n, hk), dt) for dt in (BF16, BF16, F32, BF16, BF16)]
    return pl.pallas_call(
        functools.partial(_inproj_kernel, n_prompt_tiles=npt, ch=ch, hk=hk),
        grid=(npt + nst,),
        in_specs=[
            pl.BlockSpec((tm, d), lambda i: (jnp.minimum(i, npt - 1), 0)),
            pl.BlockSpec((tm, d), lambda i: (jnp.maximum(i - npt, 0), 0)),
            pl.BlockSpec((1, d), const),
            pl.BlockSpec(memory_space=pl.ANY),
            pl.BlockSpec(lb_logits.shape, const),
        ],
        out_specs=[pl.BlockSpec((tm, ch), row)] + [pl.BlockSpec((tm, hk), row)] * 5,
        out_shape=outs,
        scratch_shapes=[pltpu.VMEM((tm, d), BF16), pltpu.VMEM((d, cols), BF16),
                        pltpu.VMEM((2, d, cw), F32), pltpu.SemaphoreType.DMA((2,))],
        compiler_params=_params(("arbitrary",)),
        name="inproj",
    )(xp, xs, norm_g, w_in, lb_logits)


def _ln_silu(c, g, b):
    mu = jnp.mean(c, axis=-1, keepdims=True)
    d = c - mu
    var = jnp.mean(d * d, axis=-1, keepdims=True)
    return _silu(d * lax.rsqrt(var + EPS) * g + b)


def _conv_prompt_kernel(halo_ref, cur_ref, w_ref, b_ref, lg_ref, lb_ref, c_ref, ext_ref, acc_ref, *, width):
    t = pl.program_id(1)
    tt, ch = cur_ref.shape
    halo = halo_ref[...]
    ext_ref[0:CONV_HALO, :] = jnp.where(t == 0, jnp.zeros_like(halo), halo)
    ext_ref[CONV_HALO:, :] = cur_ref[...]
    off = CONV_HALO - (width - 1)
    rows = min(CONV_ROWS, tt)
    for l in range(ch // LANES):
        ls = slice(l * LANES, (l + 1) * LANES)
        wl = w_ref[:, ls]
        bl = b_ref[:, ls]
        for r0 in range(0, tt, rows):
            acc = jnp.broadcast_to(bl, (rows, LANES))
            for res in range(SUBLANES):
                extra = SUBLANES if res else 0
                part = None
                for a in range((off + width - 1) // SUBLANES + 1):
                    j = SUBLANES * a + res - off
                    if 0 <= j < width:
                        lo = r0 + SUBLANES * a
                        term = ext_ref[lo:lo + rows + extra, ls] * wl[j:j + 1, :]
                        part = term if part is None else part + term
                acc = acc + part[res:res + rows]
            acc_ref[r0:r0 + rows, ls] = acc
    c_ref[...] = _ln_silu(acc_ref[...], lg_ref[...], lb_ref[...]).astype(BF16)


def _conv_prompt(u, n_batch, seq, w_dw, b_dw, ln_g, ln_b):
    ch = u.shape[1]
    width = w_dw.shape[0]
    tt = min(CONV_TILE, seq)
    assert seq % tt == 0 and tt % CONV_HALO == 0 and width - 1 <= CONV_HALO
    nt = seq // tt
    hb = tt // CONV_HALO
    const = lambda b, t: (0, 0)
    return pl.pallas_call(
        functools.partial(_conv_prompt_kernel, width=width),
        grid=(n_batch, nt),
        in_specs=[
            pl.BlockSpec((CONV_HALO, ch), lambda b, t: (jnp.maximum((b * nt + t) * hb - 1, 0), 0)),
            pl.BlockSpec((tt, ch), lambda b, t: (b * nt + t, 0)),
            pl.BlockSpec((width, ch), const),
            pl.BlockSpec((1, ch), const),
            pl.BlockSpec((1, ch), const),
            pl.BlockSpec((1, ch), const),
        ],
        out_specs=pl.BlockSpec((tt, ch), lambda b, t: (b * nt + t, 0)),
        out_shape=jax.ShapeDtypeStruct((n_batch * seq, ch), BF16),
        scratch_shapes=[pltpu.VMEM((CONV_HALO + tt, ch), F32), pltpu.VMEM((tt, ch), F32)],
        compiler_params=_params(("arbitrary", "arbitrary")),
        name="conv_prompt",
    )(u, u, w_dw, b_dw, ln_g, ln_b)


def _conv_sample_kernel(state_ref, u_ref, w_ref, b_ref, lg_ref, lb_ref, c_ref, new_ref):
    hist = state_ref.shape[0]
    steps = u_ref.shape[0]
    for r in range(hist - steps):
        new_ref[r] = state_ref[r + steps]
    for s in range(steps):
        new_ref[hist - steps + s] = u_ref[s]
    for t in range(steps):
        acc = jnp.broadcast_to(b_ref[...], u_ref.shape[1:])
        for r in range(t, hist):
            acc = acc + state_ref[r] * w_ref[r - t:r - t + 1, :]
        for s in range(t + 1):
            acc = acc + u_ref[s] * w_ref[hist - t + s:hist - t + s + 1, :]
        c_ref[t] = _ln_silu(acc, lg_ref[...], lb_ref[...]).astype(BF16)


def _conv_sample(u_t, state_t, w_dw, b_dw, ln_g, ln_b):
    hist, n_batch, ch = state_t.shape
    width = w_dw.shape[0]
    steps = u_t.shape[0]
    assert hist == width - 1 and steps <= hist
    bb = min(SAMPLE_CONV_BATCH_BLOCK, n_batch)
    assert n_batch % bb == 0
    c2 = lambda i: (0, 0)
    blk = lambda i: (0, i, 0)
    return pl.pallas_call(
        _conv_sample_kernel,
        grid=(n_batch // bb,),
        in_specs=[
            pl.BlockSpec((hist, bb, ch), blk),
            pl.BlockSpec((steps, bb, ch), blk),
            pl.BlockSpec((width, ch), c2),
            pl.BlockSpec((1, ch), c2),
            pl.BlockSpec((1, ch), c2),
            pl.BlockSpec((1, ch), c2),
        ],
        out_specs=[pl.BlockSpec((steps, bb, ch), blk), pl.BlockSpec((hist, bb, ch), blk)],
        out_shape=[jax.ShapeDtypeStruct((steps, n_batch, ch), BF16),
                   jax.ShapeDtypeStruct((hist, n_batch, ch), F32)],
        compiler_params=_params(("arbitrary",)),
        name="conv_sample",
    )(state_t, u_t, w_dw, b_dw, ln_g, ln_b)


def _chunk_constants(c):
    levels = int(np.log2(c))
    assert 2 ** levels == c
    t = np.arange(c)[:, None]
    j = np.arange(c)[None, :]
    mats = [(j <= t)]
    right = np.zeros((levels, c, LANES), np.float32)
    left = np.zeros((levels, c, LANES), np.float32)
    same = np.zeros((levels, c, c), np.float32)
    for lv in range(levels):
        half = c >> (lv + 1)
        split = (t // (2 * half)) * (2 * half) + half
        is_right = t >= split
        mats.append(np.where(is_right, (j >= split) & (j <= t), (j > t) & (j < split)))
        right[lv] = is_right
        left[lv] = ~is_right
        same[lv] = (t // (2 * half)) == (j // (2 * half))
    mats.append(j > t)
    dm = np.concatenate(mats, axis=0).astype(np.float32)
    dm = np.concatenate([dm, dm, dm], axis=1)
    return (jnp.asarray(dm, BF16), jnp.asarray(right), jnp.asarray(left), jnp.asarray(same))


def _hgrn_prompt_kernel(q_ref, k_ref, lf_ref, v_ref, gs_ref, gn_ref, dm_ref, rm_ref, lm_ref, bm_ref,
                        o_ref, s_ref, oi_ref, qe_ref, dec_ref, kv_ref, st_ref, *, chunk):
    seq, width = q_ref.shape
    hp = width // HEAD_DIM
    levels = rm_ref.shape[0]
    n_chunks = seq // chunk
    nt = (((1,), (1,)), ((), ()))
    tn = (((0,), (0,)), ((), ()))
    row_i = lax.broadcasted_iota(jnp.int32, (chunk, chunk), 0)
    col_i = lax.broadcasted_iota(jnp.int32, (chunk, chunk), 1)

    def intra(chains, robust):
        n = len(chains)
        hs = [slice(h * HEAD_DIM, (h + 1) * HEAD_DIM) for _, h in chains]
        rows = [pl.ds(pl.multiple_of(c * chunk, chunk), chunk) for c, _ in chains]
        q = [q_ref[rows[i], hs[i]].astype(F32) for i in range(n)]
        k = [k_ref[rows[i], hs[i]].astype(F32) for i in range(n)]
        v = [v_ref[rows[i], hs[i]] for i in range(n)]
        lsp = [jnp.concatenate(_split3(lf_ref[rows[i], hs[i]]), axis=0) for i in range(n)]

        def decay_sums(i, dm):
            return jnp.dot(dm, lsp[i], preferred_element_type=F32)

        if robust:
            ex = [jnp.exp(decay_sums(i, dm_ref[...])) for i in range(n)]
            e_cum = [e[0:chunk] for e in ex]
            e_tail = [e[(levels + 1) * chunk:(levels + 2) * chunk] for e in ex]
            att = [jnp.where(row_i == col_i, jnp.sum(q[i] * k[i], axis=-1, keepdims=True), 0.0) for i in range(n)]
            for lv in range(levels):
                for i in range(n):
                    e = ex[i][(lv + 1) * chunk:(lv + 2) * chunk]
                    ql = (q[i] * e * rm_ref[lv]).astype(BF16)
                    kl = (k[i] * e * lm_ref[lv]).astype(BF16)
                    att[i] = att[i] + bm_ref[lv] * lax.dot_general(ql, kl, nt, preferred_element_type=F32)
            qe = [(q[i] * e_cum[i]).astype(BF16) for i in range(n)]
        else:
            sums = [decay_sums(i, dm_ref[0:chunk, :]) for i in range(n)]
            tails = [s[chunk - 1:chunk, :] - s for s in sums]
            e_cum = [jnp.exp(s) for s in sums]
            e_tail = [jnp.exp(t) for t in tails]
            qe = [(q[i] * e_cum[i]).astype(BF16) for i in range(n)]
            kn = [(k[i] * jnp.exp(-sums[i])).astype(BF16) for i in range(n)]
            att = [lax.dot_general(qe[i], kn[i], nt, preferred_element_type=F32) for i in range(n)]
            att = [jnp.where(row_i >= col_i, a, 0.0) for a in att]
        kv = [lax.dot_general(v[i], (k[i] * e_tail[i]).astype(BF16), tn, preferred_element_type=F32) for i in range(n)]
        oi = [jnp.dot(att[i].astype(BF16), v[i], preferred_element_type=F32) for i in range(n)]
        for i, (c, h) in enumerate(chains):
            kv_ref[c, h] = kv[i]
            oi_ref[rows[i], hs[i]] = oi[i]
            qe_ref[rows[i], hs[i]] = qe[i]
            dec_ref[c, :, hs[i]] = e_cum[i][chunk - 1:chunk, :]

    def finish(chains):
        n = len(chains)
        hs = [slice(h * HEAD_DIM, (h + 1) * HEAD_DIM) for _, h in chains]
        rows = [pl.ds(pl.multiple_of(c * chunk, chunk), chunk) for c, _ in chains]
        inter = [lax.dot_general(qe_ref[rows[i], hs[i]], st_ref[c, h], nt, preferred_element_type=F32)
                 for i, (c, h) in enumerate(chains)]
        o = [oi_ref[rows[i], hs[i]] + inter[i] for i in range(n)]
        o = [_rms(o[i], gn_ref[...]) * gs_ref[rows[i], hs[i]].astype(F32) for i in range(n)]
        for i in range(n):
            o_ref[rows[i], hs[i]] = o[i].astype(BF16)

    def run(robust):
        group = min(HGRN_CHUNK_GROUP, n_chunks)

        def intra_body(g, carry):
            intra([(g * group + u, h) for u in range(group) for h in range(hp)], robust)
            return carry
        lax.fori_loop(0, n_chunks // group, intra_body, 0)

        def scan_body(c, sts):
            new = []
            for h in range(hp):
                st_ref[c, h] = sts[h].astype(BF16)
                new.append(sts[h] * dec_ref[c, :, h * HEAD_DIM:(h + 1) * HEAD_DIM] + kv_ref[c, h])
            return tuple(new)
        sts = lax.fori_loop(0, n_chunks, scan_body, tuple(jnp.zeros((HEAD_DIM, HEAD_DIM), F32) for _ in range(hp)))
        for h in range(hp):
            s_ref[0, h] = sts[h].T

        fgroup = min(HGRN_FINISH_GROUP, n_chunks)

        def finish_body(g, carry):
            finish([(g * fgroup + u, h) for u in range(fgroup) for h in range(hp)])
            return carry
        lax.fori_loop(0, n_chunks // fgroup, finish_body, 0)

    lf_all = lf_ref[...].reshape(n_chunks, chunk, width)
    slowest = jnp.min(jnp.sum(lf_all, axis=1))
    fast = slowest >= FAST_DECAY_LIMIT
    pl.when(fast)(lambda: run(False))
    pl.when(jnp.logical_not(fast))(lambda: run(True))


def _hgrn_prompt(q, k, lf, v, gs, g_norm, n_batch, seq, heads):
    chunk = HGRN_CHUNK if seq % HGRN_CHUNK == 0 else seq
    dm, rm, lm, bm = _chunk_constants(chunk)
    hp = min(HGRN_HEADS_PER_STEP, heads)
    assert heads % hp == 0
    n_chunks = seq // chunk
    tok = pl.BlockSpec((seq, hp * HEAD_DIM), lambda b, h: (b, h))
    c2 = lambda b, h: (0, 0)
    c3 = lambda b, h: (0, 0, 0)
    return pl.pallas_call(
        functools.partial(_hgrn_prompt_kernel, chunk=chunk),
        grid=(n_batch, heads // hp),
        in_specs=[tok, tok, tok, tok, tok,
                  pl.BlockSpec((1, HEAD_DIM), c2),
                  pl.BlockSpec(dm.shape, c2),
                  pl.BlockSpec(rm.shape, c3), pl.BlockSpec(lm.shape, c3), pl.BlockSpec(bm.shape, c3)],
        out_specs=[tok, pl.BlockSpec((1, hp, HEAD_DIM, HEAD_DIM), lambda b, h: (b, h, 0, 0))],
        out_shape=[jax.ShapeDtypeStruct((n_batch * seq, heads * HEAD_DIM), BF16),
                   jax.ShapeDtypeStruct((n_batch, heads, HEAD_DIM, HEAD_DIM), F32)],
        scratch_shapes=[pltpu.VMEM((seq, hp * HEAD_DIM), F32),
                        pltpu.VMEM((seq, hp * HEAD_DIM), BF16),
                        pltpu.VMEM((n_chunks, 1, hp * HEAD_DIM), F32),
                        pltpu.VMEM((n_chunks, hp, HEAD_DIM, HEAD_DIM), F32),
                        pltpu.VMEM((n_chunks, hp, HEAD_DIM, HEAD_DIM), BF16)],
        compiler_params=_params(("arbitrary", "arbitrary")),
        name="hgrn_prompt",
    )(q, k, lf, v, gs, g_norm, dm, rm, lm, bm)


def _hgrn_sample_kernel(q_ref, k_ref, lf_ref, v_ref, gs_ref, gn_ref, s0_ref, o_ref, s_ref, inter_ref, *, steps):
    rows, width = q_ref.shape
    heads = width // HEAD_DIM
    q = q_ref[...].astype(F32)
    k = k_ref[...].astype(F32)
    v = v_ref[...].astype(F32)
    lf = lf_ref[...]
    step = lax.broadcasted_iota(jnp.int32, (rows, 1), 0) & (steps - 1)

    def back(x, d):
        return pltpu.roll(x, d, 0)

    cum = lf
    for d in range(1, steps):
        cum = cum + jnp.where(step >= d, back(lf, d), 0.0)
    tail = jnp.zeros_like(lf)
    for d in range(1, steps):
        tail = tail + jnp.where(step + d < steps, pltpu.roll(lf, rows - d, 0), 0.0)

    def head_sum(x):
        return [jnp.sum(x[:, h * HEAD_DIM:(h + 1) * HEAD_DIM], axis=-1, keepdims=True) for h in range(heads)]

    def head_scale(cols, x):
        return jnp.concatenate([cols[h] * x[:, h * HEAD_DIM:(h + 1) * HEAD_DIM] for h in range(heads)], axis=1)

    intra = head_scale(head_sum(q * k), v)
    for d in range(1, steps):
        ok = step >= d
        rel = jnp.where(ok, cum - back(cum, d), 0.0)
        w = jnp.where(ok, q * back(k, d) * jnp.exp(rel), 0.0)
        intra = intra + head_scale(head_sum(w), back(v, d))

    qe = (q * jnp.exp(cum)).astype(BF16)
    kd = k * jnp.exp(tail)
    total = jnp.exp(cum)
    per = SUBLANES // steps
    grp = lax.broadcasted_iota(jnp.int32, (SUBLANES, 1), 0)
    tn = (((0,), (0,)), ((), ()))
    for b in range(rows // steps):
        r8 = (b // per) * SUBLANES
        lo = (b % per) * steps
        mine = (grp >= lo) & (grp < lo + steps)
        spare = (lo + steps) % SUBLANES
        for h in range(heads):
            hs = slice(h * HEAD_DIM, (h + 1) * HEAD_DIM)
            s0 = s0_ref[b, h]
            res = jnp.dot(qe[r8:r8 + SUBLANES, hs], s0.astype(BF16), preferred_element_type=F32)
            inter_ref[b * steps:(b + 1) * steps, hs] = res[lo:lo + steps]
            d1, d2, d3 = _split3(total[r8 + lo + steps - 1:r8 + lo + steps, hs])
            dec = jnp.where(grp == spare, d1.astype(F32),
                            jnp.where(grp == spare + 1, d2.astype(F32),
                                      jnp.where(grp == spare + 2, d3.astype(F32), 0.0)))
            lhs = jnp.where(mine, kd[r8:r8 + SUBLANES, hs], dec).astype(BF16)
            vb = jnp.where(mine, v[r8:r8 + SUBLANES, hs], 0.0)
            ones = jnp.where(mine, 0.0, 1.0) * jnp.ones((SUBLANES, HEAD_DIM), F32)
            rhs = jnp.concatenate([vb, ones], axis=1).astype(BF16)
            upd = lax.dot_general(lhs, rhs, tn, preferred_element_type=F32)
            s_ref[b, h] = upd[:, HEAD_DIM:] * s0 + upd[:, :HEAD_DIM]
    o = intra + inter_ref[...]
    gn = gn_ref[...]
    o = jnp.concatenate([_rms(o[:, h * HEAD_DIM:(h + 1) * HEAD_DIM], gn) for h in range(heads)], axis=1)
    o_ref[...] = (o * gs_ref[...].astype(F32)).astype(BF16)


def _hgrn_sample(q, k, lf, v, gs, g_norm, s0, row0, steps):
    n_batch, heads = s0.shape[:2]
    width = heads * HEAD_DIM
    assert steps & (steps - 1) == 0 and SUBLANES - steps >= 3
    bb = min(SAMPLE_BATCH_BLOCK, n_batch)
    rows = bb * steps
    assert n_batch % bb == 0 and rows % SUBLANES == 0 and row0 % rows == 0
    blk0 = row0 // rows
    tok = pl.BlockSpec((rows, width), lambda i: (blk0 + i, 0))
    st = pl.BlockSpec((bb, heads, HEAD_DIM, HEAD_DIM), lambda i: (i, 0, 0, 0))
    return pl.pallas_call(
        functools.partial(_hgrn_sample_kernel, steps=steps),
        grid=(n_batch // bb,),
        in_specs=[tok, tok, tok, tok, tok, pl.BlockSpec((1, HEAD_DIM), lambda i: (0, 0)), st],
        out_specs=[pl.BlockSpec((rows, width), lambda i: (i, 0)), st],
        out_shape=[jax.ShapeDtypeStruct((n_batch * steps, width), BF16),
                   jax.ShapeDtypeStruct(s0.shape, F32)],
        scratch_shapes=[pltpu.VMEM((rows, width), F32)],
        compiler_params=_params(("arbitrary",)),
        name="hgrn_sample",
    )(q, k, lf, v, gs, g_norm, s0)


def _outproj_kernel(cp_ref, cs_ref, op_ref, os_ref, xp_ref, xs_ref, w_hbm, g_ref, wr_ref, br_ref, tri_ref,
                    x1_ref, h2_ref, ri_ref, rw_ref, cnt_ref, run_ref, w_ref, stage_ref, wsem,
                    *, n_prompt_tiles, n_experts):
    i = pl.program_id(0)
    is_p = i < n_prompt_tiles
    ch = cp_ref.shape[1]

    @pl.when(i == 0)
    def _():
        run_ref[...] = jnp.zeros_like(run_ref)
        _load_weight_as_bf16(w_hbm, w_ref, stage_ref, wsem)

    def mix(c_ref, o_ref, x_ref):
        y = jnp.dot(c_ref[...], w_ref[0:ch, :], preferred_element_type=F32)
        y = y + jnp.dot(o_ref[...], w_ref[ch:, :], preferred_element_type=F32)
        x1_ref[...] = x_ref[...] + y

    pl.when(is_p)(lambda: mix(cp_ref, op_ref, xp_ref))
    pl.when(jnp.logical_not(is_p))(lambda: mix(cs_ref, os_ref, xs_ref))

    h2 = _rms(x1_ref[...], g_ref[...])
    h2_ref[...] = h2
    a1, a2, _ = _split3(h2)
    p1 = jnp.dot(a1, wr_ref[...], preferred_element_type=F32)
    p2 = jnp.dot(a2, wr_ref[...], preferred_element_type=F32)
    logits = p1 + pltpu.roll(p1, LANES // 2, 1) + p2 + br_ref[...]
    tm = logits.shape[0]
    lane = lax.broadcasted_iota(jnp.int32, (tm, LANES), 1)
    lane_f = lane.astype(F32)
    neg = jnp.float32(-jnp.inf)

    def top(x):
        m = jnp.max(x, axis=-1, keepdims=True)
        idx = jnp.min(jnp.where(x == m, lane_f, float(LANES)), axis=-1, keepdims=True)
        return m, idx.astype(jnp.int32)

    is_group = (lane >= n_experts) & (lane < n_experts + N_GROUPS)
    gl = jnp.where(is_group, logits, neg)
    gmax, gidx = top(gl)
    p_top = 1.0 / jnp.sum(jnp.exp(gl - gmax), axis=-1, keepdims=True)
    g_lo = (gidx - n_experts) * EXPERTS_PER_GROUP
    el = jnp.where((lane >= g_lo) & (lane < g_lo + EXPERTS_PER_GROUP), logits, neg)
    v1, e1 = top(el)
    v2, e2 = top(jnp.where(lane == e1, neg, el))
    t = jnp.exp(v2 - v1)
    w1 = p_top / (1.0 + t)
    w2 = p_top * t / (1.0 + t)
    hot = ((lane == e1) | (lane == e2)).astype(F32)
    before = run_ref[...] + jnp.dot(tri_ref[...], hot.astype(BF16), preferred_element_type=F32)
    r1 = jnp.sum(jnp.where(lane == e1, before, 0.0), axis=-1, keepdims=True)
    r2 = jnp.sum(jnp.where(lane == e2, before, 0.0), axis=-1, keepdims=True)
    run_ref[...] = run_ref[...] + jnp.sum(hot, axis=0, keepdims=True)
    cnt_ref[...] = run_ref[...]
    info = jnp.where(lane == 0, e1.astype(F32), jnp.where(lane == 1, e2.astype(F32),
                     jnp.where(lane == 2, r1, jnp.where(lane == 3, r2, 0.0))))
    ri_ref[...] = info.T[0:SUBLANES, :].astype(jnp.int32)
    rw_ref[...] = jnp.where(lane == 0, w1, jnp.where(lane == 1, w2, 0.0))


def _outproj(c_p, c_s, o_p, o_s, xp, xs, w_out, norm_g, w_router3, b_router, n_experts):
    n_p, d = xp.shape
    n_s = xs.shape[0]
    tm = min(TOKEN_TILE, n_s)
    npt, nst = n_p // tm, n_s // tm
    n = n_p + n_s
    ch = c_p.shape[1]
    hv = o_p.shape[1]
    cw = min(WEIGHT_STAGE_COLS, d)
    assert w_out.shape == (ch + hv, d) and d % cw == 0
    tri = jnp.asarray(np.tril(np.ones((tm, tm), np.float32), -1), BF16)
    pidx = lambda i: (jnp.minimum(i, npt - 1), 0)
    sidx = lambda i: (jnp.maximum(i - npt, 0), 0)
    row = lambda i: (i, 0)
    c2 = lambda i: (0, 0)
    return pl.pallas_call(
        functools.partial(_outproj_kernel, n_prompt_tiles=npt, n_experts=n_experts),
        grid=(npt + nst,),
        in_specs=[
            pl.BlockSpec((tm, ch), pidx), pl.BlockSpec((tm, ch), sidx),
            pl.BlockSpec((tm, hv), pidx), pl.BlockSpec((tm, hv), sidx),
            pl.BlockSpec((tm, d), pidx), pl.BlockSpec((tm, d), sidx),
            pl.BlockSpec(memory_space=pl.ANY),
            pl.BlockSpec((1, d), c2),
            pl.BlockSpec((d, LANES), c2),
            pl.BlockSpec((1, LANES), c2),
            pl.BlockSpec((tm, tm), c2),
        ],
        out_specs=[pl.BlockSpec((tm, d), row), pl.BlockSpec((tm, d), row),
                   pl.BlockSpec((SUBLANES, tm), lambda i: (0, i)), pl.BlockSpec((tm, LANES), row),
                   pl.BlockSpec((1, LANES), c2)],
        out_shape=[jax.ShapeDtypeStruct((n, d), F32), jax.ShapeDtypeStruct((n, d), F32),
                   jax.ShapeDtypeStruct((SUBLANES, n), jnp.int32), jax.ShapeDtypeStruct((n, LANES), F32),
                   jax.ShapeDtypeStruct((1, LANES), F32)],
        scratch_shapes=[pltpu.VMEM((1, LANES), F32), pltpu.VMEM((ch + hv, d), BF16),
                        pltpu.VMEM((2, ch + hv, cw), F32), pltpu.SemaphoreType.DMA((2,))],
        compiler_params=_params(("arbitrary",)),
        name="outproj",
    )(c_p, c_s, o_p, o_s, xp, xs, w_out, norm_g, w_router3, b_router, tri)


def _experts_kernel(dest_ref, pad_ref, te_ref, tf_ref, tn_ref, ts_ref, nu_ref, h2_ref, wg_ref, wu_ref, wd_ref, y_ref,
                    x_ref, wgf_ref, wuf_ref, wdf_ref, wgb_ref, wub_ref, wdb_ref, src_ref, sem, wsem):
    i = pl.program_id(0)
    n_used = nu_ref[0]
    rows = x_ref.shape[1]

    def invert_routing():
        n_tokens = dest_ref.shape[0] // 2
        n_exp = pad_ref.shape[0] // 2
        spread = (1 << (n_tokens.bit_length() - 1)) - 1

        def fill_expert(e, carry):
            def fill(p, c):
                src_ref[p] = p & spread
                return c
            lax.fori_loop(pad_ref[e], pad_ref[n_exp + e], fill, 0)
            return carry
        lax.fori_loop(0, n_exp, fill_expert, 0)

        def place(g, carry):
            for u in range(GATHER_UNROLL):
                t = g * GATHER_UNROLL + u
                src_ref[dest_ref[t]] = t
                src_ref[dest_ref[n_tokens + t]] = t
            return carry
        lax.fori_loop(0, n_tokens // GATHER_UNROLL, place, 0)

    def row_copy(tile, slot, r):
        return pltpu.make_async_copy(h2_ref.at[pl.ds(src_ref[tile * rows + r], 1)],
                                     x_ref.at[slot, pl.ds(r, 1)], sem.at[slot])

    def gather(tile, slot):
        def start(g, carry):
            for u in range(GATHER_UNROLL):
                row_copy(tile, slot, g * GATHER_UNROLL + u).start()
            return carry
        lax.fori_loop(0, rows // GATHER_UNROLL, start, 0)

    def weight_copies(e, slot):
        return [pltpu.make_async_copy(src.at[e], dst.at[slot], wsem.at[slot])
                for src, dst in ((wg_ref, wgf_ref), (wu_ref, wuf_ref), (wd_ref, wdf_ref))]

    @pl.when(i == 0)
    def _():
        for c in weight_copies(te_ref[0], 0):
            c.start(priority=1)
        invert_routing()
        gather(0, 0)

    @pl.when(tf_ref[i] == 1)
    def _():
        wslot = ts_ref[i]

        @pl.when(tn_ref[i] >= 0)
        def _():
            for c in weight_copies(tn_ref[i], 1 - wslot):
                c.start(priority=1)
        for c in weight_copies(te_ref[i], wslot):
            c.wait()
        wgb_ref[...] = wgf_ref[wslot].astype(BF16)
        wub_ref[...] = wuf_ref[wslot].astype(BF16)
        wdb_ref[...] = wdf_ref[wslot].astype(BF16)

    def tile_step(prefetch):
        slot = i % 2

        def wait(g, carry):
            for u in range(GATHER_UNROLL):
                row_copy(i, slot, g * GATHER_UNROLL + u).wait()
            return carry
        lax.fori_loop(0, rows // GATHER_UNROLL, wait, 0)

        f = wgb_ref.shape[1]
        d = wdb_ref.shape[1]
        fc, dc = min(MXU_COLS, f), min(MXU_COLS, d)
        n_pieces = 2 * (f // fc) + d // dc
        per_piece = -(-rows // n_pieces)
        issued = [0]

        def issue_share():
            if not prefetch:
                return
            for r in range(issued[0], min(issued[0] + per_piece, rows)):
                row_copy(i + 1, 1 - slot, r).start()
            issued[0] = min(issued[0] + per_piece, rows)

        x = x_ref[slot].astype(BF16)
        hid = []
        for j in range(f // fc):
            cs = slice(j * fc, (j + 1) * fc)
            hg = jnp.dot(x, wgb_ref[:, cs], preferred_element_type=F32)
            issue_share()
            hu = jnp.dot(x, wub_ref[:, cs], preferred_element_type=F32)
            issue_share()
            hid.append((_silu(hg) * hu).astype(BF16))
        hid = jnp.concatenate(hid, axis=1)
        for j in range(d // dc):
            cs = slice(j * dc, (j + 1) * dc)
            y_ref[:, cs] = jnp.dot(hid, wdb_ref[:, cs], preferred_element_type=F32)
            issue_share()

    pl.when(i + 1 < n_used)(lambda: tile_step(True))
    pl.when(i + 1 == n_used)(lambda: tile_step(False))

    @pl.when(i >= n_used)
    def _():
        y_ref[...] = jnp.zeros_like(y_ref)


def _experts(dest_flat, pad_rows, n_rows, tile_expert, tile_first, tile_next, tile_slot, n_used, h2,
             w_gate, w_up, w_down):
    d = h2.shape[1]
    f = w_gate.shape[2]
    n_tiles = n_rows // EXPERT_TILE
    any_space = pl.BlockSpec(memory_space=pl.ANY)
    return pl.pallas_call(
        _experts_kernel,
        grid_spec=pltpu.PrefetchScalarGridSpec(
            num_scalar_prefetch=7, grid=(n_tiles,),
            in_specs=[any_space, any_space, any_space, any_space],
            out_specs=pl.BlockSpec((EXPERT_TILE, d), lambda i, *_: (i, 0)),
            scratch_shapes=[pltpu.VMEM((2, EXPERT_TILE, d), F32),
                            pltpu.VMEM((2, d, f), F32), pltpu.VMEM((2, d, f), F32), pltpu.VMEM((2, f, d), F32),
                            pltpu.VMEM((d, f), BF16), pltpu.VMEM((d, f), BF16), pltpu.VMEM((f, d), BF16),
                            pltpu.SMEM((n_rows,), jnp.int32),
                            pltpu.SemaphoreType.DMA((2,)), pltpu.SemaphoreType.DMA((2,))],
        ),
        out_shape=jax.ShapeDtypeStruct((n_rows, d), F32),
        compiler_params=_params(("arbitrary",)),
        name="experts",
    )(dest_flat, pad_rows, tile_expert, tile_first, tile_next, tile_slot, n_used, h2, w_gate, w_up, w_down)


def _combine_kernel(dest_ref, ys_ref, x1_ref, rw_ref, g_ref, yp_ref, ysm_ref, buf_ref, sem, *, n_prompt_tiles):
    i = pl.program_id(0)
    tm = x1_ref.shape[0]

    n_tokens = dest_ref.shape[0] // 2
    n_steps = pl.num_programs(0)

    def copy(tile, r, slot):
        return pltpu.make_async_copy(ys_ref.at[pl.ds(dest_ref[slot * n_tokens + tile * tm + r], 1)],
                                     buf_ref.at[tile % 2, slot, pl.ds(r, 1)], sem.at[tile % 2])

    def gather(tile):
        def start(r, carry):
            copy(tile, r, 0).start(priority=0)
            copy(tile, r, 1).start(priority=1)
            return carry
        lax.fori_loop(0, tm, start, 0, unroll=GATHER_UNROLL)

    @pl.when(i == 0)
    def _():
        gather(0)

    def gather_unrolled(tile, par):
        for r in range(tm):
            for slot in range(2):
                pltpu.make_async_copy(ys_ref.at[pl.ds(dest_ref[slot * n_tokens + tile * tm + r], 1)],
                                      buf_ref.at[par, slot, pl.ds(r, 1)], sem.at[par]).start(priority=slot)

    for par in range(2):
        pl.when((i + 1 < n_steps) & ((i + 1) % 2 == par))(functools.partial(gather_unrolled, i + 1, par))

    def wait(r, carry):
        copy(i, r, 0).wait()
        copy(i, r, 1).wait()
        return carry

    lax.fori_loop(0, tm, wait, 0, unroll=GATHER_UNROLL)
    rw = rw_ref[...]
    par = i % 2
    x2 = x1_ref[...] + rw[:, 0:1] * buf_ref[par, 0] + rw[:, 1:2] * buf_ref[par, 1]
    y = _rms(x2, g_ref[...])

    @pl.when(i < n_prompt_tiles)
    def _():
        yp_ref[...] = y

    @pl.when(i >= n_prompt_tiles)
    def _():
        ysm_ref[...] = y


def _combine(dest_flat, ys, x1, rw, norm_g, n_p, n_s):
    n, d = x1.shape
    tm = min(CONV_TILE, n_s)
    npt, nst = n_p // tm, n_s // tm
    row = lambda i, dest: (i, 0)
    return pl.pallas_call(
        functools.partial(_combine_kernel, n_prompt_tiles=npt),
        grid_spec=pltpu.PrefetchScalarGridSpec(
            num_scalar_prefetch=1, grid=(npt + nst,),
            in_specs=[pl.BlockSpec(memory_space=pl.ANY),
                      pl.BlockSpec((tm, d), row), pl.BlockSpec((tm, LANES), row),
                      pl.BlockSpec((1, d), lambda i, dest: (0, 0))],
            out_specs=[pl.BlockSpec((tm, d), lambda i, dest: (jnp.minimum(i, npt - 1), 0)),
                       pl.BlockSpec((tm, d), lambda i, dest: (jnp.maximum(i - npt, 0), 0))],
            scratch_shapes=[pltpu.VMEM((2, 2, tm, d), F32), pltpu.SemaphoreType.DMA((2,))],
        ),
        out_shape=[jax.ShapeDtypeStruct((n_p, d), F32), jax.ShapeDtypeStruct((n_s, d), F32)],
        compiler_params=_params(("arbitrary",)),
        name="combine",
    )(dest_flat, ys, x1, rw, norm_g)


def kernel(x_prompt, x_sample, state_conv, state_hgrn, norm_mix, w_in, w_dw, b_dw, ln_conv_g, ln_conv_b, lb_logits, hgrn_norm_g, w_out, norm_ffn, w_router_group, b_router_group, w_router_expert, b_router_expert, w_exp_gate, w_exp_up, w_exp_down, norm_final):
    assert w_in.shape[0] == 1, "single-layer trunk"
    n_batch, seq, d = x_prompt.shape
    s_batch, steps, _ = x_sample.shape
    ch = w_dw.shape[-1]
    hk = lb_logits.shape[-1]
    heads = hk // HEAD_DIM
    n_experts = w_exp_gate.shape[1]
    assert n_experts == N_GROUPS * EXPERTS_PER_GROUP and n_experts + N_GROUPS <= LANES // 2
    n_p, n_s = n_batch * seq, s_batch * steps
    n = n_p + n_s
    xp = x_prompt.reshape(n_p, d)
    xs = x_sample.reshape(n_s, d)

    u, q, k, lf, v, gs = _inproj(xp, xs, norm_mix, w_in[0], lb_logits, ch, hk)

    c_p = _conv_prompt(u, n_batch, seq, w_dw[0], b_dw, ln_conv_g, ln_conv_b)
    u_t = u[n_p:].reshape(s_batch, steps, ch).transpose(1, 0, 2)
    c_t, new_conv_t = _conv_sample(u_t, state_conv[0].transpose(1, 0, 2), w_dw[0], b_dw, ln_conv_g, ln_conv_b)
    c_s = c_t.transpose(1, 0, 2).reshape(n_s, ch)
    new_conv_sample = new_conv_t.transpose(1, 0, 2)[None]
    hist = state_conv.shape[2]
    new_conv_prompt = jnp.stack([u[(b + 1) * seq - hist:(b + 1) * seq] for b in range(n_batch)])

    o_p, hgrn_p = _hgrn_prompt(q, k, lf, v, gs, hgrn_norm_g, n_batch, seq, heads)
    o_s, hgrn_s = _hgrn_sample(q, k, lf, v, gs, hgrn_norm_g, state_hgrn[0], n_p, steps)

    w_r = jnp.concatenate([w_router_expert[0], w_router_group[0]], axis=1)
    w_r = jnp.pad(w_r, ((0, 0), (0, LANES // 2 - w_r.shape[1])))
    r1 = w_r.astype(BF16)
    r2 = (w_r - r1.astype(F32)).astype(BF16)
    b_r = jnp.pad(jnp.concatenate([b_router_expert[0], b_router_group[0]]), (0, LANES - n_experts - N_GROUPS))[None]
    x1, h2, ri, rw, counts = _outproj(c_p, c_s, o_p, o_s, xp, xs, w_out[0], norm_ffn,
                                      jnp.concatenate([r1, r2], axis=1), b_r, n_experts)

    cnt = counts[0, :n_experts].astype(jnp.int32)
    tiles_per = (cnt + EXPERT_TILE - 1) // EXPERT_TILE
    tile_end = jnp.cumsum(tiles_per)
    row_start = (tile_end - tiles_per) * EXPERT_TILE
    n_tiles = (2 * n) // EXPERT_TILE + n_experts
    is_e = ri[0:2, :, None] == jnp.arange(n_experts, dtype=jnp.int32)
    dest = jnp.sum(jnp.where(is_e, row_start, 0), axis=-1) + ri[2:4]
    dest_flat = dest.reshape(-1)
    n_used = tile_end[-1:]
    tid = jnp.minimum(jnp.arange(n_tiles, dtype=jnp.int32), n_used - 1)
    tile_expert = jnp.sum((tile_end[None, :] <= tid[:, None]).astype(jnp.int32), axis=1)
    prev = jnp.concatenate([jnp.full((1,), -1, jnp.int32), tile_expert[:-1]])
    tile_first = ((tile_expert != prev) & (jnp.arange(n_tiles) < n_used)).astype(jnp.int32)
    eid = jnp.arange(n_experts, dtype=jnp.int32)
    used = tiles_per > 0
    later = used[None, :] & (eid[None, :] > eid[:, None])
    next_expert = jnp.min(jnp.where(later, eid[None, :], n_experts), axis=1)
    next_expert = jnp.where(next_expert == n_experts, -1, next_expert)
    parity = (jnp.cumsum(used.astype(jnp.int32)) - 1) % 2
    tile_is = tile_expert[:, None] == eid[None, :]
    tile_next = jnp.sum(jnp.where(tile_is, next_expert[None, :], 0), axis=1).astype(jnp.int32)
    tile_slot = jnp.sum(jnp.where(tile_is, parity[None, :], 0), axis=1).astype(jnp.int32)

    pad_rows = jnp.concatenate([row_start + cnt, tile_end * EXPERT_TILE]).astype(jnp.int32)
    ys_sorted = _experts(dest_flat, pad_rows, n_tiles * EXPERT_TILE, tile_expert, tile_first, tile_next, tile_slot,
                         n_used.astype(jnp.int32), h2, w_exp_gate[0], w_exp_up[0], w_exp_down[0])
    y_p, y_s = _combine(dest_flat, ys_sorted, x1, rw, norm_final[None], n_p, n_s)

    return (y_p.reshape(n_batch, seq, d), y_s.reshape(s_batch, steps, d),
            new_conv_prompt[None], hgrn_p[None], new_conv_sample, hgrn_s[None])
```

```python
import functools

import numpy as np
import jax
import jax.numpy as jnp
from jax import lax
from jax.experimental import pallas as pl
from jax.experimental.pallas import tpu as pltpu

F32 = jnp.float32
BF16 = jnp.bfloat16
EPS = 1e-6
LANES = 128
SUBLANES = 8
HEAD_DIM = 128
HGRN_CHUNK = 64
HGRN_HEADS_PER_STEP = 2
HGRN_FINISH_GROUP = 16
HGRN_CHUNK_GROUP = 8
FAST_DECAY_LIMIT = -60.0
N_GROUPS = 4
EXPERTS_PER_GROUP = 8
VMEM_LIMIT = 56 * 1024 * 1024
TOKEN_TILE = 256
INPROJ_TILE = 256
WEIGHT_STAGE_COLS = 512
CONV_TILE = 256
CONV_HALO = 32
CONV_ROWS = 64
EXPERT_TILE = 256
SAMPLE_BATCH_BLOCK = 8
SAMPLE_CONV_BATCH_BLOCK = 16
GATHER_UNROLL = 8
MXU_COLS = 256


def _sigmoid(x):
    return 1.0 / (1.0 + jnp.exp(-x))


def _silu(x):
    return x * _sigmoid(x)


def _rms(x, g):
    return x * lax.rsqrt(jnp.mean(x * x, axis=-1, keepdims=True) + EPS) * g


def _split3(x):
    h1 = x.astype(BF16)
    r1 = x - h1.astype(F32)
    h2 = r1.astype(BF16)
    h3 = (r1 - h2.astype(F32)).astype(BF16)
    return h1, h2, h3


def _params(sem, flags=None):
    return pltpu.CompilerParams(dimension_semantics=sem, vmem_limit_bytes=VMEM_LIMIT, flags=flags)


def _load_weight_as_bf16(w_hbm, w_ref, stage_ref, wsem):
    cw = stage_ref.shape[2]
    n_cols = w_ref.shape[1] // cw
    copies = [pltpu.make_async_copy(w_hbm.at[:, pl.ds(c * cw, cw)], stage_ref.at[c % 2], wsem.at[c % 2])
              for c in range(n_cols)]
    copies[0].start()
    for c in range(n_cols):
        if c + 1 < n_cols:
            copies[c + 1].start()
        copies[c].wait()
        w_ref[:, c * cw:(c + 1) * cw] = stage_ref[c % 2].astype(BF16)


def _inproj_kernel(xp_ref, xs_ref, g_ref, w_hbm, lbl_ref,
                   u_ref, q_ref, k_ref, lf_ref, v_ref, gs_ref, h_ref, w_ref, stage_ref, wsem,
                   *, n_prompt_tiles, ch, hk):
    i = pl.program_id(0)
    pl.when(i == 0)(lambda: _load_weight_as_bf16(w_hbm, w_ref, stage_ref, wsem))

    @pl.when(i < n_prompt_tiles)
    def _():
        h_ref[...] = _rms(xp_ref[...], g_ref[...]).astype(BF16)

    @pl.when(i >= n_prompt_tiles)
    def _():
        h_ref[...] = _rms(xs_ref[...], g_ref[...]).astype(BF16)

    h = h_ref[...]

    def proj(c0, width):
        return jnp.dot(h, w_ref[:, c0:c0 + width], preferred_element_type=F32)

    a = proj(0, ch)
    ga = proj(ch, ch)
    u_ref[...] = a * _sigmoid(ga)
    q = proj(2 * ch, hk)
    q_ref[...] = _silu(q).astype(BF16)
    f = proj(2 * ch + hk, hk)
    lbl = lbl_ref[...]
    e = jnp.exp(lbl - jnp.max(lbl, axis=0, keepdims=True))
    lb = e[0:1, :] / jnp.sum(e, axis=0, keepdims=True)
    fg = lb + (1.0 - lb) * _sigmoid(f)
    k_ref[...] = (1.0 - fg).astype(BF16)
    lf_ref[...] = jnp.log(fg)
    g = proj(2 * ch + 3 * hk, hk)
    gs_ref[...] = _silu(g).astype(BF16)
    v_ref[...] = proj(2 * ch + 2 * hk, hk).astype(BF16)


def _inproj(xp, xs, norm_g, w_in, lb_logits, ch, hk):
    n_p, d = xp.shape
    n_s = xs.shape[0]
    tm = min(INPROJ_TILE, n_s)
    assert n_p % tm == 0 and n_s % tm == 0
    npt, nst = n_p // tm, n_s // tm
    n = n_p + n_s
    cols = w_in.shape[1]
    cw = min(WEIGHT_STAGE_COLS, cols)
    assert cols % cw == 0
    row = lambda i: (i, 0)
    const = lambda i: (0, 0)
    outs = [jax.ShapeDtypeStruct((n, ch), F32)] + [
        jax.ShapeDtypeStruct((n, hk), dt) for dt in (BF16, BF16, F32, BF16, BF16)]
    return pl.pallas_call(
        functools.partial(_inproj_kernel, n_prompt_tiles=npt, ch=ch, hk=hk),
        grid=(npt + nst,),
        in_specs=[
            pl.BlockSpec((tm, d), lambda i: (jnp.minimum(i, npt - 1), 0)),
            pl.BlockSpec((tm, d), lambda i: (jnp.maximum(i - npt, 0), 0)),
            pl.BlockSpec((1, d), const),
            pl.BlockSpec(memory_space=pl.ANY),
            pl.BlockSpec(lb_logits.shape, const),
        ],
        out_specs=[pl.BlockSpec((tm, ch), row)] + [pl.BlockSpec((tm, hk), row)] * 5,
        out_shape=outs,
        scratch_shapes=[pltpu.VMEM((tm, d), BF16), pltpu.VMEM((d, cols), BF16),
                        pltpu.VMEM((2, d, cw), F32), pltpu.SemaphoreType.DMA((2,))],
        compiler_params=_params(("arbitrary",)),
        name="inproj",
    )(xp, xs, norm_g, w_in, lb_logits)


def _ln_silu(c, g, b):
    mu = jnp.mean(c, axis=-1, keepdims=True)
    d = c - mu
    var = jnp.mean(d * d, axis=-1, keepdims=True)
    return _silu(d * lax.rsqrt(var + EPS) * g + b)


def _conv_prompt_kernel(halo_ref, cur_ref, w_ref, b_ref, lg_ref, lb_ref, c_ref, ext_ref, acc_ref, *, width):
    t = pl.program_id(1)
    tt, ch = cur_ref.shape
    halo = halo_ref[...]
    ext_ref[0:CONV_HALO, :] = jnp.where(t == 0, jnp.zeros_like(halo), halo)
    ext_ref[CONV_HALO:, :] = cur_ref[...]
    off = CONV_HALO - (width - 1)
    rows = min(CONV_ROWS, tt)
    for l in range(ch // LANES):
        ls = slice(l * LANES, (l + 1) * LANES)
        wl = w_ref[:, ls]
        bl = b_ref[:, ls]
        for r0 in range(0, tt, rows):
            acc = jnp.broadcast_to(bl, (rows, LANES))
            for res in range(SUBLANES):
                extra = SUBLANES if res else 0
                part = None
                for a in range((off + width - 1) // SUBLANES + 1):
                    j = SUBLANES * a + res - off
                    if 0 <= j < width:
                        lo = r0 + SUBLANES * a
                        term = ext_ref[lo:lo + rows + extra, ls] * wl[j:j + 1, :]
                        part = term if part is None else part + term
                acc = acc + part[res:res + rows]
            acc_ref[r0:r0 + rows, ls] = acc
    c_ref[...] = _ln_silu(acc_ref[...], lg_ref[...], lb_ref[...]).astype(BF16)


def _conv_prompt(u, n_batch, seq, w_dw, b_dw, ln_g, ln_b):
    ch = u.shape[1]
    width = w_dw.shape[0]
    tt = min(CONV_TILE, seq)
    assert seq % tt == 0 and tt % CONV_HALO == 0 and width - 1 <= CONV_HALO
    nt = seq // tt
    hb = tt // CONV_HALO
    const = lambda b, t: (0, 0)
    return pl.pallas_call(
        functools.partial(_conv_prompt_kernel, width=width),
        grid=(n_batch, nt),
        in_specs=[
            pl.BlockSpec((CONV_HALO, ch), lambda b, t: (jnp.maximum((b * nt + t) * hb - 1, 0), 0)),
            pl.BlockSpec((tt, ch), lambda b, t: (b * nt + t, 0)),
            pl.BlockSpec((width, ch), const),
            pl.BlockSpec((1, ch), const),
            pl.BlockSpec((1, ch), const),
            pl.BlockSpec((1, ch), const),
        ],
        out_specs=pl.BlockSpec((tt, ch), lambda b, t: (b * nt + t, 0)),
        out_shape=jax.ShapeDtypeStruct((n_batch * seq, ch), BF16),
        scratch_shapes=[pltpu.VMEM((CONV_HALO + tt, ch), F32), pltpu.VMEM((tt, ch), F32)],
        compiler_params=_params(("arbitrary", "arbitrary")),
        name="conv_prompt",
    )(u, u, w_dw, b_dw, ln_g, ln_b)


def _conv_sample_kernel(state_ref, u_ref, w_ref, b_ref, lg_ref, lb_ref, c_ref, new_ref):
    hist = state_ref.shape[0]
    steps = u_ref.shape[0]
    for r in range(hist - steps):
        new_ref[r] = state_ref[r + steps]
    for s in range(steps):
        new_ref[hist - steps + s] = u_ref[s]
    for t in range(steps):
        acc = jnp.broadcast_to(b_ref[...], u_ref.shape[1:])
        for r in range(t, hist):
            acc = acc + state_ref[r] * w_ref[r - t:r - t + 1, :]
        for s in range(t + 1):
            acc = acc + u_ref[s] * w_ref[hist - t + s:hist - t + s + 1, :]
        c_ref[t] = _ln_silu(acc, lg_ref[...], lb_ref[...]).astype(BF16)


def _conv_sample(u_t, state_t, w_dw, b_dw, ln_g, ln_b):
    hist, n_batch, ch = state_t.shape
    width = w_dw.shape[0]
    steps = u_t.shape[0]
    assert hist == width - 1 and steps <= hist
    bb = min(SAMPLE_CONV_BATCH_BLOCK, n_batch)
    assert n_batch % bb == 0
    c2 = lambda i: (0, 0)
    blk = lambda i: (0, i, 0)
    return pl.pallas_call(
        _conv_sample_kernel,
        grid=(n_batch // bb,),
        in_specs=[
            pl.BlockSpec((hist, bb, ch), blk),
            pl.BlockSpec((steps, bb, ch), blk),
            pl.BlockSpec((width, ch), c2),
            pl.BlockSpec((1, ch), c2),
            pl.BlockSpec((1, ch), c2),
            pl.BlockSpec((1, ch), c2),
        ],
        out_specs=[pl.BlockSpec((steps, bb, ch), blk), pl.BlockSpec((hist, bb, ch), blk)],
        out_shape=[jax.ShapeDtypeStruct((steps, n_batch, ch), BF16),
                   jax.ShapeDtypeStruct((hist, n_batch, ch), F32)],
        compiler_params=_params(("arbitrary",)),
        name="conv_sample",
    )(state_t, u_t, w_dw, b_dw, ln_g, ln_b)


def _chunk_constants(c):
    levels = int(np.log2(c))
    assert 2 ** levels == c
    t = np.arange(c)[:, None]
    j = np.arange(c)[None, :]
    mats = [(j <= t)]
    right = np.zeros((levels, c, LANES), np.float32)
    left = np.zeros((levels, c, LANES), np.float32)
    same = np.zeros((levels, c, c), np.float32)
    for lv in range(levels):
        half = c >> (lv + 1)
        split = (t // (2 * half)) * (2 * half) + half
        is_right = t >= split
        mats.append(np.where(is_right, (j >= split) & (j <= t), (j > t) & (j < split)))
        right[lv] = is_right
        left[lv] = ~is_right
        same[lv] = (t // (2 * half)) == (j // (2 * half))
    mats.append(j > t)
    dm = np.concatenate(mats, axis=0).astype(np.float32)
    dm = np.concatenate([dm, dm, dm], axis=1)
    return (jnp.asarray(dm, BF16), jnp.asarray(right), jnp.asarray(left), jnp.asarray(same))


def _hgrn_prompt_kernel(q_ref, k_ref, lf_ref, v_ref, gs_ref, gn_ref, dm_ref, rm_ref, lm_ref, bm_ref,
                        o_ref, s_ref, oi_ref, qe_ref, dec_ref, kv_ref, st_ref, *, chunk):
    seq, width = q_ref.shape
    hp = width // HEAD_DIM
    levels = rm_ref.shape[0]
    n_chunks = seq // chunk
    nt = (((1,), (1,)), ((), ()))
    tn = (((0,), (0,)), ((), ()))
    row_i = lax.broadcasted_iota(jnp.int32, (chunk, chunk), 0)
    col_i = lax.broadcasted_iota(jnp.int32, (chunk, chunk), 1)

    def intra(chains, robust):
        n = len(chains)
        hs = [slice(h * HEAD_DIM, (h + 1) * HEAD_DIM) for _, h in chains]
        rows = [pl.ds(pl.multiple_of(c * chunk, chunk), chunk) for c, _ in chains]
        q = [q_ref[rows[i], hs[i]].astype(F32) for i in range(n)]
        k = [k_ref[rows[i], hs[i]].astype(F32) for i in range(n)]
        v = [v_ref[rows[i], hs[i]] for i in range(n)]
        lsp = [jnp.concatenate(_split3(lf_ref[rows[i], hs[i]]), axis=0) for i in range(n)]

        def decay_sums(i, dm):
            return jnp.dot(dm, lsp[i], preferred_element_type=F32)

        if robust:
            ex = [jnp.exp(decay_sums(i, dm_ref[...])) for i in range(n)]
            e_cum = [e[0:chunk] for e in ex]
            e_tail = [e[(levels + 1) * chunk:(levels + 2) * chunk] for e in ex]
            att = [jnp.where(row_i == col_i, jnp.sum(q[i] * k[i], axis=-1, keepdims=True), 0.0) for i in range(n)]
            for lv in range(levels):
                for i in range(n):
                    e = ex[i][(lv + 1) * chunk:(lv + 2) * chunk]
                    ql = (q[i] * e * rm_ref[lv]).astype(BF16)
                    kl = (k[i] * e * lm_ref[lv]).astype(BF16)
                    att[i] = att[i] + bm_ref[lv] * lax.dot_general(ql, kl, nt, preferred_element_type=F32)
            qe = [(q[i] * e_cum[i]).astype(BF16) for i in range(n)]
        else:
            sums = [decay_sums(i, dm_ref[0:chunk, :]) for i in range(n)]
            tails = [s[chunk - 1:chunk, :] - s for s in sums]
            e_cum = [jnp.exp(s) for s in sums]
            e_tail = [jnp.exp(t) for t in tails]
            qe = [(q[i] * e_cum[i]).astype(BF16) for i in range(n)]
            kn = [(k[i] * jnp.exp(-sums[i])).astype(BF16) for i in range(n)]
            att = [lax.dot_general(qe[i], kn[i], nt, preferred_element_type=F32) for i in range(n)]
            att = [jnp.where(row_i >= col_i, a, 0.0) for a in att]
        kv = [lax.dot_general(v[i], (k[i] * e_tail[i]).astype(BF16), tn, preferred_element_type=F32) for i in range(n)]
        oi = [jnp.dot(att[i].astype(BF16), v[i], preferred_element_type=F32) for i in range(n)]
        for i, (c, h) in enumerate(chains):
            kv_ref[c, h] = kv[i]
            oi_ref[rows[i], hs[i]] = oi[i]
            qe_ref[rows[i], hs[i]] = qe[i]
            dec_ref[c, :, hs[i]] = e_cum[i][chunk - 1:chunk, :]

    def finish(chains):
        n = len(chains)
        hs = [slice(h * HEAD_DIM, (h + 1) * HEAD_DIM) for _, h in chains]
        rows = [pl.ds(pl.multiple_of(c * chunk, chunk), chunk) for c, _ in chains]
        inter = [lax.dot_general(qe_ref[rows[i], hs[i]], st_ref[c, h], nt, preferred_element_type=F32)
                 for i, (c, h) in enumerate(chains)]
        o = [oi_ref[rows[i], hs[i]] + inter[i] for i in range(n)]
        o = [_rms(o[i], gn_ref[...]) * gs_ref[rows[i], hs[i]].astype(F32) for i in range(n)]
        for i in range(n):
            o_ref[rows[i], hs[i]] = o[i].astype(BF16)

    def run(robust):
        group = min(HGRN_CHUNK_GROUP, n_chunks)

        def intra_body(g, carry):
            intra([(g * group + u, h) for u in range(group) for h in range(hp)], robust)
            return carry
        lax.fori_loop(0, n_chunks // group, intra_body, 0)

        def scan_body(c, sts):
            new = []
            for h in range(hp):
                st_ref[c, h] = sts[h].astype(BF16)
                new.append(sts[h] * dec_ref[c, :, h * HEAD_DIM:(h + 1) * HEAD_DIM] + kv_ref[c, h])
            return tuple(new)
        sts = lax.fori_loop(0, n_chunks, scan_body, tuple(jnp.zeros((HEAD_DIM, HEAD_DIM), F32) for _ in range(hp)))
        for h in range(hp):
            s_ref[0, h] = sts[h].T

        fgroup = min(HGRN_FINISH_GROUP, n_chunks)

        def finish_body(g, carry):
            finish([(g * fgroup + u, h) for u in range(fgroup) for h in range(hp)])
            return carry
        lax.fori_loop(0, n_chunks // fgroup, finish_body, 0)

    lf_all = lf_ref[...].reshape(n_chunks, chunk, width)
    slowest = jnp.min(jnp.sum(lf_all, axis=1))
    fast = slowest >= FAST_DECAY_LIMIT
    pl.when(fast)(lambda: run(False))
    pl.when(jnp.logical_not(fast))(lambda: run(True))


def _hgrn_prompt(q, k, lf, v, gs, g_norm, n_batch, seq, heads):
    chunk = HGRN_CHUNK if seq % HGRN_CHUNK == 0 else seq
    dm, rm, lm, bm = _chunk_constants(chunk)
    hp = min(HGRN_HEADS_PER_STEP, heads)
    assert heads % hp == 0
    n_chunks = seq // chunk
    tok = pl.BlockSpec((seq, hp * HEAD_DIM), lambda b, h: (b, h))
    c2 = lambda b, h: (0, 0)
    c3 = lambda b, h: (0, 0, 0)
    return pl.pallas_call(
        functools.partial(_hgrn_prompt_kernel, chunk=chunk),
        grid=(n_batch, heads // hp),
        in_specs=[tok, tok, tok, tok, tok,
                  pl.BlockSpec((1, HEAD_DIM), c2),
                  pl.BlockSpec(dm.shape, c2),
                  pl.BlockSpec(rm.shape, c3), pl.BlockSpec(lm.shape, c3), pl.BlockSpec(bm.shape, c3)],
        out_specs=[tok, pl.BlockSpec((1, hp, HEAD_DIM, HEAD_DIM), lambda b, h: (b, h, 0, 0))],
        out_shape=[jax.ShapeDtypeStruct((n_batch * seq, heads * HEAD_DIM), BF16),
                   jax.ShapeDtypeStruct((n_batch, heads, HEAD_DIM, HEAD_DIM), F32)],
        scratch_shapes=[pltpu.VMEM((seq, hp * HEAD_DIM), F32),
                        pltpu.VMEM((seq, hp * HEAD_DIM), BF16),
                        pltpu.VMEM((n_chunks, 1, hp * HEAD_DIM), F32),
                        pltpu.VMEM((n_chunks, hp, HEAD_DIM, HEAD_DIM), F32),
                        pltpu.VMEM((n_chunks, hp, HEAD_DIM, HEAD_DIM), BF16)],
        compiler_params=_params(("arbitrary", "arbitrary")),
        name="hgrn_prompt",
    )(q, k, lf, v, gs, g_norm, dm, rm, lm, bm)


def _hgrn_sample_kernel(q_ref, k_ref, lf_ref, v_ref, gs_ref, gn_ref, s0_ref, o_ref, s_ref, inter_ref, *, steps):
    rows, width = q_ref.shape
    heads = width // HEAD_DIM
    q = q_ref[...].astype(F32)
    k = k_ref[...].astype(F32)
    v = v_ref[...].astype(F32)
    lf = lf_ref[...]
    step = lax.broadcasted_iota(jnp.int32, (rows, 1), 0) & (steps - 1)

    def back(x, d):
        return pltpu.roll(x, d, 0)

    cum = lf
    for d in range(1, steps):
        cum = cum + jnp.where(step >= d, back(lf, d), 0.0)
    tail = jnp.zeros_like(lf)
    for d in range(1, steps):
        tail = tail + jnp.where(step + d < steps, pltpu.roll(lf, rows - d, 0), 0.0)

    def head_sum(x):
        return [jnp.sum(x[:, h * HEAD_DIM:(h + 1) * HEAD_DIM], axis=-1, keepdims=True) for h in range(heads)]

    def head_scale(cols, x):
        return jnp.concatenate([cols[h] * x[:, h * HEAD_DIM:(h + 1) * HEAD_DIM] for h in range(heads)], axis=1)

    intra = head_scale(head_sum(q * k), v)
    for d in range(1, steps):
        ok = step >= d
        rel = jnp.where(ok, cum - back(cum, d), 0.0)
        w = jnp.where(ok, q * back(k, d) * jnp.exp(rel), 0.0)
        intra = intra + head_scale(head_sum(w), back(v, d))

    qe = (q * jnp.exp(cum)).astype(BF16)
    kd = k * jnp.exp(tail)
    total = jnp.exp(cum)
    per = SUBLANES // steps
    grp = lax.broadcasted_iota(jnp.int32, (SUBLANES, 1), 0)
    tn = (((0,), (0,)), ((), ()))
    for b in range(rows // steps):
        r8 = (b // per) * SUBLANES
        lo = (b % per) * steps
        mine = (grp >= lo) & (grp < lo + steps)
        spare = (lo + steps) % SUBLANES
        for h in range(heads):
            hs = slice(h * HEAD_DIM, (h + 1) * HEAD_DIM)
            s0 = s0_ref[b, h]
            res = jnp.dot(qe[r8:r8 + SUBLANES, hs], s0.astype(BF16), preferred_element_type=F32)
            inter_ref[b * steps:(b + 1) * steps, hs] = res[lo:lo + steps]
            d1, d2, d3 = _split3(total[r8 + lo + steps - 1:r8 + lo + steps, hs])
            dec = jnp.where(grp == spare, d1.astype(F32),
                            jnp.where(grp == spare + 1, d2.astype(F32),
                                      jnp.where(grp == spare + 2, d3.astype(F32), 0.0)))
            lhs = jnp.where(mine, kd[r8:r8 + SUBLANES, hs], dec).astype(BF16)
            vb = jnp.where(mine, v[r8:r8 + SUBLANES, hs], 0.0)
            ones = jnp.where(mine, 0.0, 1.0) * jnp.ones((SUBLANES, HEAD_DIM), F32)
            rhs = jnp.concatenate([vb, ones], axis=1).astype(BF16)
            upd = lax.dot_general(lhs, rhs, tn, preferred_element_type=F32)
            s_ref[b, h] = upd[:, HEAD_DIM:] * s0 + upd[:, :HEAD_DIM]
    o = intra + inter_ref[...]
    gn = gn_ref[...]
    o = jnp.concatenate([_rms(o[:, h * HEAD_DIM:(h + 1) * HEAD_DIM], gn) for h in range(heads)], axis=1)
    o_ref[...] = (o * gs_ref[...].astype(F32)).astype(BF16)


def _hgrn_sample(q, k, lf, v, gs, g_norm, s0, row0, steps):
    n_batch, heads = s0.shape[:2]
    width = heads * HEAD_DIM
    assert steps & (steps - 1) == 0 and SUBLANES - steps >= 3
    bb = min(SAMPLE_BATCH_BLOCK, n_batch)
    rows = bb * steps
    assert n_batch % bb == 0 and rows % SUBLANES == 0 and row0 % rows == 0
    blk0 = row0 // rows
    tok = pl.BlockSpec((rows, width), lambda i: (blk0 + i, 0))
    st = pl.BlockSpec((bb, heads, HEAD_DIM, HEAD_DIM), lambda i: (i, 0, 0, 0))
    return pl.pallas_call(
        functools.partial(_hgrn_sample_kernel, steps=steps),
        grid=(n_batch // bb,),
        in_specs=[tok, tok, tok, tok, tok, pl.BlockSpec((1, HEAD_DIM), lambda i: (0, 0)), st],
        out_specs=[pl.BlockSpec((rows, width), lambda i: (i, 0)), st],
        out_shape=[jax.ShapeDtypeStruct((n_batch * steps, width), BF16),
                   jax.ShapeDtypeStruct(s0.shape, F32)],
        scratch_shapes=[pltpu.VMEM((rows, width), F32)],
        compiler_params=_params(("arbitrary",)),
        name="hgrn_sample",
    )(q, k, lf, v, gs, g_norm, s0)


def _outproj_kernel(cp_ref, cs_ref, op_ref, os_ref, xp_ref, xs_ref, w_hbm, g_ref, wr_ref, br_ref, tri_ref,
                    x1_ref, h2_ref, ri_ref, rw_ref, cnt_ref, run_ref, w_ref, stage_ref, wsem,
                    *, n_prompt_tiles, n_experts):
    i = pl.program_id(0)
    is_p = i < n_prompt_tiles
    ch = cp_ref.shape[1]

    @pl.when(i == 0)
    def _():
        run_ref[...] = jnp.zeros_like(run_ref)
        _load_weight_as_bf16(w_hbm, w_ref, stage_ref, wsem)

    def mix(c_ref, o_ref, x_ref):
        y = jnp.dot(c_ref[...], w_ref[0:ch, :], preferred_element_type=F32)
        y = y + jnp.dot(o_ref[...], w_ref[ch:, :], preferred_element_type=F32)
        x1_ref[...] = x_ref[...] + y

    pl.when(is_p)(lambda: mix(cp_ref, op_ref, xp_ref))
    pl.when(jnp.logical_not(is_p))(lambda: mix(cs_ref, os_ref, xs_ref))

    h2 = _rms(x1_ref[...], g_ref[...])
    h2_ref[...] = h2
    a1, a2, _ = _split3(h2)
    p1 = jnp.dot(a1, wr_ref[...], preferred_element_type=F32)
    p2 = jnp.dot(a2, wr_ref[...], preferred_element_type=F32)
    logits = p1 + pltpu.roll(p1, LANES // 2, 1) + p2 + br_ref[...]
    tm = logits.shape[0]
    lane = lax.broadcasted_iota(jnp.int32, (tm, LANES), 1)
    lane_f = lane.astype(F32)
    neg = jnp.float32(-jnp.inf)

    def top(x):
        m = jnp.max(x, axis=-1, keepdims=True)
        idx = jnp.min(jnp.where(x == m, lane_f, float(LANES)), axis=-1, keepdims=True)
        return m, idx.astype(jnp.int32)

    is_group = (lane >= n_experts) & (lane < n_experts + N_GROUPS)
    gl = jnp.where(is_group, logits, neg)
    gmax, gidx = top(gl)
    p_top = 1.0 / jnp.sum(jnp.exp(gl - gmax), axis=-1, keepdims=True)
    g_lo = (gidx - n_experts) * EXPERTS_PER_GROUP
    el = jnp.where((lane >= g_lo) & (lane < g_lo + EXPERTS_PER_GROUP), logits, neg)
    v1, e1 = top(el)
    v2, e2 = top(jnp.where(lane == e1, neg, el))
    t = jnp.exp(v2 - v1)
    w1 = p_top / (1.0 + t)
    w2 = p_top * t / (1.0 + t)
    hot = ((lane == e1) | (lane == e2)).astype(F32)
    before = run_ref[...] + jnp.dot(tri_ref[...], hot.astype(BF16), preferred_element_type=F32)
    r1 = jnp.sum(jnp.where(lane == e1, before, 0.0), axis=-1, keepdims=True)
    r2 = jnp.sum(jnp.where(lane == e2, before, 0.0), axis=-1, keepdims=True)
    run_ref[...] = run_ref[...] + jnp.sum(hot, axis=0, keepdims=True)
    cnt_ref[...] = run_ref[...]
    info = jnp.where(lane == 0, e1.astype(F32), jnp.where(lane == 1, e2.astype(F32),
                     jnp.where(lane == 2, r1, jnp.where(lane == 3, r2, 0.0))))
    ri_ref[...] = info.T[0:SUBLANES, :].astype(jnp.int32)
    rw_ref[...] = jnp.where(lane == 0, w1, jnp.where(lane == 1, w2, 0.0))


def _outproj(c_p, c_s, o_p, o_s, xp, xs, w_out, norm_g, w_router3, b_router, n_experts):
    n_p, d = xp.shape
    n_s = xs.shape[0]
    tm = min(TOKEN_TILE, n_s)
    npt, nst = n_p // tm, n_s // tm
    n = n_p + n_s
    ch = c_p.shape[1]
    hv = o_p.shape[1]
    cw = min(WEIGHT_STAGE_COLS, d)
    assert w_out.shape == (ch + hv, d) and d % cw == 0
    tri = jnp.asarray(np.tril(np.ones((tm, tm), np.float32), -1), BF16)
    pidx = lambda i: (jnp.minimum(i, npt - 1), 0)
    sidx = lambda i: (jnp.maximum(i - npt, 0), 0)
    row = lambda i: (i, 0)
    c2 = lambda i: (0, 0)
    return pl.pallas_call(
        functools.partial(_outproj_kernel, n_prompt_tiles=npt, n_experts=n_experts),
        grid=(npt + nst,),
        in_specs=[
            pl.BlockSpec((tm, ch), pidx), pl.BlockSpec((tm, ch), sidx),
            pl.BlockSpec((tm, hv), pidx), pl.BlockSpec((tm, hv), sidx),
            pl.BlockSpec((tm, d), pidx), pl.BlockSpec((tm, d), sidx),
            pl.BlockSpec(memory_space=pl.ANY),
            pl.BlockSpec((1, d), c2),
            pl.BlockSpec((d, LANES), c2),
            pl.BlockSpec((1, LANES), c2),
            pl.BlockSpec((tm, tm), c2),
        ],
        out_specs=[pl.BlockSpec((tm, d), row), pl.BlockSpec((tm, d), row),
                   pl.BlockSpec((SUBLANES, tm), lambda i: (0, i)), pl.BlockSpec((tm, LANES), row),
                   pl.BlockSpec((1, LANES), c2)],
        out_shape=[jax.ShapeDtypeStruct((n, d), F32), jax.ShapeDtypeStruct((n, d), F32),
                   jax.ShapeDtypeStruct((SUBLANES, n), jnp.int32), jax.ShapeDtypeStruct((n, LANES), F32),
                   jax.ShapeDtypeStruct((1, LANES), F32)],
        scratch_shapes=[pltpu.VMEM((1, LANES), F32), pltpu.VMEM((ch + hv, d), BF16),
                        pltpu.VMEM((2, ch + hv, cw), F32), pltpu.SemaphoreType.DMA((2,))],
        compiler_params=_params(("arbitrary",)),
        name="outproj",
    )(c_p, c_s, o_p, o_s, xp, xs, w_out, norm_g, w_router3, b_router, tri)


def _experts_kernel(dest_ref, pad_ref, te_ref, tf_ref, tn_ref, ts_ref, nu_ref, h2_ref, wg_ref, wu_ref, wd_ref, y_ref,
                    x_ref, wgf_ref, wuf_ref, wdf_ref, wgb_ref, wub_ref, wdb_ref, src_ref, sem, wsem):
    i = pl.program_id(0)
    n_used = nu_ref[0]
    rows = x_ref.shape[1]

    def invert_routing():
        n_tokens = dest_ref.shape[0] // 2
        n_exp = pad_ref.shape[0] // 2
        spread = (1 << (n_tokens.bit_length() - 1)) - 1

        def fill_expert(e, carry):
            def fill(g, c):
                for u in range(GATHER_UNROLL):
                    p = g * GATHER_UNROLL + u
                    src_ref[p] = p & spread
                return c
            lax.fori_loop(pad_ref[e] // GATHER_UNROLL, pad_ref[n_exp + e] // GATHER_UNROLL, fill, 0)
            return carry
        lax.fori_loop(0, n_exp, fill_expert, 0)

        def place(g, carry):
            for u in range(GATHER_UNROLL):
                t = g * GATHER_UNROLL + u
                src_ref[dest_ref[t]] = t
                src_ref[dest_ref[n_tokens + t]] = t
            return carry
        lax.fori_loop(0, n_tokens // GATHER_UNROLL, place, 0)

    def row_copy(tile, slot, r):
        return pltpu.make_async_copy(h2_ref.at[pl.ds(src_ref[tile * rows + r], 1)],
                                     x_ref.at[slot, pl.ds(r, 1)], sem.at[slot])

    def gather(tile, slot):
        def start(g, carry):
            for u in range(GATHER_UNROLL):
                row_copy(tile, slot, g * GATHER_UNROLL + u).start()
            return carry
        lax.fori_loop(0, rows // GATHER_UNROLL, start, 0)

    def weight_copies(e, slot):
        return [pltpu.make_async_copy(src.at[e], dst.at[slot], wsem.at[slot])
                for src, dst in ((wg_ref, wgf_ref), (wu_ref, wuf_ref), (wd_ref, wdf_ref))]

    @pl.when(i == 0)
    def _():
        for c in weight_copies(te_ref[0], 0):
            c.start(priority=1)
        invert_routing()
        gather(0, 0)

    @pl.when(tf_ref[i] == 1)
    def _():
        wslot = ts_ref[i]

        @pl.when(tn_ref[i] >= 0)
        def _():
            for c in weight_copies(tn_ref[i], 1 - wslot):
                c.start(priority=1)
        for c in weight_copies(te_ref[i], wslot):
            c.wait()
        wgb_ref[...] = wgf_ref[wslot].astype(BF16)
        wub_ref[...] = wuf_ref[wslot].astype(BF16)
        wdb_ref[...] = wdf_ref[wslot].astype(BF16)

    def tile_step(prefetch):
        slot = i % 2

        def wait(g, carry):
            for u in range(GATHER_UNROLL):
                row_copy(i, slot, g * GATHER_UNROLL + u).wait()
            return carry
        lax.fori_loop(0, rows // GATHER_UNROLL, wait, 0)

        f = wgb_ref.shape[1]
        d = wdb_ref.shape[1]
        fc, dc = min(MXU_COLS, f), min(MXU_COLS, d)
        n_pieces = 2 * (f // fc) + d // dc
        per_piece = -(-rows // n_pieces)
        issued = [0]

        def issue_share():
            if not prefetch:
                return
            for r in range(issued[0], min(issued[0] + per_piece, rows)):
                row_copy(i + 1, 1 - slot, r).start()
            issued[0] = min(issued[0] + per_piece, rows)

        x = x_ref[slot].astype(BF16)
        hid = []
        for j in range(f // fc):
            cs = slice(j * fc, (j + 1) * fc)
            hg = jnp.dot(x, wgb_ref[:, cs], preferred_element_type=F32)
            issue_share()
            hu = jnp.dot(x, wub_ref[:, cs], preferred_element_type=F32)
            issue_share()
            hid.append((_silu(hg) * hu).astype(BF16))
        hid = jnp.concatenate(hid, axis=1)
        for j in range(d // dc):
            cs = slice(j * dc, (j + 1) * dc)
            y_ref[:, cs] = jnp.dot(hid, wdb_ref[:, cs], preferred_element_type=F32)
            issue_share()

    pl.when(i + 1 < n_used)(lambda: tile_step(True))
    pl.when(i + 1 == n_used)(lambda: tile_step(False))

    @pl.when(i >= n_used)
    def _():
        y_ref[...] = jnp.zeros_like(y_ref)


def _experts(dest_flat, pad_rows, n_rows, tile_expert, tile_first, tile_next, tile_slot, n_used, h2,
             w_gate, w_up, w_down):
    d = h2.shape[1]
    f = w_gate.shape[2]
    n_tiles = n_rows // EXPERT_TILE
    any_space = pl.BlockSpec(memory_space=pl.ANY)
    return pl.pallas_call(
        _experts_kernel,
        grid_spec=pltpu.PrefetchScalarGridSpec(
            num_scalar_prefetch=7, grid=(n_tiles,),
            in_specs=[any_space, any_space, any_space, any_space],
            out_specs=pl.BlockSpec((EXPERT_TILE, d), lambda i, *_: (i, 0)),
            scratch_shapes=[pltpu.VMEM((2, EXPERT_TILE, d), F32),
                            pltpu.VMEM((2, d, f), F32), pltpu.VMEM((2, d, f), F32), pltpu.VMEM((2, f, d), F32),
                            pltpu.VMEM((d, f), BF16), pltpu.VMEM((d, f), BF16), pltpu.VMEM((f, d), BF16),
                            pltpu.SMEM((n_rows,), jnp.int32),
                            pltpu.SemaphoreType.DMA((2,)), pltpu.SemaphoreType.DMA((2,))],
        ),
        out_shape=jax.ShapeDtypeStruct((n_rows, d), F32),
        compiler_params=_params(("arbitrary",)),
        name="experts",
    )(dest_flat, pad_rows, tile_expert, tile_first, tile_next, tile_slot, n_used, h2, w_gate, w_up, w_down)


def _combine_kernel(dest_ref, ys_ref, x1_ref, rw_ref, g_ref, yp_ref, ysm_ref, buf_ref, sem, *, n_prompt_tiles):
    i = pl.program_id(0)
    tm = x1_ref.shape[0]

    n_tokens = dest_ref.shape[0] // 2
    n_steps = pl.num_programs(0)

    def copy(tile, r, slot):
        return pltpu.make_async_copy(ys_ref.at[pl.ds(dest_ref[slot * n_tokens + tile * tm + r], 1)],
                                     buf_ref.at[tile % 2, slot, pl.ds(r, 1)], sem.at[tile % 2])

    def gather(tile):
        def start(r, carry):
            copy(tile, r, 0).start(priority=0)
            copy(tile, r, 1).start(priority=1)
            return carry
        lax.fori_loop(0, tm, start, 0, unroll=GATHER_UNROLL)

    @pl.when(i == 0)
    def _():
        gather(0)

    def gather_unrolled(tile, par):
        for r in range(tm):
            for slot in range(2):
                pltpu.make_async_copy(ys_ref.at[pl.ds(dest_ref[slot * n_tokens + tile * tm + r], 1)],
                                      buf_ref.at[par, slot, pl.ds(r, 1)], sem.at[par]).start(priority=slot)

    for par in range(2):
        pl.when((i + 1 < n_steps) & ((i + 1) % 2 == par))(functools.partial(gather_unrolled, i + 1, par))

    def wait(r, carry):
        copy(i, r, 0).wait()
        copy(i, r, 1).wait()
        return carry

    lax.fori_loop(0, tm, wait, 0, unroll=GATHER_UNROLL)
    rw = rw_ref[...]
    par = i % 2
    x2 = x1_ref[...] + rw[:, 0:1] * buf_ref[par, 0] + rw[:, 1:2] * buf_ref[par, 1]
    y = _rms(x2, g_ref[...])

    @pl.when(i < n_prompt_tiles)
    def _():
        yp_ref[...] = y

    @pl.when(i >= n_prompt_tiles)
    def _():
        ysm_ref[...] = y


def _combine(dest_flat, ys, x1, rw, norm_g, n_p, n_s):
    n, d = x1.shape
    tm = min(CONV_TILE, n_s)
    npt, nst = n_p // tm, n_s // tm
    row = lambda i, dest: (i, 0)
    return pl.pallas_call(
        functools.partial(_combine_kernel, n_prompt_tiles=npt),
        grid_spec=pltpu.PrefetchScalarGridSpec(
            num_scalar_prefetch=1, grid=(npt + nst,),
            in_specs=[pl.BlockSpec(memory_space=pl.ANY),
                      pl.BlockSpec((tm, d), row), pl.BlockSpec((tm, LANES), row),
                      pl.BlockSpec((1, d), lambda i, dest: (0, 0))],
            out_specs=[pl.BlockSpec((tm, d), lambda i, dest: (jnp.minimum(i, npt - 1), 0)),
                       pl.BlockSpec((tm, d), lambda i, dest: (jnp.maximum(i - npt, 0), 0))],
            scratch_shapes=[pltpu.VMEM((2, 2, tm, d), F32), pltpu.SemaphoreType.DMA((2,))],
        ),
        out_shape=[jax.ShapeDtypeStruct((n_p, d), F32), jax.ShapeDtypeStruct((n_s, d), F32)],
        compiler_params=_params(("arbitrary",)),
        name="combine",
    )(dest_flat, ys, x1, rw, norm_g)


def kernel(x_prompt, x_sample, state_conv, state_hgrn, norm_mix, w_in, w_dw, b_dw, ln_conv_g, ln_conv_b, lb_logits, hgrn_norm_g, w_out, norm_ffn, w_router_group, b_router_group, w_router_expert, b_router_expert, w_exp_gate, w_exp_up, w_exp_down, norm_final):
    assert w_in.shape[0] == 1, "single-layer trunk"
    n_batch, seq, d = x_prompt.shape
    s_batch, steps, _ = x_sample.shape
    ch = w_dw.shape[-1]
    hk = lb_logits.shape[-1]
    heads = hk // HEAD_DIM
    n_experts = w_exp_gate.shape[1]
    assert n_experts == N_GROUPS * EXPERTS_PER_GROUP and n_experts + N_GROUPS <= LANES // 2
    n_p, n_s = n_batch * seq, s_batch * steps
    n = n_p + n_s
    xp = x_prompt.reshape(n_p, d)
    xs = x_sample.reshape(n_s, d)

    u, q, k, lf, v, gs = _inproj(xp, xs, norm_mix, w_in[0], lb_logits, ch, hk)

    c_p = _conv_prompt(u, n_batch, seq, w_dw[0], b_dw, ln_conv_g, ln_conv_b)
    u_t = u[n_p:].reshape(s_batch, steps, ch).transpose(1, 0, 2)
    c_t, new_conv_t = _conv_sample(u_t, state_conv[0].transpose(1, 0, 2), w_dw[0], b_dw, ln_conv_g, ln_conv_b)
    c_s = c_t.transpose(1, 0, 2).reshape(n_s, ch)
    new_conv_sample = new_conv_t.transpose(1, 0, 2)[None]
    hist = state_conv.shape[2]
    new_conv_prompt = jnp.stack([u[(b + 1) * seq - hist:(b + 1) * seq] for b in range(n_batch)])

    o_p, hgrn_p = _hgrn_prompt(q, k, lf, v, gs, hgrn_norm_g, n_batch, seq, heads)
    o_s, hgrn_s = _hgrn_sample(q, k, lf, v, gs, hgrn_norm_g, state_hgrn[0], n_p, steps)

    w_r = jnp.concatenate([w_router_expert[0], w_router_group[0]], axis=1)
    w_r = jnp.pad(w_r, ((0, 0), (0, LANES // 2 - w_r.shape[1])))
    r1 = w_r.astype(BF16)
    r2 = (w_r - r1.astype(F32)).astype(BF16)
    b_r = jnp.pad(jnp.concatenate([b_router_expert[0], b_router_group[0]]), (0, LANES - n_experts - N_GROUPS))[None]
    x1, h2, ri, rw, counts = _outproj(c_p, c_s, o_p, o_s, xp, xs, w_out[0], norm_ffn,
                                      jnp.concatenate([r1, r2], axis=1), b_r, n_experts)

    cnt = counts[0, :n_experts].astype(jnp.int32)
    tiles_per = (cnt + EXPERT_TILE - 1) // EXPERT_TILE
    tile_end = jnp.cumsum(tiles_per)
    row_start = (tile_end - tiles_per) * EXPERT_TILE
    n_tiles = (2 * n) // EXPERT_TILE + n_experts
    is_e = ri[0:2, :, None] == jnp.arange(n_experts, dtype=jnp.int32)
    dest = jnp.sum(jnp.where(is_e, row_start, 0), axis=-1) + ri[2:4]
    dest_flat = dest.reshape(-1)
    n_used = tile_end[-1:]
    tid = jnp.minimum(jnp.arange(n_tiles, dtype=jnp.int32), n_used - 1)
    tile_expert = jnp.sum((tile_end[None, :] <= tid[:, None]).astype(jnp.int32), axis=1)
    prev = jnp.concatenate([jnp.full((1,), -1, jnp.int32), tile_expert[:-1]])
    tile_first = ((tile_expert != prev) & (jnp.arange(n_tiles) < n_used)).astype(jnp.int32)
    eid = jnp.arange(n_experts, dtype=jnp.int32)
    used = tiles_per > 0
    later = used[None, :] & (eid[None, :] > eid[:, None])
    next_expert = jnp.min(jnp.where(later, eid[None, :], n_experts), axis=1)
    next_expert = jnp.where(next_expert == n_experts, -1, next_expert)
    parity = (jnp.cumsum(used.astype(jnp.int32)) - 1) % 2
    tile_is = tile_expert[:, None] == eid[None, :]
    tile_next = jnp.sum(jnp.where(tile_is, next_expert[None, :], 0), axis=1).astype(jnp.int32)
    tile_slot = jnp.sum(jnp.where(tile_is, parity[None, :], 0), axis=1).astype(jnp.int32)

    pad_rows = jnp.concatenate([row_start + cnt, tile_end * EXPERT_TILE]).astype(jnp.int32)
    ys_sorted = _experts(dest_flat, pad_rows, n_tiles * EXPERT_TILE, tile_expert, tile_first, tile_next, tile_slot,
                         n_used.astype(jnp.int32), h2, w_exp_gate[0], w_exp_up[0], w_exp_down[0])
    y_p, y_s = _combine(dest_flat, ys_sorted, x1, rw, norm_final[None], n_p, n_s)

    return (y_p.reshape(n_batch, seq, d), y_s.reshape(s_batch, steps, d),
            new_conv_prompt[None], hgrn_p[None], new_conv_sample, hgrn_s[None])
```
